```python
import jax, jax.numpy as jnp
from jax import lax
import numpy as np

D_MODEL = 4096
BATCH = 4
SEQ = 2048
DEPTH = 2

MIX_WIDTH = D_MODEL
BRANCH = MIX_WIDTH // 4

S5_GROUP = 16
S5_GROUPS = BRANCH // S5_GROUP
S5_STATE = 64
S5_DT_MIN = 1e-3
S5_DT_MAX = 1e-1

MLA_NOPE = 128
MLA_ROPE = 64
MLA_V = 128
MLA_HEADS = BRANCH // MLA_V
MLA_Q_RANK = D_MODEL * 3 // 16
MLA_KV_RANK = D_MODEL // 8
ROPE_THETA = 10000.0
Q_BLOCK = 128
NEG_INF = -1e30

GLA_HEADS = 4
GLA_DV = BRANCH // GLA_HEADS
GLA_DK = GLA_DV // 2
GLA_GATE_RANK = 16
GLA_TAU = 16.0
GLA_CHUNK = 64

LRU_BLOCKS = 8
LRU_BLOCK_W = BRANCH // LRU_BLOCKS
LRU_CONV = 4
LRU_C = 8.0

NORM_EPS = 1e-6

IN_SIZES = (BRANCH, BRANCH,
            MLA_Q_RANK, MLA_KV_RANK, MLA_ROPE, BRANCH,
            GLA_HEADS * GLA_DK, GLA_HEADS * GLA_DK, BRANCH,
            GLA_GATE_RANK, BRANCH,
            BRANCH, BRANCH)
IN_COLS = 7 * BRANCH + MLA_Q_RANK + MLA_KV_RANK + MLA_ROPE + 2 * GLA_HEADS * GLA_DK + GLA_GATE_RANK

kernel_name = "hybrid_s5_mla_gla_rglru_parallel_heads"


def rms_norm(x, g):
    x32 = x.astype(jnp.float32)
    y = x32 * lax.rsqrt(jnp.mean(x32 * x32, axis=-1, keepdims=True) + NORM_EPS)
    return (y * g.astype(jnp.float32)).astype(x.dtype)


def split_cols(z):
    idx = np.cumsum(IN_SIZES)[:-1].tolist()
    return jnp.split(z, idx, axis=-1)


def linear_scan(a, b):
    def combine(left, right):
        a_l, b_l = left
        a_r, b_r = right
        return a_l * a_r, a_r * b_l + b_r
    _, h = lax.associative_scan(combine, (a, b), axis=1)
    return h


def s5_mixer(u, lam_re, lam_im, b_re, b_im, c_re, c_im, d, log_dt, w_glu, b_glu):
    bsz, seq, _ = u.shape
    f32 = jnp.float32
    lam = lax.complex(lam_re.astype(f32), lam_im.astype(f32))
    dt = jnp.exp(log_dt.astype(f32))[:, None]
    lam_bar = jnp.exp(lam * dt)
    b_c = lax.complex(b_re.astype(f32), b_im.astype(f32))
    b_bar = ((lam_bar - 1.0) / lam)[..., None] * b_c
    c_c = lax.complex(c_re.astype(f32), c_im.astype(f32))
    ug = u.astype(f32).reshape(bsz, seq, S5_GROUPS, S5_GROUP)
    bu = jnp.einsum('blgh,gph->blgp', ug.astype(jnp.complex64), b_bar)
    state = linear_scan(jnp.broadcast_to(lam_bar, bu.shape), bu)
    y = jnp.einsum('blgp,ghp->blgh', state, c_c).real + d.astype(f32) * ug
    y = jax.nn.gelu(y.reshape(bsz, seq, BRANCH))
    y = y * jax.nn.sigmoid(y @ w_glu.astype(f32) + b_glu.astype(f32))
    return y.astype(u.dtype)


def rope(x, positions):
    half = MLA_ROPE // 2
    inv_freq = 1.0 / (ROPE_THETA ** (jnp.arange(half, dtype=jnp.float32) / half))
    ang = positions.astype(jnp.float32)[..., None] * inv_freq
    cos = jnp.cos(ang)[:, :, None, :]
    sin = jnp.sin(ang)[:, :, None, :]
    x32 = x.astype(jnp.float32)
    x1, x2 = x32[..., :half], x32[..., half:]
    return jnp.concatenate([x1 * cos - x2 * sin, x2 * cos + x1 * sin], axis=-1).astype(x.dtype)


def mla_mixer(c_q, c_kv, k_rope, positions, q_norm_g, kv_norm_g, w_uq, w_uk, w_uv):
    bsz, seq, _ = c_q.shape
    dqk = MLA_NOPE + MLA_ROPE
    q = (rms_norm(c_q, q_norm_g) @ w_uq).reshape(bsz, seq, MLA_HEADS, dqk)
    q = jnp.concatenate([q[..., :MLA_NOPE], rope(q[..., MLA_NOPE:], positions)], axis=-1)
    c = rms_norm(c_kv, kv_norm_g)
    k_nope = (c @ w_uk).reshape(bsz, seq, MLA_HEADS, MLA_NOPE)
    v = (c @ w_uv).reshape(bsz, seq, MLA_HEADS, MLA_V)
    k_r = rope(k_rope[:, :, None, :], positions)
    k = jnp.concatenate([k_nope, jnp.broadcast_to(k_r, (bsz, seq, MLA_HEADS, MLA_ROPE))], axis=-1)
    scale = dqk ** -0.5
    n_blk = seq // Q_BLOCK
    q_blocks = q.reshape(bsz, n_blk, Q_BLOCK, MLA_HEADS, dqk).transpose(1, 0, 2, 3, 4)
    key_idx = jnp.arange(seq)

    def attend(args):
        q_blk, blk = args
        s = jnp.einsum('bqhd,bkhd->bhqk', q_blk, k).astype(jnp.float32) * scale
        q_idx = blk * Q_BLOCK + jnp.arange(Q_BLOCK)
        s = jnp.where(key_idx[None, :] <= q_idx[:, None], s, NEG_INF)
        p = jax.nn.softmax(s, axis=-1).astype(v.dtype)
        return jnp.einsum('bhqk,bkhd->bqhd', p, v)

    o = lax.map(attend, (q_blocks, jnp.arange(n_blk)))
    return o.transpose(1, 0, 2, 3, 4).reshape(bsz, seq, MLA_HEADS * MLA_V)


def gla_mixer(q, k, v, g_lr, w_g2, b_g, out_g):
    bsz, seq, _ = q.shape
    n_ch = seq // GLA_CHUNK
    f32 = jnp.float32

    def chunks(t, dim):
        return t.astype(f32).reshape(bsz, n_ch, GLA_CHUNK, GLA_HEADS, dim).transpose(0, 3, 1, 2, 4)

    log_a = jax.nn.log_sigmoid(g_lr.astype(f32) @ w_g2.astype(f32) + b_g.astype(f32)) / GLA_TAU
    qc = chunks(q, GLA_DK) * GLA_DK ** -0.5
    kc = chunks(k, GLA_DK)
    vc = chunks(v, GLA_DV)
    cum = jnp.cumsum(chunks(log_a, GLA_DK), axis=3)
    q_dec = qc * jnp.exp(cum)
    k_inv = kc * jnp.exp(-cum)
    causal = jnp.tril(jnp.ones((GLA_CHUNK, GLA_CHUNK), dtype=bool))
    att = jnp.where(causal, jnp.einsum('bhncd,bhnsd->bhncs', q_dec, k_inv), 0.0)
    o = jnp.einsum('bhncs,bhnsv->bhncv', att, vc)
    total = cum[:, :, :, -1:, :]
    kv = jnp.einsum('bhncd,bhncv->bhndv', kc * jnp.exp(total - cum), vc)
    decay = jnp.exp(total[:, :, :, 0, :])

    def step(s, inp):
        dec, kv_n = inp
        return dec[..., None] * s + kv_n, s

    s0 = jnp.zeros((bsz, GLA_HEADS, GLA_DK, GLA_DV), f32)
    _, s_prev = lax.scan(step, s0, (jnp.moveaxis(decay, 2, 0), jnp.moveaxis(kv, 2, 0)))
    o = o + jnp.einsum('bhncd,bhndv->bhncv', q_dec, jnp.moveaxis(s_prev, 0, 2))
    o = o.transpose(0, 2, 3, 1, 4).reshape(bsz, seq, GLA_HEADS, GLA_DV)
    o = rms_norm(o, out_g.reshape(GLA_HEADS, GLA_DV))
    return o.reshape(bsz, seq, BRANCH).astype(q.dtype)


def rglru_mixer(x, conv_w, conv_b, w_a, b_a, w_x, b_x, lam):
    bsz, seq, _ = x.shape
    f32 = jnp.float32
    xc = lax.conv_general_dilated(
        x, conv_w.reshape(LRU_CONV, 1, BRANCH).astype(x.dtype), window_strides=(1,),
        padding=((LRU_CONV - 1, 0),), dimension_numbers=('NWC', 'WIO', 'NWC'),
        feature_group_count=BRANCH) + conv_b
    xb = xc.astype(f32).reshape(bsz, seq, LRU_BLOCKS, LRU_BLOCK_W)
    r = jax.nn.sigmoid(jnp.einsum('blhi,hij->blhj', xb, w_a.astype(f32))
                       + b_a.astype(f32).reshape(LRU_BLOCKS, LRU_BLOCK_W))
    i = jax.nn.sigmoid(jnp.einsum('blhi,hij->blhj', xb, w_x.astype(f32))
                       + b_x.astype(f32).reshape(LRU_BLOCKS, LRU_BLOCK_W))
    log_a = -LRU_C * r * jax.nn.softplus(-lam.astype(f32).reshape(LRU_BLOCKS, LRU_BLOCK_W))
    gated = jnp.sqrt(-jnp.expm1(2.0 * log_a)) * (i * xb)
    h = linear_scan(jnp.exp(log_a), gated)
    return h.reshape(bsz, seq, BRANCH).astype(x.dtype)


def setup_inputs(seed: int = 0) -> dict:
    key = jax.random.key(seed)
    ks = iter(jax.random.split(key, 48))
    f32 = jnp.float32
    L = DEPTH

    def nrm(shape, scale):
        return jax.random.normal(next(ks), shape, f32) * scale

    def gain(shape):
        return 1.0 + nrm(shape, 0.02)

    x = nrm((BATCH, SEQ, D_MODEL), 1.0)
    offset = jax.random.randint(next(ks), (BATCH, 1), 0, 4096, dtype=jnp.int32)
    positions = offset + jnp.arange(SEQ, dtype=jnp.int32)[None, :]
    norm_g = gain((L, D_MODEL))
    w_in = nrm((L, D_MODEL, IN_COLS), D_MODEL ** -0.5)
    n_idx = jnp.arange(S5_STATE, dtype=f32)
    s5_lambda_re = -0.5 + nrm((L, S5_GROUPS, S5_STATE), 0.01)
    s5_lambda_im = jnp.pi * n_idx + nrm((L, S5_GROUPS, S5_STATE), 0.01)
    s5_b_re = nrm((L, S5_GROUPS, S5_STATE, S5_GROUP), (2 * S5_GROUP) ** -0.5)
    s5_b_im = nrm((L, S5_GROUPS, S5_STATE, S5_GROUP), (2 * S5_GROUP) ** -0.5)
    s5_c_re = nrm((L, S5_GROUPS, S5_GROUP, S5_STATE), (2 * S5_STATE) ** -0.5)
    s5_c_im = nrm((L, S5_GROUPS, S5_GROUP, S5_STATE), (2 * S5_STATE) ** -0.5)
    s5_d = nrm((L, S5_GROUPS, S5_GROUP), 1.0)
    s5_log_dt = jax.random.uniform(next(ks), (L, S5_GROUPS), f32,
                                   float(np.log(S5_DT_MIN)), float(np.log(S5_DT_MAX)))
    s5_w_glu = nrm((L, BRANCH, BRANCH), BRANCH ** -0.5)
    s5_b_glu = nrm((L, BRANCH), 0.01)
    s5_out_g = gain((L, BRANCH))
    mla_q_norm_g = gain((L, MLA_Q_RANK))
    mla_kv_norm_g = gain((L, MLA_KV_RANK))
    mla_w_uq = nrm((L, MLA_Q_RANK, MLA_HEADS * (MLA_NOPE + MLA_ROPE)), MLA_Q_RANK ** -0.5)
    mla_w_uk = nrm((L, MLA_KV_RANK, MLA_HEADS * MLA_NOPE), MLA_KV_RANK ** -0.5)
    mla_w_uv = nrm((L, MLA_KV_RANK, MLA_HEADS * MLA_V), MLA_KV_RANK ** -0.5)
    mla_out_g = gain((L, BRANCH))
    gla_w_gate = nrm((L, GLA_GATE_RANK, GLA_HEADS * GLA_DK), GLA_GATE_RANK ** -0.5)
    gla_b_gate = nrm((L, GLA_HEADS * GLA_DK), 0.1)
    gla_out_g = gain((L, BRANCH))
    lru_conv_w = nrm((L, LRU_CONV, BRANCH), LRU_CONV ** -0.5)
    lru_conv_b = nrm((L, BRANCH), 0.01)
    lru_w_a = nrm((L, LRU_BLOCKS, LRU_BLOCK_W, LRU_BLOCK_W), LRU_BLOCK_W ** -0.5)
    lru_b_a = nrm((L, BRANCH), 0.01)
    lru_w_x = nrm((L, LRU_BLOCKS, LRU_BLOCK_W, LRU_BLOCK_W), LRU_BLOCK_W ** -0.5)
    lru_b_x = nrm((L, BRANCH), 0.01)
    a0 = jax.random.uniform(next(ks), (L, BRANCH), f32, 0.9, 0.999)
    p0 = a0 ** (1.0 / LRU_C)
    lru_lambda = jnp.log(p0) - jnp.log1p(-p0)
    lru_out_g = gain((L, BRANCH))
    w_out = nrm((L, MIX_WIDTH, D_MODEL), MIX_WIDTH ** -0.5)
    final_g = gain((D_MODEL,))
    return {"x": x, "positions": positions, "norm_g": norm_g, "w_in": w_in,
            "s5_lambda_re": s5_lambda_re, "s5_lambda_im": s5_lambda_im,
            "s5_b_re": s5_b_re, "s5_b_im": s5_b_im, "s5_c_re": s5_c_re, "s5_c_im": s5_c_im,
            "s5_d": s5_d, "s5_log_dt": s5_log_dt, "s5_w_glu": s5_w_glu, "s5_b_glu": s5_b_glu,
            "s5_out_g": s5_out_g, "mla_q_norm_g": mla_q_norm_g, "mla_kv_norm_g": mla_kv_norm_g,
            "mla_w_uq": mla_w_uq, "mla_w_uk": mla_w_uk, "mla_w_uv": mla_w_uv, "mla_out_g": mla_out_g,
            "gla_w_gate": gla_w_gate, "gla_b_gate": gla_b_gate, "gla_out_g": gla_out_g,
            "lru_conv_w": lru_conv_w, "lru_conv_b": lru_conv_b, "lru_w_a": lru_w_a, "lru_b_a": lru_b_a,
            "lru_w_x": lru_w_x, "lru_b_x": lru_b_x, "lru_lambda": lru_lambda, "lru_out_g": lru_out_g,
            "w_out": w_out, "final_g": final_g}


def reference(x, positions, norm_g, w_in, s5_lambda_re, s5_lambda_im, s5_b_re, s5_b_im,
              s5_c_re, s5_c_im, s5_d, s5_log_dt, s5_w_glu, s5_b_glu, s5_out_g,
              mla_q_norm_g, mla_kv_norm_g, mla_w_uq, mla_w_uk, mla_w_uv, mla_out_g,
              gla_w_gate, gla_b_gate, gla_out_g, lru_conv_w, lru_conv_b, lru_w_a, lru_b_a,
              lru_w_x, lru_b_x, lru_lambda, lru_out_g, w_out, final_g):
    for l in range(DEPTH):
        h = rms_norm(x, norm_g[l])
        z = h @ w_in[l]
        (s5_u, s5_gate, mla_cq, mla_ckv, mla_kr, mla_gate, gla_q, gla_k, gla_v,
         gla_glr, gla_gate, lru_u, lru_gate) = split_cols(z)
        y_s5 = rms_norm(s5_mixer(s5_u, s5_lambda_re[l], s5_lambda_im[l], s5_b_re[l], s5_b_im[l],
                                 s5_c_re[l], s5_c_im[l], s5_d[l], s5_log_dt[l], s5_w_glu[l],
                                 s5_b_glu[l]), s5_out_g[l]) * jax.nn.silu(s5_gate)
        y_mla = rms_norm(mla_mixer(mla_cq, mla_ckv, mla_kr, positions, mla_q_norm_g[l],
                                   mla_kv_norm_g[l], mla_w_uq[l], mla_w_uk[l], mla_w_uv[l]),
                         mla_out_g[l]) * jax.nn.silu(mla_gate)
        y_gla = gla_mixer(gla_q, gla_k, gla_v, gla_glr, gla_w_gate[l], gla_b_gate[l],
                          gla_out_g[l]) * jax.nn.silu(gla_gate)
        y_lru = rms_norm(rglru_mixer(lru_u, lru_conv_w[l], lru_conv_b[l], lru_w_a[l], lru_b_a[l],
                                     lru_w_x[l], lru_b_x[l], lru_lambda[l]),
                         lru_out_g[l]) * jax.nn.silu(lru_gate)
        y = jnp.concatenate([y_s5, y_mla, y_gla, y_lru], axis=-1)
        x = x + y @ w_out[l]
    return rms_norm(x, final_g)
```

```python
import functools

import jax
import jax.numpy as jnp
import numpy as np
from jax import lax
from jax.experimental import pallas as pl
from jax.experimental.pallas import tpu as pltpu

F32 = jnp.float32
BF16 = jnp.bfloat16

D_MODEL = 4096
BRANCH = 1024
NORM_EPS = 1e-6

S5_GROUP = 16
S5_GROUPS = 64
S5_STATE = 64

MLA_NOPE = 128
MLA_ROPE = 64
MLA_HEADS = 8
MLA_Q_RANK = 768
MLA_KV_RANK = 512
ROPE_THETA = 10000.0
MLA_SCALE = (MLA_NOPE + MLA_ROPE) ** -0.5
NEG_INF = -1e30

GLA_HEADS = 4
GLA_DV = 256
GLA_DK = 128
GLA_GATE_RANK = 16
GLA_TAU = 16.0
GLA_CHUNK = 64

LRU_BLOCKS = 8
LRU_BLOCK_W = 128
LRU_CONV = 4
LRU_C = 8.0

LANES = 128
SUBLANES = 8

Z_S5_U, Z_S5_GATE, Z_MLA_GATE, Z_GLA_V, Z_GLA_GATE, Z_LRU_U, Z_LRU_GATE = range(7)
Z_CKV_512 = 14
Z_CQ_768 = 10
Z_KR_128 = 66
Z_GLR_128 = 67
Z_GLA_Q_512 = 17
Z_GLA_K_512 = 18
Z_COLS = 9728

VMEM_LIMIT = 56 * 1024 * 1024


def _cparams(n_axes):
    return pltpu.CompilerParams(dimension_semantics=("arbitrary",) * n_axes,
                                vmem_limit_bytes=VMEM_LIMIT)


def _rms(x, g):
    return x * lax.rsqrt(jnp.mean(x * x, axis=-1, keepdims=True) + NORM_EPS) * g


def _silu(x):
    return x * jax.nn.sigmoid(x)


def _inproj_kernel(x_ref, g_ref, w_ref, o_ref, h_ref):
    @pl.when(pl.program_id(1) == 0)
    def _():
        h_ref[...] = _rms(x_ref[...], g_ref[...]).astype(BF16)

    o_ref[...] = jnp.dot(h_ref[...], w_ref[...], preferred_element_type=F32)


def _inproj(x2, g, w, *, tm=512, tn=512):
    t, d = x2.shape
    n = w.shape[1]
    return pl.pallas_call(
        _inproj_kernel,
        grid=(t // tm, n // tn),
        in_specs=[pl.BlockSpec((tm, d), lambda i, j: (i, 0)),
                  pl.BlockSpec((1, d), lambda i, j: (0, 0)),
                  pl.BlockSpec((d, tn), lambda i, j: (0, j))],
        out_specs=pl.BlockSpec((tm, tn), lambda i, j: (i, j)),
        out_shape=jax.ShapeDtypeStruct((t, n), F32),
        scratch_shapes=[pltpu.VMEM((tm, d), BF16)],
        compiler_params=_cparams(2),
        name="inproj",
    )(x2, g, w)


def _s5_kernel(u_ref, gate_ref, wb_ref, lamr_ref, lami_ref, wcr_ref, wci_ref, d_ref,
               wglu_ref, bglu_ref, g_ref, o_ref,
               bre_ref, bim_ref, hre_ref, him_ref, y_ref, *, nb, tc, pitch):
    @pl.when(pl.program_id(0) == 0)
    def _():
        hre_ref[...] = jnp.zeros_like(hre_ref)
        him_ref[...] = jnp.zeros_like(him_ref)

    u2 = u_ref[...].reshape(nb * tc, BRANCH)
    ub = u2.astype(BF16)

    for kb in range(8):
        res = jnp.dot(ub[:, kb * 128:(kb + 1) * 128], wb_ref[kb], preferred_element_type=F32)
        half = kb // 4
        for b in range(nb):
            row0 = (2 * b + half) * pitch
            for jj in range(4):
                j = (kb % 4) * 4 + jj
                bre_ref[j, pl.ds(row0, tc), :] = res[b * tc:(b + 1) * tc, jj * 128:(jj + 1) * 128]
                bim_ref[j, pl.ds(row0, tc), :] = res[b * tc:(b + 1) * tc,
                                                     512 + jj * 128:512 + (jj + 1) * 128]

    for cb in range(4):
        lr = [lamr_ref[:, (cb * 4 + jj) * 128:(cb * 4 + jj + 1) * 128] for jj in range(4)]
        li = [lami_ref[:, (cb * 4 + jj) * 128:(cb * 4 + jj + 1) * 128] for jj in range(4)]
        init = []
        for jj in range(4):
            init.append(hre_ref[:, (cb * 4 + jj) * 128:(cb * 4 + jj + 1) * 128])
            init.append(him_ref[:, (cb * 4 + jj) * 128:(cb * 4 + jj + 1) * 128])

        def body(t, carry, cb=cb, lr=lr, li=li):
            new = []
            for jj in range(4):
                j = cb * 4 + jj
                hr, hi = carry[2 * jj], carry[2 * jj + 1]
                idx = pl.ds(t, SUBLANES, stride=pitch)
                nr = lr[jj] * hr - li[jj] * hi + bre_ref[j, idx, :]
                ni = lr[jj] * hi + li[jj] * hr + bim_ref[j, idx, :]
                bre_ref[j, idx, :] = nr
                bim_ref[j, idx, :] = ni
                new += [nr, ni]
            return tuple(new)

        fin = lax.fori_loop(0, tc, body, tuple(init), unroll=4)
        for jj in range(4):
            hre_ref[:, (cb * 4 + jj) * 128:(cb * 4 + jj + 1) * 128] = fin[2 * jj]
            him_ref[:, (cb * 4 + jj) * 128:(cb * 4 + jj + 1) * 128] = fin[2 * jj + 1]

    for kb in range(8):
        half = kb // 4
        sre, sim = [], []
        for b in range(nb):
            rows = pl.ds((2 * b + half) * pitch, tc)
            sre.append(jnp.concatenate([bre_ref[(kb % 4) * 4 + jj, rows, :] for jj in range(4)], axis=-1))
            sim.append(jnp.concatenate([bim_ref[(kb % 4) * 4 + jj, rows, :] for jj in range(4)], axis=-1))
        sre = jnp.concatenate(sre, axis=0).astype(BF16)
        sim = jnp.concatenate(sim, axis=0).astype(BF16)
        ykb = (jnp.dot(sre, wcr_ref[kb], preferred_element_type=F32)
               + jnp.dot(sim, wci_ref[kb], preferred_element_type=F32))
        cols = slice(kb * 128, (kb + 1) * 128)
        y_ref[:, cols] = ykb + d_ref[:, cols] * u2[:, cols]

    y = jax.nn.gelu(y_ref[...], approximate=True)
    zg = jnp.dot(y.astype(BF16), wglu_ref[...], preferred_element_type=F32) + bglu_ref[...]
    y = y * jax.nn.sigmoid(zg)
    gate = gate_ref[...].reshape(nb * tc, BRANCH)
    o_ref[...] = (_rms(y, g_ref[...]) * _silu(gate)).astype(BF16).reshape(nb, tc, BRANCH)


def _s5(z3, wb, lamr, lami, wcr, wci, d, wglu, bglu, g, *, tc=128):
    nb, seq, _ = z3.shape
    assert 2 * nb == SUBLANES
    pitch = tc + SUBLANES
    full = lambda a: pl.BlockSpec(a.shape, lambda i, _n=a.ndim: (0,) * _n)
    kern = functools.partial(_s5_kernel, nb=nb, tc=tc, pitch=pitch)
    return pl.pallas_call(
        kern,
        grid=(seq // tc,),
        in_specs=[pl.BlockSpec((nb, tc, BRANCH), lambda i: (0, i, Z_S5_U)),
                  pl.BlockSpec((nb, tc, BRANCH), lambda i: (0, i, Z_S5_GATE)),
                  full(wb), full(lamr), full(lami), full(wcr), full(wci), full(d),
                  full(wglu), full(bglu), full(g)],
        out_specs=pl.BlockSpec((nb, tc, BRANCH), lambda i: (0, i, 0)),
        out_shape=jax.ShapeDtypeStruct((nb, seq, BRANCH), BF16),
        scratch_shapes=[pltpu.VMEM((16, SUBLANES * pitch, LANES), F32),
                        pltpu.VMEM((16, SUBLANES * pitch, LANES), F32),
                        pltpu.VMEM((SUBLANES, 2048), F32),
                        pltpu.VMEM((SUBLANES, 2048), F32),
                        pltpu.VMEM((nb * tc, BRANCH), F32)],
        compiler_params=_cparams(1),
        name="s5",
    )(z3, z3, wb, lamr, lami, wcr, wci, d, wglu, bglu, g)


def _s5_params(lam_re, lam_im, b_re, b_im, c_re, c_im, log_dt, nb):
    dt = jnp.exp(log_dt.astype(F32))[:, None]
    lr, li = lam_re.astype(F32), lam_im.astype(F32)
    mag = jnp.exp(lr * dt)
    lbr, lbi = mag * jnp.cos(li * dt), mag * jnp.sin(li * dt)
    den = lr * lr + li * li
    cr = ((lbr - 1.0) * lr + lbi * li) / den
    ci = (lbi * lr - (lbr - 1.0) * li) / den
    bbr = cr[..., None] * b_re - ci[..., None] * b_im
    bbi = cr[..., None] * b_im + ci[..., None] * b_re
    eye = jnp.eye(8, dtype=F32)

    def in_layout(m):
        m = m.reshape(8, 8, S5_STATE, S5_GROUP)
        return jnp.einsum('kgph,gG->kghGp', m, eye).reshape(8, 128, 512)

    def out_layout(m):
        m = m.reshape(8, 8, S5_GROUP, S5_STATE)
        return jnp.einsum('kghp,gG->kgpGh', m, eye).reshape(8, 512, 128)

    wb = jnp.concatenate([in_layout(bbr), in_layout(bbi)], axis=-1).astype(BF16)
    wcr = out_layout(c_re.astype(F32)).astype(BF16)
    wci = out_layout(-c_im.astype(F32)).astype(BF16)

    def seq_layout(v):
        return jnp.tile(v.reshape(1, 2, 2048), (nb, 1, 1)).reshape(2 * nb, 2048)

    return wb, seq_layout(lbr), seq_layout(lbi), wcr, wci


def _lru_kernel(u_ref, gate_ref, cw_ref, cb_ref, wax_ref, ba_ref, bx_ref, sp_ref, g_ref, o_ref,
                xpad_ref, a_ref, h_ref, st_ref, *, nb, tc, pitch):
    @pl.when(pl.program_id(0) == 0)
    def _():
        xpad_ref[:, 0:SUBLANES, :] = jnp.zeros((nb, SUBLANES, BRANCH), F32)
        st_ref[...] = jnp.zeros_like(st_ref)

    xpad_ref[:, SUBLANES:SUBLANES + tc, :] = u_ref[...]
    xc = cb_ref[...].reshape(1, 1, BRANCH)
    for k in range(LRU_CONV):
        off = SUBLANES - (LRU_CONV - 1) + k
        xc = xc + cw_ref[k:k + 1, :].reshape(1, 1, BRANCH) * xpad_ref[:, off:off + tc, :]
    xpad_ref[:, 0:SUBLANES, :] = xpad_ref[:, tc:tc + SUBLANES, :]
    x2 = xc.reshape(nb * tc, BRANCH)
    xb = x2.astype(BF16)

    for hb in range(LRU_BLOCKS):
        cols = slice(hb * 128, (hb + 1) * 128)
        ri = jnp.dot(xb[:, cols], wax_ref[hb], preferred_element_type=F32)
        r = jax.nn.sigmoid(ri[:, :128] + ba_ref[:, cols])
        ig = jax.nn.sigmoid(ri[:, 128:] + bx_ref[:, cols])
        log_a = (-LRU_C) * r * sp_ref[:, cols]
        a = jnp.exp(log_a)
        gated = jnp.sqrt(-jnp.tanh(log_a) * (a * a + 1.0)) * (ig * x2[:, cols])
        half, j = hb // 4, hb % 4
        for b in range(nb):
            rows = pl.ds((2 * b + half) * pitch, tc)
            a_ref[j, rows, :] = a[b * tc:(b + 1) * tc]
            h_ref[j, rows, :] = gated[b * tc:(b + 1) * tc]

    def body(t, carry):
        new = []
        for j in range(4):
            idx = pl.ds(t, SUBLANES, stride=pitch)
            h = a_ref[j, idx, :] * carry[j] + h_ref[j, idx, :]
            h_ref[j, idx, :] = h
            new.append(h)
        return tuple(new)

    init = tuple(st_ref[:, j * 128:(j + 1) * 128] for j in range(4))
    fin = lax.fori_loop(0, tc, body, init, unroll=8)
    for j in range(4):
        st_ref[:, j * 128:(j + 1) * 128] = fin[j]

    for b in range(nb):
        parts = [h_ref[j, pl.ds((2 * b + half) * pitch, tc), :] for half in range(2) for j in range(4)]
        hb_ = jnp.concatenate(parts, axis=-1)
        o_ref[b] = (_rms(hb_, g_ref[...]) * _silu(gate_ref[b])).astype(BF16)


def _lru(z3, cw, cb, wax, ba, bx, sp, g, *, tc=128):
    nb, seq, _ = z3.shape
    assert 2 * nb == SUBLANES
    pitch = tc + SUBLANES
    full = lambda a: pl.BlockSpec(a.shape, lambda i, _n=a.ndim: (0,) * _n)
    kern = functools.partial(_lru_kernel, nb=nb, tc=tc, pitch=pitch)
    return pl.pallas_call(
        kern,
        grid=(seq // tc,),
        in_specs=[pl.BlockSpec((nb, tc, BRANCH), lambda i: (0, i, Z_LRU_U)),
                  pl.BlockSpec((nb, tc, BRANCH), lambda i: (0, i, Z_LRU_GATE)),
                  full(cw), full(cb), full(wax), full(ba), full(bx), full(sp), full(g)],
        out_specs=pl.BlockSpec((nb, tc, BRANCH), lambda i: (0, i, 0)),
        out_shape=jax.ShapeDtypeStruct((nb, seq, BRANCH), BF16),
        scratch_shapes=[pltpu.VMEM((nb, tc + SUBLANES, BRANCH), F32),
                        pltpu.VMEM((4, SUBLANES * pitch, LANES), F32),
                        pltpu.VMEM((4, SUBLANES * pitch, LANES), F32),
                        pltpu.VMEM((SUBLANES, 512), F32)],
        compiler_params=_cparams(1),
        name="rglru",
    )(z3, z3, cw, cb, wax, ba, bx, sp, g)


def _gla_kernel(q_ref, k_ref, v_ref, glr_ref, gate_ref, wg_ref, bg_ref, og_ref, tril_ref, o_ref,
                st_ref, *, tg):
    @pl.when(pl.program_id(1) == 0)
    def _():
        st_ref[...] = jnp.zeros_like(st_ref)

    n_ch = tg // GLA_CHUNK
    x = jnp.dot(glr_ref[...].astype(BF16), wg_ref[...], preferred_element_type=F32) + bg_ref[...]
    log_a = (jnp.minimum(x, 0.0) - jnp.log1p(jnp.exp(-jnp.abs(x)))) / GLA_TAU

    tril = tril_ref[...]
    p0 = log_a.astype(BF16)
    r0 = log_a - p0.astype(F32)
    p1 = r0.astype(BF16)
    p2 = (r0 - p1.astype(F32)).astype(BF16)
    cum = (jnp.dot(tril, p0, preferred_element_type=F32)
           + jnp.dot(tril, p1, preferred_element_type=F32)
           + jnp.dot(tril, p2, preferred_element_type=F32))

    row = lax.broadcasted_iota(jnp.int32, (tg, tg), 0)
    col = lax.broadcasted_iota(jnp.int32, (tg, tg), 1)
    causal = (col <= row) & ((row // GLA_CHUNK) == (col // GLA_CHUNK))
    nt = (((1,), (1,)), ((), ()))
    tn = (((0,), (0,)), ((), ()))

    for hh in range(GLA_HEADS):
        kc = slice(hh * GLA_DK, (hh + 1) * GLA_DK)
        vc = slice(hh * GLA_DV, (hh + 1) * GLA_DV)
        cumh = cum[:, kc]
        kh = k_ref[:, kc]
        qd = (q_ref[:, kc] * (GLA_DK ** -0.5)) * jnp.exp(cumh)
        ki = kh * jnp.exp(-cumh)
        qdb = qd.astype(BF16)
        att = lax.dot_general(qdb, ki.astype(BF16), nt, preferred_element_type=F32)
        att = jnp.where(causal, att, 0.0)
        vb = v_ref[:, vc].astype(BF16)
        o = jnp.dot(att.astype(BF16), vb, preferred_element_type=F32)
        st = st_ref[hh]
        inter = []
        for c in range(n_ch):
            rows = slice(c * GLA_CHUNK, (c + 1) * GLA_CHUNK)
            tot = cumh[(c + 1) * GLA_CHUNK - 1:(c + 1) * GLA_CHUNK, :]
            kd = kh[rows] * jnp.exp(tot - cumh[rows])
            inter.append(lax.dot_general(qdb[rows], st.astype(BF16), nt, preferred_element_type=F32))
            kv_t = lax.dot_general(vb[rows], kd.astype(BF16), tn, preferred_element_type=F32)
            st = st * jnp.exp(tot) + kv_t
        st_ref[hh] = st
        o = o + jnp.concatenate(inter, axis=0)
        gate = gate_ref[:, vc]
        o_ref[:, vc] = (_rms(o, og_ref[:, vc]) * _silu(gate)).astype(BF16)


def _gla(z3, wg, bg, og, tril, *, tg=256):
    nb, seq, _ = z3.shape
    kern = functools.partial(_gla_kernel, tg=tg)
    full = lambda a: pl.BlockSpec(a.shape, lambda b, i, _n=a.ndim: (0,) * _n)
    return pl.pallas_call(
        kern,
        grid=(nb, seq // tg),
        in_specs=[pl.BlockSpec((None, tg, 512), lambda b, i: (b, i, Z_GLA_Q_512)),
                  pl.BlockSpec((None, tg, 512), lambda b, i: (b, i, Z_GLA_K_512)),
                  pl.BlockSpec((None, tg, BRANCH), lambda b, i: (b, i, Z_GLA_V)),
                  pl.BlockSpec((None, tg, 128), lambda b, i: (b, i, Z_GLR_128)),
                  pl.BlockSpec((None, tg, BRANCH), lambda b, i: (b, i, Z_GLA_GATE)),
                  full(wg), full(bg), full(og), full(tril)],
        out_specs=pl.BlockSpec((None, tg, BRANCH), lambda b, i: (b, i, 0)),
        out_shape=jax.ShapeDtypeStruct((nb, seq, BRANCH), BF16),
        scratch_shapes=[pltpu.VMEM((GLA_HEADS, GLA_DV, GLA_DK), F32)],
        compiler_params=_cparams(2),
        name="gla",
    )(z3, z3, z3, z3, z3, wg, bg, og, tril)


def _mla_proj_kernel(cq_ref, ckv_ref, kr_ref, pos_ref, gq_ref, gkv_ref, wq_ref, wkv_ref,
                     invf_ref, sgn_ref, msk_ref, q_out, k_out, v_out):
    cqn = _rms(cq_ref[...], gq_ref[...]).astype(BF16)
    qa = jnp.dot(cqn, wq_ref[...], preferred_element_type=F32)
    ang = pos_ref[...].astype(F32) * invf_ref[...]
    cosm = jnp.cos(ang) * msk_ref[...]
    sinm = jnp.sin(ang) * sgn_ref[...]
    nh = MLA_HEADS * 128
    for h in range(MLA_HEADS):
        q_out[:, h * 256:h * 256 + 128] = (qa[:, h * 128:(h + 1) * 128] * MLA_SCALE).astype(BF16)
        qr = qa[:, nh + h * 128:nh + (h + 1) * 128] * cosm + qa[:, 2 * nh + h * 128:2 * nh + (h + 1) * 128] * sinm
        q_out[:, h * 256 + 128:(h + 1) * 256] = (qr * MLA_SCALE).astype(BF16)
    c = _rms(ckv_ref[...], gkv_ref[...]).astype(BF16)
    kv = jnp.dot(c, wkv_ref[...], preferred_element_type=F32)
    kr = kr_ref[...]
    krp = (kr * cosm + pltpu.roll(kr, 64, 1) * sinm).astype(BF16)
    for h in range(MLA_HEADS):
        k_out[:, h * 256:h * 256 + 128] = kv[:, h * 128:(h + 1) * 128].astype(BF16)
        k_out[:, h * 256 + 128:(h + 1) * 256] = krp
    v_out[...] = kv[:, nh:].astype(BF16)


def _mla_proj(z2, pos2, gq, gkv, wq, wkv, invf, sgn, msk, *, tm=512):
    t = z2.shape[0]
    full = lambda a: pl.BlockSpec(a.shape, lambda i, _n=a.ndim: (0,) * _n)
    return pl.pallas_call(
        _mla_proj_kernel,
        grid=(t // tm,),
        in_specs=[pl.BlockSpec((tm, MLA_Q_RANK), lambda i: (i, Z_CQ_768)),
                  pl.BlockSpec((tm, MLA_KV_RANK), lambda i: (i, Z_CKV_512)),
                  pl.BlockSpec((tm, 128), lambda i: (i, Z_KR_128)),
                  pl.BlockSpec((tm, 1), lambda i: (i, 0)),
                  full(gq), full(gkv), full(wq), full(wkv), full(invf), full(sgn), full(msk)],
        out_specs=[pl.BlockSpec((tm, 2048), lambda i: (i, 0)),
                   pl.BlockSpec((tm, 2048), lambda i: (i, 0)),
                   pl.BlockSpec((tm, BRANCH), lambda i: (i, 0))],
        out_shape=[jax.ShapeDtypeStruct((t, 2048), BF16),
                   jax.ShapeDtypeStruct((t, 2048), BF16),
                   jax.ShapeDtypeStruct((t, BRANCH), BF16)],
        compiler_params=_cparams(1),
        name="mla_proj",
    )(z2, z2, z2, pos2, gq, gkv, wq, wkv, invf, sgn, msk)


def _mla_attn_kernel(q_ref, k_ref, v_ref, gate_ref, g_ref, o_ref, acc_ref, *, tq):
    qi = pl.program_id(1)
    nt = (((1,), (1,)), ((), ()))
    row = lax.broadcasted_iota(jnp.int32, (tq, tq), 0)
    col = lax.broadcasted_iota(jnp.int32, (tq, tq), 1)

    for h in range(MLA_HEADS):
        q = q_ref[:, h * 256:(h + 1) * 256]

        def step(j, carry, masked, h=h, q=q):
            m, l, acc = carry
            r0 = pl.multiple_of(j * tq, tq)
            s = lax.dot_general(q, k_ref[pl.ds(r0, tq), h * 256:(h + 1) * 256], nt,
                                preferred_element_type=F32)
            if masked:
                s = jnp.where(col <= row, s, NEG_INF)
            m_new = jnp.maximum(m, jnp.max(s, axis=-1, keepdims=True))
            alpha = jnp.exp(m - m_new)
            p = jnp.exp(s - m_new)
            l = alpha * l + jnp.sum(p, axis=-1, keepdims=True)
            acc = alpha * acc + jnp.dot(p.astype(BF16), v_ref[pl.ds(r0, tq), h * 128:(h + 1) * 128],
                                        preferred_element_type=F32)
            return m_new, l, acc

        init = (jnp.full((tq, 1), NEG_INF, F32), jnp.zeros((tq, 1), F32), jnp.zeros((tq, 128), F32))
        carry = lax.fori_loop(0, qi, functools.partial(step, masked=False), init)
        _, l, acc = step(qi, carry, True)
        acc_ref[:, h * 128:(h + 1) * 128] = acc / l

    o_ref[...] = (_rms(acc_ref[...], g_ref[...]) * _silu(gate_ref[...])).astype(BF16)


def _mla_attn(q3, k3, v3, z3, g, *, tq=256):
    nb, seq, _ = q3.shape
    kern = functools.partial(_mla_attn_kernel, tq=tq)
    return pl.pallas_call(
        kern,
        grid=(nb, seq // tq),
        in_specs=[pl.BlockSpec((None, tq, 2048), lambda b, i: (b, i, 0)),
                  pl.BlockSpec((None, seq, 2048), lambda b, i: (b, 0, 0)),
                  pl.BlockSpec((None, seq, BRANCH), lambda b, i: (b, 0, 0)),
                  pl.BlockSpec((None, tq, BRANCH), lambda b, i: (b, i, Z_MLA_GATE)),
                  pl.BlockSpec((1, BRANCH), lambda b, i: (0, 0))],
        out_specs=pl.BlockSpec((None, tq, BRANCH), lambda b, i: (b, i, 0)),
        out_shape=jax.ShapeDtypeStruct((nb, seq, BRANCH), BF16),
        scratch_shapes=[pltpu.VMEM((tq, BRANCH), F32)],
        compiler_params=_cparams(2),
        name="mla_attn",
    )(q3, k3, v3, z3, g)


def _outproj_kernel(y0_ref, y1_ref, y2_ref, y3_ref, w_ref, x_ref, o_ref):
    acc = x_ref[...]
    for n, y_ref in enumerate((y0_ref, y1_ref, y2_ref, y3_ref)):
        acc = acc + jnp.dot(y_ref[...], w_ref[n * BRANCH:(n + 1) * BRANCH, :], preferred_element_type=F32)
    o_ref[...] = acc


def _outproj(ys, w, x2, *, tm=1024, tn=512):
    t, d = x2.shape
    yspec = pl.BlockSpec((tm, BRANCH), lambda i, j: (i, 0))
    return pl.pallas_call(
        _outproj_kernel,
        grid=(t // tm, d // tn),
        in_specs=[yspec, yspec, yspec, yspec,
                  pl.BlockSpec((4 * BRANCH, tn), lambda i, j: (0, j)),
                  pl.BlockSpec((tm, tn), lambda i, j: (i, j))],
        out_specs=pl.BlockSpec((tm, tn), lambda i, j: (i, j)),
        out_shape=jax.ShapeDtypeStruct((t, d), F32),
        compiler_params=_cparams(2),
        name="outproj",
    )(*ys, w, x2)


def _final_norm_kernel(x_ref, g_ref, o_ref):
    o_ref[...] = _rms(x_ref[...], g_ref[...])


def _final_norm(x2, g, *, tm=512):
    t, d = x2.shape
    return pl.pallas_call(
        _final_norm_kernel,
        grid=(t // tm,),
        in_specs=[pl.BlockSpec((tm, d), lambda i: (i, 0)), pl.BlockSpec((1, d), lambda i: (0, 0))],
        out_specs=pl.BlockSpec((tm, d), lambda i: (i, 0)),
        out_shape=jax.ShapeDtypeStruct((t, d), F32),
        compiler_params=_cparams(1),
        name="final_norm",
    )(x2, g)


def _swap_halves(w, axis=-1):
    a, b = jnp.split(w, 2, axis=axis)
    return jnp.concatenate([b, a], axis=axis)


def _inproj_weight(w):
    sizes = (BRANCH, BRANCH, MLA_Q_RANK, MLA_KV_RANK, MLA_ROPE, BRANCH,
             GLA_HEADS * GLA_DK, GLA_HEADS * GLA_DK, BRANCH, GLA_GATE_RANK, BRANCH, BRANCH, BRANCH)
    (s5_u, s5_gate, cq, ckv, kr, mla_gate, gq, gk, gv, glr, gla_gate, lru_u, lru_gate) = jnp.split(
        w, np.cumsum(sizes)[:-1].tolist(), axis=1)
    glr_pad = jnp.pad(glr, ((0, 0), (0, 128 - GLA_GATE_RANK)))
    cols = [s5_u, s5_gate, mla_gate, gv, gla_gate, lru_u, lru_gate, ckv, cq,
            kr, _swap_halves(kr), glr_pad, gq, gk]
    return jnp.concatenate(cols, axis=1).astype(BF16)


def _mla_q_weight(w_uq):
    w = w_uq.reshape(MLA_Q_RANK, MLA_HEADS, MLA_NOPE + MLA_ROPE)
    nope = w[:, :, :MLA_NOPE].reshape(MLA_Q_RANK, MLA_HEADS * MLA_NOPE)
    rope = w[:, :, MLA_NOPE:]
    pad = ((0, 0), (0, 0), (0, 128 - MLA_ROPE))
    rope_p = jnp.pad(rope, pad).reshape(MLA_Q_RANK, MLA_HEADS * 128)
    rope_s = jnp.pad(_swap_halves(rope), pad).reshape(MLA_Q_RANK, MLA_HEADS * 128)
    return jnp.concatenate([nope, rope_p, rope_s], axis=1).astype(BF16)


def _rope_tables():
    half = MLA_ROPE // 2
    inv = 1.0 / (ROPE_THETA ** (jnp.arange(half, dtype=F32) / half))
    zeros = jnp.zeros((128 - MLA_ROPE,), F32)
    invf = jnp.concatenate([inv, inv, zeros]).reshape(1, 128)
    sgn = jnp.concatenate([-jnp.ones((half,), F32), jnp.ones((half,), F32), zeros]).reshape(1, 128)
    msk = jnp.concatenate([jnp.ones((MLA_ROPE,), F32), zeros]).reshape(1, 128)
    return invf, sgn, msk


def _gla_tril(tg):
    r = np.arange(tg)
    m = (r[None, :] <= r[:, None]) & ((r[:, None] // GLA_CHUNK) == (r[None, :] // GLA_CHUNK))
    return jnp.asarray(m, dtype=BF16)


def kernel(x, positions, norm_g, w_in, s5_lambda_re, s5_lambda_im, s5_b_re, s5_b_im, s5_c_re, s5_c_im,
           s5_d, s5_log_dt, s5_w_glu, s5_b_glu, s5_out_g, mla_q_norm_g, mla_kv_norm_g, mla_w_uq,
           mla_w_uk, mla_w_uv, mla_out_g, gla_w_gate, gla_b_gate, gla_out_g, lru_conv_w, lru_conv_b,
           lru_w_a, lru_b_a, lru_w_x, lru_b_x, lru_lambda, lru_out_g, w_out, final_g):
    nb, seq, d = x.shape
    depth = w_in.shape[0]
    t = nb * seq
    row = lambda v: v.reshape(1, -1).astype(F32)
    x2 = x.reshape(t, d)
    pos2 = positions.reshape(t, 1)
    invf, sgn, msk = _rope_tables()
    gla_tg = 256
    tril = _gla_tril(gla_tg)

    for l in range(depth):
        z2 = _inproj(x2, row(norm_g[l]), _inproj_weight(w_in[l]))
        z3 = z2.reshape(nb, seq, Z_COLS)

        wb, lamr, lami, wcr, wci = _s5_params(s5_lambda_re[l], s5_lambda_im[l], s5_b_re[l], s5_b_im[l],
                                              s5_c_re[l], s5_c_im[l], s5_log_dt[l], nb)
        y_s5 = _s5(z3, wb, lamr, lami, wcr, wci, row(s5_d[l]), s5_w_glu[l].astype(BF16),
                   row(s5_b_glu[l]), row(s5_out_g[l]))

        wkv = jnp.concatenate([mla_w_uk[l], mla_w_uv[l]], axis=1).astype(BF16)
        q2, k2, v2 = _mla_proj(z2, pos2, row(mla_q_norm_g[l]), row(mla_kv_norm_g[l]),
                               _mla_q_weight(mla_w_uq[l]), wkv, invf, sgn, msk)
        y_mla = _mla_attn(q2.reshape(nb, seq, 2048), k2.reshape(nb, seq, 2048),
                          v2.reshape(nb, seq, BRANCH), z3, row(mla_out_g[l]))

        wg = jnp.pad(gla_w_gate[l], ((0, 128 - GLA_GATE_RANK), (0, 0))).astype(BF16)
        y_gla = _gla(z3, wg, row(gla_b_gate[l]), row(gla_out_g[l]), tril, tg=gla_tg)

        wax = jnp.concatenate([lru_w_a[l], lru_w_x[l]], axis=-1).astype(BF16)
        sp = row(jax.nn.softplus(-lru_lambda[l].astype(F32)))
        y_lru = _lru(z3, lru_conv_w[l].astype(F32), row(lru_conv_b[l]), wax, row(lru_b_a[l]),
                     row(lru_b_x[l]), sp, row(lru_out_g[l]))

        ys = [y.reshape(t, BRANCH) for y in (y_s5, y_mla, y_gla, y_lru)]
        x2 = _outproj(ys, w_out[l].astype(BF16), x2)

    return _final_norm(x2, row(final_g)).reshape(nb, seq, d)
```

```python
import functools

import jax
import jax.numpy as jnp
import numpy as np
from jax import lax
from jax.experimental import pallas as pl
from jax.experimental.pallas import tpu as pltpu

F32 = jnp.float32
BF16 = jnp.bfloat16

D_MODEL = 4096
BRANCH = 1024
NORM_EPS = 1e-6

S5_GROUP = 16
S5_GROUPS = 64
S5_STATE = 64

MLA_NOPE = 128
MLA_ROPE = 64
MLA_HEADS = 8
MLA_Q_RANK = 768
MLA_KV_RANK = 512
ROPE_THETA = 10000.0
MLA_SCALE = (MLA_NOPE + MLA_ROPE) ** -0.5
NEG_INF = -1e30
LOG2_E = 1.4426950408889634
MLA_TQ = 512

GLA_HEADS = 4
GLA_DV = 256
GLA_DK = 128
GLA_GATE_RANK = 16
GLA_TAU = 16.0
GLA_CHUNK = 64

LRU_BLOCKS = 8
LRU_BLOCK_W = 128
LRU_CONV = 4
LRU_C = 8.0

LANES = 128
SUBLANES = 8

Z_S5_U, Z_S5_GATE, Z_MLA_GATE, Z_GLA_V, Z_GLA_GATE, Z_LRU_U, Z_LRU_GATE = range(7)
Z_CKV_512 = 14
Z_CQ_768 = 10
Z_KR_128 = 66
Z_GLR_128 = 67
Z_GLA_Q_512 = 17
Z_GLA_K_512 = 18
Z_COLS = 9728

VMEM_LIMIT = 56 * 1024 * 1024


def _cparams(n_axes):
    return pltpu.CompilerParams(dimension_semantics=("arbitrary",) * n_axes,
                                vmem_limit_bytes=VMEM_LIMIT)


def _rms(x, g):
    return x * lax.rsqrt(jnp.mean(x * x, axis=-1, keepdims=True) + NORM_EPS) * g


def _silu(x):
    return x * jax.nn.sigmoid(x)


def _inproj_kernel(x_ref, g_ref, w_ref, o_ref, h_ref):
    @pl.when(pl.program_id(1) == 0)
    def _():
        h_ref[...] = _rms(x_ref[...], g_ref[...]).astype(BF16)

    o_ref[...] = jnp.dot(h_ref[...], w_ref[...], preferred_element_type=F32)


def _inproj(x2, g, w, *, tm=512, tn=512):
    t, d = x2.shape
    n = w.shape[1]
    return pl.pallas_call(
        _inproj_kernel,
        grid=(t // tm, n // tn),
        in_specs=[pl.BlockSpec((tm, d), lambda i, j: (i, 0)),
                  pl.BlockSpec((1, d), lambda i, j: (0, 0)),
                  pl.BlockSpec((d, tn), lambda i, j: (0, j))],
        out_specs=pl.BlockSpec((tm, tn), lambda i, j: (i, j)),
        out_shape=jax.ShapeDtypeStruct((t, n), F32),
        scratch_shapes=[pltpu.VMEM((tm, d), BF16)],
        compiler_params=_cparams(2),
        name="inproj",
    )(x2, g, w)


def _s5_kernel(u_ref, gate_ref, wb_ref, lamr_ref, lami_ref, wcr_ref, wci_ref, d_ref,
               wglu_ref, bglu_ref, g_ref, o_ref,
               bre_ref, bim_ref, hre_ref, him_ref, y_ref, *, nb, tc, pitch):
    @pl.when(pl.program_id(0) == 0)
    def _():
        hre_ref[...] = jnp.zeros_like(hre_ref)
        him_ref[...] = jnp.zeros_like(him_ref)

    u2 = u_ref[...].reshape(nb * tc, BRANCH)
    ub = u2.astype(BF16)

    for kb in range(8):
        res = jnp.dot(ub[:, kb * 128:(kb + 1) * 128], wb_ref[kb], preferred_element_type=F32)
        half = kb // 4
        for b in range(nb):
            row0 = (2 * b + half) * pitch
            for jj in range(4):
                j = (kb % 4) * 4 + jj
                bre_ref[j, pl.ds(row0, tc), :] = res[b * tc:(b + 1) * tc, jj * 128:(jj + 1) * 128]
                bim_ref[j, pl.ds(row0, tc), :] = res[b * tc:(b + 1) * tc,
                                                     512 + jj * 128:512 + (jj + 1) * 128]

    for cb in range(4):
        lr = [lamr_ref[:, (cb * 4 + jj) * 128:(cb * 4 + jj + 1) * 128] for jj in range(4)]
        li = [lami_ref[:, (cb * 4 + jj) * 128:(cb * 4 + jj + 1) * 128] for jj in range(4)]
        init = []
        for jj in range(4):
            init.append(hre_ref[:, (cb * 4 + jj) * 128:(cb * 4 + jj + 1) * 128])
            init.append(him_ref[:, (cb * 4 + jj) * 128:(cb * 4 + jj + 1) * 128])

        def body(t, carry, cb=cb, lr=lr, li=li):
            new = []
            for jj in range(4):
                j = cb * 4 + jj
                hr, hi = carry[2 * jj], carry[2 * jj + 1]
                idx = pl.ds(t, SUBLANES, stride=pitch)
                nr = lr[jj] * hr - li[jj] * hi + bre_ref[j, idx, :]
                ni = lr[jj] * hi + li[jj] * hr + bim_ref[j, idx, :]
                bre_ref[j, idx, :] = nr
                bim_ref[j, idx, :] = ni
                new += [nr, ni]
            return tuple(new)

        fin = lax.fori_loop(0, tc, body, tuple(init), unroll=4)
        for jj in range(4):
            hre_ref[:, (cb * 4 + jj) * 128:(cb * 4 + jj + 1) * 128] = fin[2 * jj]
            him_ref[:, (cb * 4 + jj) * 128:(cb * 4 + jj + 1) * 128] = fin[2 * jj + 1]

    for kb in range(8):
        half = kb // 4
        sre, sim = [], []
        for b in range(nb):
            rows = pl.ds((2 * b + half) * pitch, tc)
            sre.append(jnp.concatenate([bre_ref[(kb % 4) * 4 + jj, rows, :] for jj in range(4)], axis=-1))
            sim.append(jnp.concatenate([bim_ref[(kb % 4) * 4 + jj, rows, :] for jj in range(4)], axis=-1))
        sre = jnp.concatenate(sre, axis=0).astype(BF16)
        sim = jnp.concatenate(sim, axis=0).astype(BF16)
        ykb = (jnp.dot(sre, wcr_ref[kb], preferred_element_type=F32)
               + jnp.dot(sim, wci_ref[kb], preferred_element_type=F32))
        cols = slice(kb * 128, (kb + 1) * 128)
        y_ref[:, cols] = ykb + d_ref[:, cols] * u2[:, cols]

    y = jax.nn.gelu(y_ref[...], approximate=True)
    zg = jnp.dot(y.astype(BF16), wglu_ref[...], preferred_element_type=F32) + bglu_ref[...]
    y = y * jax.nn.sigmoid(zg)
    gate = gate_ref[...].reshape(nb * tc, BRANCH)
    o_ref[...] = (_rms(y, g_ref[...]) * _silu(gate)).astype(BF16).reshape(nb, tc, BRANCH)


def _s5(z3, wb, lamr, lami, wcr, wci, d, wglu, bglu, g, *, tc=128):
    nb, seq, _ = z3.shape
    assert 2 * nb == SUBLANES
    pitch = tc + SUBLANES
    full = lambda a: pl.BlockSpec(a.shape, lambda i, _n=a.ndim: (0,) * _n)
    kern = functools.partial(_s5_kernel, nb=nb, tc=tc, pitch=pitch)
    return pl.pallas_call(
        kern,
        grid=(seq // tc,),
        in_specs=[pl.BlockSpec((nb, tc, BRANCH), lambda i: (0, i, Z_S5_U)),
                  pl.BlockSpec((nb, tc, BRANCH), lambda i: (0, i, Z_S5_GATE)),
                  full(wb), full(lamr), full(lami), full(wcr), full(wci), full(d),
                  full(wglu), full(bglu), full(g)],
        out_specs=pl.BlockSpec((nb, tc, BRANCH), lambda i: (0, i, 0)),
        out_shape=jax.ShapeDtypeStruct((nb, seq, BRANCH), BF16),
        scratch_shapes=[pltpu.VMEM((16, SUBLANES * pitch, LANES), F32),
                        pltpu.VMEM((16, SUBLANES * pitch, LANES), F32),
                        pltpu.VMEM((SUBLANES, 2048), F32),
                        pltpu.VMEM((SUBLANES, 2048), F32),
                        pltpu.VMEM((nb * tc, BRANCH), F32)],
        compiler_params=_cparams(1),
        name="s5",
    )(z3, z3, wb, lamr, lami, wcr, wci, d, wglu, bglu, g)


def _s5_params(lam_re, lam_im, b_re, b_im, c_re, c_im, log_dt, nb):
    dt = jnp.exp(log_dt.astype(F32))[:, None]
    lr, li = lam_re.astype(F32), lam_im.astype(F32)
    mag = jnp.exp(lr * dt)
    lbr, lbi = mag * jnp.cos(li * dt), mag * jnp.sin(li * dt)
    den = lr * lr + li * li
    cr = ((lbr - 1.0) * lr + lbi * li) / den
    ci = (lbi * lr - (lbr - 1.0) * li) / den
    bbr = cr[..., None] * b_re - ci[..., None] * b_im
    bbi = cr[..., None] * b_im + ci[..., None] * b_re
    eye = jnp.eye(8, dtype=F32)

    def in_layout(m):
        m = m.reshape(8, 8, S5_STATE, S5_GROUP)
        return jnp.einsum('kgph,gG->kghGp', m, eye).reshape(8, 128, 512)

    def out_layout(m):
        m = m.reshape(8, 8, S5_GROUP, S5_STATE)
        return jnp.einsum('kghp,gG->kgpGh', m, eye).reshape(8, 512, 128)

    wb = jnp.concatenate([in_layout(bbr), in_layout(bbi)], axis=-1).astype(BF16)
    wcr = out_layout(c_re.astype(F32)).astype(BF16)
    wci = out_layout(-c_im.astype(F32)).astype(BF16)

    def seq_layout(v):
        return jnp.tile(v.reshape(1, 2, 2048), (nb, 1, 1)).reshape(2 * nb, 2048)

    return wb, seq_layout(lbr), seq_layout(lbi), wcr, wci


def _lru_kernel(u_ref, gate_ref, cw_ref, cb_ref, wax_ref, ba_ref, bx_ref, sp_ref, g_ref, o_ref,
                xpad_ref, a_ref, h_ref, st_ref, *, nb, tc, pitch):
    @pl.when(pl.program_id(0) == 0)
    def _():
        xpad_ref[:, 0:SUBLANES, :] = jnp.zeros((nb, SUBLANES, BRANCH), F32)
        st_ref[...] = jnp.zeros_like(st_ref)

    xpad_ref[:, SUBLANES:SUBLANES + tc, :] = u_ref[...]
    xc = cb_ref[...].reshape(1, 1, BRANCH)
    for k in range(LRU_CONV):
        off = SUBLANES - (LRU_CONV - 1) + k
        xc = xc + cw_ref[k:k + 1, :].reshape(1, 1, BRANCH) * xpad_ref[:, off:off + tc, :]
    xpad_ref[:, 0:SUBLANES, :] = xpad_ref[:, tc:tc + SUBLANES, :]
    x2 = xc.reshape(nb * tc, BRANCH)
    xb = x2.astype(BF16)

    for hb in range(LRU_BLOCKS):
        cols = slice(hb * 128, (hb + 1) * 128)
        ri = jnp.dot(xb[:, cols], wax_ref[hb], preferred_element_type=F32)
        r = jax.nn.sigmoid(ri[:, :128] + ba_ref[:, cols])
        ig = jax.nn.sigmoid(ri[:, 128:] + bx_ref[:, cols])
        log_a = (-LRU_C) * r * sp_ref[:, cols]
        a = jnp.exp(log_a)
        gated = jnp.sqrt(-jnp.tanh(log_a) * (a * a + 1.0)) * (ig * x2[:, cols])
        half, j = hb // 4, hb % 4
        for b in range(nb):
            rows = pl.ds((2 * b + half) * pitch, tc)
            a_ref[j, rows, :] = a[b * tc:(b + 1) * tc]
            h_ref[j, rows, :] = gated[b * tc:(b + 1) * tc]

    def body(t, carry):
        new = []
        for j in range(4):
            idx = pl.ds(t, SUBLANES, stride=pitch)
            h = a_ref[j, idx, :] * carry[j] + h_ref[j, idx, :]
            h_ref[j, idx, :] = h
            new.append(h)
        return tuple(new)

    init = tuple(st_ref[:, j * 128:(j + 1) * 128] for j in range(4))
    fin = lax.fori_loop(0, tc, body, init, unroll=8)
    for j in range(4):
        st_ref[:, j * 128:(j + 1) * 128] = fin[j]

    for b in range(nb):
        parts = [h_ref[j, pl.ds((2 * b + half) * pitch, tc), :] for half in range(2) for j in range(4)]
        hb_ = jnp.concatenate(parts, axis=-1)
        o_ref[b] = (_rms(hb_, g_ref[...]) * _silu(gate_ref[b])).astype(BF16)


def _lru(z3, cw, cb, wax, ba, bx, sp, g, *, tc=128):
    nb, seq, _ = z3.shape
    assert 2 * nb == SUBLANES
    pitch = tc + SUBLANES
    full = lambda a: pl.BlockSpec(a.shape, lambda i, _n=a.ndim: (0,) * _n)
    kern = functools.partial(_lru_kernel, nb=nb, tc=tc, pitch=pitch)
    return pl.pallas_call(
        kern,
        grid=(seq // tc,),
        in_specs=[pl.BlockSpec((nb, tc, BRANCH), lambda i: (0, i, Z_LRU_U)),
                  pl.BlockSpec((nb, tc, BRANCH), lambda i: (0, i, Z_LRU_GATE)),
                  full(cw), full(cb), full(wax), full(ba), full(bx), full(sp), full(g)],
        out_specs=pl.BlockSpec((nb, tc, BRANCH), lambda i: (0, i, 0)),
        out_shape=jax.ShapeDtypeStruct((nb, seq, BRANCH), BF16),
        scratch_shapes=[pltpu.VMEM((nb, tc + SUBLANES, BRANCH), F32),
                        pltpu.VMEM((4, SUBLANES * pitch, LANES), F32),
                        pltpu.VMEM((4, SUBLANES * pitch, LANES), F32),
                        pltpu.VMEM((SUBLANES, 512), F32)],
        compiler_params=_cparams(1),
        name="rglru",
    )(z3, z3, cw, cb, wax, ba, bx, sp, g)


def _gla_kernel(q_ref, k_ref, v_ref, glr_ref, gate_ref, wg_ref, bg_ref, og_ref, tril_ref, o_ref,
                st_ref, *, tg):
    @pl.when(pl.program_id(1) == 0)
    def _():
        st_ref[...] = jnp.zeros_like(st_ref)

    n_ch = tg // GLA_CHUNK
    x = jnp.dot(glr_ref[...].astype(BF16), wg_ref[...], preferred_element_type=F32) + bg_ref[...]
    log_a = (jnp.minimum(x, 0.0) - jnp.log1p(jnp.exp(-jnp.abs(x)))) / GLA_TAU

    tril = tril_ref[...]
    p0 = log_a.astype(BF16)
    r0 = log_a - p0.astype(F32)
    p1 = r0.astype(BF16)
    p2 = (r0 - p1.astype(F32)).astype(BF16)
    cum = (jnp.dot(tril, p0, preferred_element_type=F32)
           + jnp.dot(tril, p1, preferred_element_type=F32)
           + jnp.dot(tril, p2, preferred_element_type=F32))

    row = lax.broadcasted_iota(jnp.int32, (tg, tg), 0)
    col = lax.broadcasted_iota(jnp.int32, (tg, tg), 1)
    causal = (col <= row) & ((row // GLA_CHUNK) == (col // GLA_CHUNK))
    nt = (((1,), (1,)), ((), ()))
    tn = (((0,), (0,)), ((), ()))

    for hh in range(GLA_HEADS):
        kc = slice(hh * GLA_DK, (hh + 1) * GLA_DK)
        vc = slice(hh * GLA_DV, (hh + 1) * GLA_DV)
        cumh = cum[:, kc]
        kh = k_ref[:, kc]
        qd = (q_ref[:, kc] * (GLA_DK ** -0.5)) * jnp.exp(cumh)
        ki = kh * jnp.exp(-cumh)
        qdb = qd.astype(BF16)
        att = lax.dot_general(qdb, ki.astype(BF16), nt, preferred_element_type=F32)
        att = jnp.where(causal, att, 0.0)
        vb = v_ref[:, vc].astype(BF16)
        o = jnp.dot(att.astype(BF16), vb, preferred_element_type=F32)
        st = st_ref[hh]
        inter = []
        for c in range(n_ch):
            rows = slice(c * GLA_CHUNK, (c + 1) * GLA_CHUNK)
            tot = cumh[(c + 1) * GLA_CHUNK - 1:(c + 1) * GLA_CHUNK, :]
            kd = kh[rows] * jnp.exp(tot - cumh[rows])
            inter.append(lax.dot_general(qdb[rows], st.astype(BF16), nt, preferred_element_type=F32))
            kv_t = lax.dot_general(vb[rows], kd.astype(BF16), tn, preferred_element_type=F32)
            st = st * jnp.exp(tot) + kv_t
        st_ref[hh] = st
        o = o + jnp.concatenate(inter, axis=0)
        gate = gate_ref[:, vc]
        o_ref[:, vc] = (_rms(o, og_ref[:, vc]) * _silu(gate)).astype(BF16)


def _gla(z3, wg, bg, og, tril, *, tg=256):
    nb, seq, _ = z3.shape
    kern = functools.partial(_gla_kernel, tg=tg)
    full = lambda a: pl.BlockSpec(a.shape, lambda b, i, _n=a.ndim: (0,) * _n)
    return pl.pallas_call(
        kern,
        grid=(nb, seq // tg),
        in_specs=[pl.BlockSpec((None, tg, 512), lambda b, i: (b, i, Z_GLA_Q_512)),
                  pl.BlockSpec((None, tg, 512), lambda b, i: (b, i, Z_GLA_K_512)),
                  pl.BlockSpec((None, tg, BRANCH), lambda b, i: (b, i, Z_GLA_V)),
                  pl.BlockSpec((None, tg, 128), lambda b, i: (b, i, Z_GLR_128)),
                  pl.BlockSpec((None, tg, BRANCH), lambda b, i: (b, i, Z_GLA_GATE)),
                  full(wg), full(bg), full(og), full(tril)],
        out_specs=pl.BlockSpec((None, tg, BRANCH), lambda b, i: (b, i, 0)),
        out_shape=jax.ShapeDtypeStruct((nb, seq, BRANCH), BF16),
        scratch_shapes=[pltpu.VMEM((GLA_HEADS, GLA_DV, GLA_DK), F32)],
        compiler_params=_cparams(2),
        name="gla",
    )(z3, z3, z3, z3, z3, wg, bg, og, tril)


def _mla_proj_kernel(cq_ref, ckv_ref, kr_ref, pos_ref, gq_ref, gkv_ref, wq_ref, wk_ref, wvt_ref,
                     invf_ref, sgn_ref, msk_ref, q_out, k_out, vt_out):
    cqn = _rms(cq_ref[...], gq_ref[...]).astype(BF16)
    qa = jnp.dot(cqn, wq_ref[...], preferred_element_type=F32)
    ang = pos_ref[...].astype(F32) * invf_ref[...]
    cosm = jnp.cos(ang) * msk_ref[...]
    sinm = jnp.sin(ang) * sgn_ref[...]
    nh = MLA_HEADS * 128
    qscale = MLA_SCALE * LOG2_E
    for h in range(MLA_HEADS):
        q_out[:, h * 256:h * 256 + 128] = (qa[:, h * 128:(h + 1) * 128] * qscale).astype(BF16)
        qr = qa[:, nh + h * 128:nh + (h + 1) * 128] * cosm + qa[:, 2 * nh + h * 128:2 * nh + (h + 1) * 128] * sinm
        q_out[:, h * 256 + 128:(h + 1) * 256] = (qr * qscale).astype(BF16)
    c = _rms(ckv_ref[...], gkv_ref[...]).astype(BF16)
    kn = jnp.dot(c, wk_ref[...], preferred_element_type=F32)
    kr = kr_ref[...]
    krp = (kr * cosm + pltpu.roll(kr, 64, 1) * sinm).astype(BF16)
    for h in range(MLA_HEADS):
        k_out[:, h * 256:h * 256 + 128] = kn[:, h * 128:(h + 1) * 128].astype(BF16)
        k_out[:, h * 256 + 128:(h + 1) * 256] = krp
    nt = (((1,), (1,)), ((), ()))
    vt_out[...] = lax.dot_general(wvt_ref[...], c, nt, preferred_element_type=F32).astype(BF16)


def _mla_proj(z2, pos2, gq, gkv, wq, wk, wvt, invf, sgn, msk, *, tm):
    t = z2.shape[0]
    full = lambda a: pl.BlockSpec(a.shape, lambda i, _n=a.ndim: (0,) * _n)
    return pl.pallas_call(
        _mla_proj_kernel,
        grid=(t // tm,),
        in_specs=[pl.BlockSpec((tm, MLA_Q_RANK), lambda i: (i, Z_CQ_768)),
                  pl.BlockSpec((tm, MLA_KV_RANK), lambda i: (i, Z_CKV_512)),
                  pl.BlockSpec((tm, 128), lambda i: (i, Z_KR_128)),
                  pl.BlockSpec((tm, 1), lambda i: (i, 0)),
                  full(gq), full(gkv), full(wq), full(wk), full(wvt), full(invf), full(sgn), full(msk)],
        out_specs=[pl.BlockSpec((tm, 2048), lambda i: (i, 0)),
                   pl.BlockSpec((tm, 2048), lambda i: (i, 0)),
                   pl.BlockSpec((None, BRANCH, tm), lambda i: (i, 0, 0))],
        out_shape=[jax.ShapeDtypeStruct((t, 2048), BF16),
                   jax.ShapeDtypeStruct((t, 2048), BF16),
                   jax.ShapeDtypeStruct((t // tm, BRANCH, tm), BF16)],
        compiler_params=_cparams(1),
        name="mla_proj",
    )(z2, z2, z2, pos2, gq, gkv, wq, wk, wvt, invf, sgn, msk)


def _mla_attn_kernel(q_ref, k_ref, vt_ref, gate_ref, g_ref, o_ref, m_ref, l_ref, acc_ref, o_scr, *, tq):
    qi = pl.program_id(1)
    nt = (((1,), (1,)), ((), ()))
    key = lax.broadcasted_iota(jnp.int32, (tq, tq), 0)
    qry = lax.broadcasted_iota(jnp.int32, (tq, tq), 1)

    m_ref[...] = jnp.full(m_ref.shape, NEG_INF, F32)
    l_ref[...] = jnp.zeros_like(l_ref)
    acc_ref[...] = jnp.zeros_like(acc_ref)

    def step(j, masked):
        r0 = pl.multiple_of(j * tq, tq)
        for h in range(MLA_HEADS):
            st = lax.dot_general(k_ref[pl.ds(r0, tq), h * 256:(h + 1) * 256],
                                 q_ref[:, h * 256:(h + 1) * 256], nt,
                                 preferred_element_type=F32)
            if masked:
                st = jnp.where(key <= qry, st, NEG_INF)
            m_old = m_ref[h]
            m_new = jnp.maximum(m_old, jnp.max(st, axis=0, keepdims=True))
            alpha = jnp.exp2(m_old - m_new)
            pt = jnp.exp2(st - m_new)
            l_ref[h] = alpha * l_ref[h] + jnp.sum(pt, axis=0, keepdims=True)
            acc_ref[h] = alpha * acc_ref[h] + jnp.dot(
                vt_ref[j, h * 128:(h + 1) * 128, :], pt.astype(BF16), preferred_element_type=F32)
            m_ref[h] = m_new

    def body(j, c):
        step(j, False)
        return c

    lax.fori_loop(0, qi, body, 0)
    step(qi, True)

    for h in range(MLA_HEADS):
        o_scr[:, h * 128:(h + 1) * 128] = (acc_ref[h] / l_ref[h]).T
    o_ref[...] = (_rms(o_scr[...], g_ref[...]) * _silu(gate_ref[...])).astype(BF16)


def _mla_attn(q3, k3, vt4, z3, g, *, tq):
    nb, seq, _ = q3.shape
    kern = functools.partial(_mla_attn_kernel, tq=tq)
    return pl.pallas_call(
        kern,
        grid=(nb, seq // tq),
        in_specs=[pl.BlockSpec((None, tq, 2048), lambda b, i: (b, i, 0)),
                  pl.BlockSpec((None, seq, 2048), lambda b, i: (b, 0, 0)),
                  pl.BlockSpec((None, seq // tq, BRANCH, tq), lambda b, i: (b, 0, 0, 0)),
                  pl.BlockSpec((None, tq, BRANCH), lambda b, i: (b, i, Z_MLA_GATE)),
                  pl.BlockSpec((1, BRANCH), lambda b, i: (0, 0))],
        out_specs=pl.BlockSpec((None, tq, BRANCH), lambda b, i: (b, i, 0)),
        out_shape=jax.ShapeDtypeStruct((nb, seq, BRANCH), BF16),
        scratch_shapes=[pltpu.VMEM((MLA_HEADS, 1, tq), F32),
                        pltpu.VMEM((MLA_HEADS, 1, tq), F32),
                        pltpu.VMEM((MLA_HEADS, 128, tq), F32),
                        pltpu.VMEM((tq, BRANCH), F32)],
        compiler_params=_cparams(2),
        name="mla_attn",
    )(q3, k3, vt4, z3, g)


def _outproj_kernel(y0_ref, y1_ref, y2_ref, y3_ref, w_ref, x_ref, o_ref):
    acc = x_ref[...]
    for n, y_ref in enumerate((y0_ref, y1_ref, y2_ref, y3_ref)):
        acc = acc + jnp.dot(y_ref[...], w_ref[n * BRANCH:(n + 1) * BRANCH, :], preferred_element_type=F32)
    o_ref[...] = acc


def _outproj(ys, w_all, layer, x2, *, tm=1024, tn=512):
    t, d = x2.shape
    yspec = pl.BlockSpec((tm, BRANCH), lambda i, j: (i, 0))
    return pl.pallas_call(
        _outproj_kernel,
        grid=(t // tm, d // tn),
        in_specs=[yspec, yspec, yspec, yspec,
                  pl.BlockSpec((None, 4 * BRANCH, tn), lambda i, j: (layer, 0, j)),
                  pl.BlockSpec((tm, tn), lambda i, j: (i, j))],
        out_specs=pl.BlockSpec((tm, tn), lambda i, j: (i, j)),
        out_shape=jax.ShapeDtypeStruct((t, d), F32),
        compiler_params=_cparams(2),
        name="outproj",
    )(*ys, w_all, x2)


def _final_norm_kernel(x_ref, g_ref, o_ref):
    o_ref[...] = _rms(x_ref[...], g_ref[...])


def _final_norm(x2, g, *, tm=512):
    t, d = x2.shape
    return pl.pallas_call(
        _final_norm_kernel,
        grid=(t // tm,),
        in_specs=[pl.BlockSpec((tm, d), lambda i: (i, 0)), pl.BlockSpec((1, d), lambda i: (0, 0))],
        out_specs=pl.BlockSpec((tm, d), lambda i: (i, 0)),
        out_shape=jax.ShapeDtypeStruct((t, d), F32),
        compiler_params=_cparams(1),
        name="final_norm",
    )(x2, g)


def _swap_halves(w, axis=-1):
    a, b = jnp.split(w, 2, axis=axis)
    return jnp.concatenate([b, a], axis=axis)


def _inproj_weight(w):
    sizes = (BRANCH, BRANCH, MLA_Q_RANK, MLA_KV_RANK, MLA_ROPE, BRANCH,
             GLA_HEADS * GLA_DK, GLA_HEADS * GLA_DK, BRANCH, GLA_GATE_RANK, BRANCH, BRANCH, BRANCH)
    (s5_u, s5_gate, cq, ckv, kr, mla_gate, gq, gk, gv, glr, gla_gate, lru_u, lru_gate) = jnp.split(
        w, np.cumsum(sizes)[:-1].tolist(), axis=1)
    glr_pad = jnp.pad(glr, ((0, 0), (0, 128 - GLA_GATE_RANK)))
    cols = [s5_u, s5_gate, mla_gate, gv, gla_gate, lru_u, lru_gate, ckv, cq,
            kr, _swap_halves(kr), glr_pad, gq, gk]
    return jnp.concatenate(cols, axis=1).astype(BF16)


def _mla_q_weight(w_uq):
    w = w_uq.reshape(MLA_Q_RANK, MLA_HEADS, MLA_NOPE + MLA_ROPE)
    nope = w[:, :, :MLA_NOPE].reshape(MLA_Q_RANK, MLA_HEADS * MLA_NOPE)
    rope = w[:, :, MLA_NOPE:]
    pad = ((0, 0), (0, 0), (0, 128 - MLA_ROPE))
    rope_p = jnp.pad(rope, pad).reshape(MLA_Q_RANK, MLA_HEADS * 128)
    rope_s = jnp.pad(_swap_halves(rope), pad).reshape(MLA_Q_RANK, MLA_HEADS * 128)
    return jnp.concatenate([nope, rope_p, rope_s], axis=1).astype(BF16)


def _rope_tables():
    half = MLA_ROPE // 2
    inv = 1.0 / (ROPE_THETA ** (jnp.arange(half, dtype=F32) / half))
    zeros = jnp.zeros((128 - MLA_ROPE,), F32)
    invf = jnp.concatenate([inv, inv, zeros]).reshape(1, 128)
    sgn = jnp.concatenate([-jnp.ones((half,), F32), jnp.ones((half,), F32), zeros]).reshape(1, 128)
    msk = jnp.concatenate([jnp.ones((MLA_ROPE,), F32), zeros]).reshape(1, 128)
    return invf, sgn, msk


def _gla_tril(tg):
    r = np.arange(tg)
    m = (r[None, :] <= r[:, None]) & ((r[:, None] // GLA_CHUNK) == (r[None, :] // GLA_CHUNK))
    return jnp.asarray(m, dtype=BF16)


def kernel(x, positions, norm_g, w_in, s5_lambda_re, s5_lambda_im, s5_b_re, s5_b_im, s5_c_re, s5_c_im,
           s5_d, s5_log_dt, s5_w_glu, s5_b_glu, s5_out_g, mla_q_norm_g, mla_kv_norm_g, mla_w_uq,
           mla_w_uk, mla_w_uv, mla_out_g, gla_w_gate, gla_b_gate, gla_out_g, lru_conv_w, lru_conv_b,
           lru_w_a, lru_b_a, lru_w_x, lru_b_x, lru_lambda, lru_out_g, w_out, final_g):
    nb, seq, d = x.shape
    depth = w_in.shape[0]
    t = nb * seq
    row = lambda v: v.reshape(1, -1).astype(F32)
    x2 = x.reshape(t, d)
    pos2 = positions.reshape(t, 1)
    invf, sgn, msk = _rope_tables()
    gla_tg = 256
    tril = _gla_tril(gla_tg)
    w_out_bf = w_out.astype(BF16)

    for l in range(depth):
        z2 = _inproj(x2, row(norm_g[l]), _inproj_weight(w_in[l]))
        z3 = z2.reshape(nb, seq, Z_COLS)

        wb, lamr, lami, wcr, wci = _s5_params(s5_lambda_re[l], s5_lambda_im[l], s5_b_re[l], s5_b_im[l],
                                              s5_c_re[l], s5_c_im[l], s5_log_dt[l], nb)
        y_s5 = _s5(z3, wb, lamr, lami, wcr, wci, row(s5_d[l]), s5_w_glu[l].astype(BF16),
                   row(s5_b_glu[l]), row(s5_out_g[l]))

        q2, k2, vt3 = _mla_proj(z2, pos2, row(mla_q_norm_g[l]), row(mla_kv_norm_g[l]),
                                _mla_q_weight(mla_w_uq[l]), mla_w_uk[l].astype(BF16),
                                mla_w_uv[l].T.astype(BF16), invf, sgn, msk, tm=MLA_TQ)
        y_mla = _mla_attn(q2.reshape(nb, seq, 2048), k2.reshape(nb, seq, 2048),
                          vt3.reshape(nb, seq // MLA_TQ, BRANCH, MLA_TQ), z3, row(mla_out_g[l]), tq=MLA_TQ)

        wg = jnp.pad(gla_w_gate[l], ((0, 128 - GLA_GATE_RANK), (0, 0))).astype(BF16)
        y_gla = _gla(z3, wg, row(gla_b_gate[l]), row(gla_out_g[l]), tril, tg=gla_tg)

        wax = jnp.concatenate([lru_w_a[l], lru_w_x[l]], axis=-1).astype(BF16)
        sp = row(jax.nn.softplus(-lru_lambda[l].astype(F32)))
        y_lru = _lru(z3, lru_conv_w[l].astype(F32), row(lru_conv_b[l]), wax, row(lru_b_a[l]),
                     row(lru_b_x[l]), sp, row(lru_out_g[l]))

        ys = [y.reshape(t, BRANCH) for y in (y_s5, y_mla, y_gla, y_lru)]
        x2 = _outproj(ys, w_out_bf, l, x2)

    return _final_norm(x2, row(final_g)).reshape(nb, seq, d)
```

```python
import functools

import jax
import jax.numpy as jnp
import numpy as np
from jax import lax
from jax.experimental import pallas as pl
from jax.experimental.pallas import tpu as pltpu

F32 = jnp.float32
BF16 = jnp.bfloat16

D_MODEL = 4096
BRANCH = 1024
NORM_EPS = 1e-6

S5_GROUP = 16
S5_GROUPS = 64
S5_STATE = 64

MLA_NOPE = 128
MLA_ROPE = 64
MLA_HEADS = 8
MLA_Q_RANK = 768
MLA_KV_RANK = 512
ROPE_THETA = 10000.0
MLA_SCALE = (MLA_NOPE + MLA_ROPE) ** -0.5
NEG_INF = -1e30
LOG2_E = 1.4426950408889634
MLA_TQ = 512

GLA_HEADS = 4
GLA_DV = 256
GLA_DK = 128
GLA_GATE_RANK = 16
GLA_TAU = 16.0
GLA_CHUNK = 64

LRU_BLOCKS = 8
LRU_BLOCK_W = 128
LRU_CONV = 4
LRU_C = 8.0

LANES = 128
SUBLANES = 8

Z_S5_U, Z_S5_GATE, Z_MLA_GATE, Z_GLA_V, Z_GLA_GATE, Z_LRU_U, Z_LRU_GATE = range(7)
Z_CKV_512 = 14
Z_CQ_768 = 10
Z_KR_128 = 66
Z_GLR_128 = 67
Z_GLA_Q_512 = 17
Z_GLA_K_512 = 18
Z_COLS = 9728

VMEM_LIMIT = 56 * 1024 * 1024
NORM_ROWS = 128


def _cparams(n_axes):
    return pltpu.CompilerParams(dimension_semantics=("arbitrary",) * n_axes,
                                vmem_limit_bytes=VMEM_LIMIT)


def _rms(x, g):
    return x * lax.rsqrt(jnp.mean(x * x, axis=-1, keepdims=True) + NORM_EPS) * g


def _silu(x):
    return x * jax.nn.sigmoid(x)


def _inproj_kernel(x_ref, g_ref, w_ref, o_ref, h_ref):
    @pl.when(pl.program_id(1) == 0)
    def _():
        def norm_rows(r, c):
            rows = pl.ds(pl.multiple_of(r * NORM_ROWS, NORM_ROWS), NORM_ROWS)
            h_ref[rows, :] = _rms(x_ref[rows, :], g_ref[...]).astype(BF16)
            return c

        lax.fori_loop(0, x_ref.shape[0] // NORM_ROWS, norm_rows, 0)

    o_ref[...] = jnp.dot(h_ref[...], w_ref[...], preferred_element_type=F32).astype(o_ref.dtype)


def _inproj(x2, g, w_blk, layer, *, tm=1024):
    t, d = x2.shape
    _, nblk, _, tn = w_blk.shape
    return pl.pallas_call(
        _inproj_kernel,
        grid=(t // tm, nblk),
        in_specs=[pl.BlockSpec((tm, d), lambda i, j: (i, 0)),
                  pl.BlockSpec((1, d), lambda i, j: (0, 0)),
                  pl.BlockSpec((None, None, d, tn), lambda i, j: (layer, j, 0, 0))],
        out_specs=pl.BlockSpec((tm, tn), lambda i, j: (i, j)),
        out_shape=jax.ShapeDtypeStruct((t, nblk * tn), BF16),
        scratch_shapes=[pltpu.VMEM((tm, d), BF16)],
        compiler_params=_cparams(2),
        name="inproj",
    )(x2, g, w_blk)


_W_IN_MOVES = (
    (0, 0, 1024), (1024, 1024, 1024),
    (2048, 3392, 1024),
    (3072, 5440, 1024), (4096, 6480, 1024),
    (5120, 7504, 1024), (6144, 8528, 1024),
    (7168, 2816, 512), (7680, 2048, 768),
    (8448, 3328, 64), (8512, 3360, 32), (8544, 3328, 32),
    (8576, 6464, 16),
    (8704, 4416, 512), (9216, 4928, 512),
)
W_IN_TN = 512


def _w_in_prep_kernel(w_ref, o_ref):
    zb, zo = divmod(Z_GLR_128 * 128 + GLA_GATE_RANK, W_IN_TN)
    o_ref[zb, :, zo:W_IN_TN] = jnp.zeros((o_ref.shape[1], W_IN_TN - zo), BF16)
    for dst, src, width in _W_IN_MOVES:
        while width > 0:
            blk, off = dst // W_IN_TN, dst % W_IN_TN
            n = min(width, W_IN_TN - off)
            o_ref[blk, :, off:off + n] = w_ref[:, src:src + n].astype(BF16)
            dst, src, width = dst + n, src + n, width - n


def _w_in_prep(w_in, *, tr=256):
    depth, d, n_in = w_in.shape
    nblk = Z_COLS // W_IN_TN
    return pl.pallas_call(
        _w_in_prep_kernel,
        grid=(depth, d // tr),
        in_specs=[pl.BlockSpec((None, tr, n_in), lambda l, i: (l, i, 0))],
        out_specs=pl.BlockSpec((None, nblk, tr, W_IN_TN), lambda l, i: (l, 0, i, 0)),
        out_shape=jax.ShapeDtypeStruct((depth, nblk, d, W_IN_TN), BF16),
        compiler_params=_cparams(2),
        name="w_in_prep",
    )(w_in)


def _s5_kernel(u_ref, gate_ref, wb_ref, lamr_ref, lami_ref, wcr_ref, wci_ref, d_ref,
               wglu_ref, bglu_ref, g_ref, o_ref,
               bre_ref, bim_ref, hre_ref, him_ref, y_ref, *, nb, tc, pitch):
    @pl.when(pl.program_id(0) == 0)
    def _():
        hre_ref[...] = jnp.zeros_like(hre_ref)
        him_ref[...] = jnp.zeros_like(him_ref)

    ub = u_ref[...].reshape(nb * tc, BRANCH)

    for kb in range(8):
        res = jnp.dot(ub[:, kb * 128:(kb + 1) * 128], wb_ref[kb], preferred_element_type=F32)
        half = kb // 4
        for b in range(nb):
            row0 = (2 * b + half) * pitch
            for jj in range(4):
                j = (kb % 4) * 4 + jj
                bre_ref[j, pl.ds(row0, tc), :] = res[b * tc:(b + 1) * tc, jj * 128:(jj + 1) * 128]
                bim_ref[j, pl.ds(row0, tc), :] = res[b * tc:(b + 1) * tc,
                                                     512 + jj * 128:512 + (jj + 1) * 128]

    for cb in range(4):
        lr = [lamr_ref[:, (cb * 4 + jj) * 128:(cb * 4 + jj + 1) * 128] for jj in range(4)]
        li = [lami_ref[:, (cb * 4 + jj) * 128:(cb * 4 + jj + 1) * 128] for jj in range(4)]
        init = []
        for jj in range(4):
            init.append(hre_ref[:, (cb * 4 + jj) * 128:(cb * 4 + jj + 1) * 128])
            init.append(him_ref[:, (cb * 4 + jj) * 128:(cb * 4 + jj + 1) * 128])

        def body(t, carry, cb=cb, lr=lr, li=li):
            new = []
            for jj in range(4):
                j = cb * 4 + jj
                hr, hi = carry[2 * jj], carry[2 * jj + 1]
                idx = pl.ds(t, SUBLANES, stride=pitch)
                nr = lr[jj] * hr - li[jj] * hi + bre_ref[j, idx, :]
                ni = lr[jj] * hi + li[jj] * hr + bim_ref[j, idx, :]
                bre_ref[j, idx, :] = nr
                bim_ref[j, idx, :] = ni
                new += [nr, ni]
            return tuple(new)

        fin = lax.fori_loop(0, tc, body, tuple(init), unroll=4)
        for jj in range(4):
            hre_ref[:, (cb * 4 + jj) * 128:(cb * 4 + jj + 1) * 128] = fin[2 * jj]
            him_ref[:, (cb * 4 + jj) * 128:(cb * 4 + jj + 1) * 128] = fin[2 * jj + 1]

    for kb in range(8):
        half = kb // 4
        sre, sim = [], []
        for b in range(nb):
            rows = pl.ds((2 * b + half) * pitch, tc)
            sre.append(jnp.concatenate([bre_ref[(kb % 4) * 4 + jj, rows, :] for jj in range(4)], axis=-1))
            sim.append(jnp.concatenate([bim_ref[(kb % 4) * 4 + jj, rows, :] for jj in range(4)], axis=-1))
        sre = jnp.concatenate(sre, axis=0).astype(BF16)
        sim = jnp.concatenate(sim, axis=0).astype(BF16)
        ykb = (jnp.dot(sre, wcr_ref[kb], preferred_element_type=F32)
               + jnp.dot(sim, wci_ref[kb], preferred_element_type=F32))
        cols = slice(kb * 128, (kb + 1) * 128)
        y_ref[:, cols] = ykb + d_ref[:, cols] * ub[:, cols].astype(F32)

    y = jax.nn.gelu(y_ref[...], approximate=True)
    zg = jnp.dot(y.astype(BF16), wglu_ref[...], preferred_element_type=F32) + bglu_ref[...]
    y = y * jax.nn.sigmoid(zg)
    gate = gate_ref[...].reshape(nb * tc, BRANCH).astype(F32)
    o_ref[...] = (_rms(y, g_ref[...]) * _silu(gate)).astype(BF16).reshape(nb, tc, BRANCH)


def _s5(z3, wb, lamr, lami, wcr, wci, d, wglu, bglu, g, *, tc=128):
    nb, seq, _ = z3.shape
    assert 2 * nb == SUBLANES
    pitch = tc + SUBLANES
    full = lambda a: pl.BlockSpec(a.shape, lambda i, _n=a.ndim: (0,) * _n)
    kern = functools.partial(_s5_kernel, nb=nb, tc=tc, pitch=pitch)
    return pl.pallas_call(
        kern,
        grid=(seq // tc,),
        in_specs=[pl.BlockSpec((nb, tc, BRANCH), lambda i: (0, i, Z_S5_U)),
                  pl.BlockSpec((nb, tc, BRANCH), lambda i: (0, i, Z_S5_GATE)),
                  full(wb), full(lamr), full(lami), full(wcr), full(wci), full(d),
                  full(wglu), full(bglu), full(g)],
        out_specs=pl.BlockSpec((nb, tc, BRANCH), lambda i: (0, i, 0)),
        out_shape=jax.ShapeDtypeStruct((nb, seq, BRANCH), BF16),
        scratch_shapes=[pltpu.VMEM((16, SUBLANES * pitch, LANES), F32),
                        pltpu.VMEM((16, SUBLANES * pitch, LANES), F32),
                        pltpu.VMEM((SUBLANES, 2048), F32),
                        pltpu.VMEM((SUBLANES, 2048), F32),
                        pltpu.VMEM((nb * tc, BRANCH), F32)],
        compiler_params=_cparams(1),
        name="s5",
    )(z3, z3, wb, lamr, lami, wcr, wci, d, wglu, bglu, g)


def _s5_params(lam_re, lam_im, b_re, b_im, c_re, c_im, log_dt, nb):
    dt = jnp.exp(log_dt.astype(F32))[:, None]
    lr, li = lam_re.astype(F32), lam_im.astype(F32)
    mag = jnp.exp(lr * dt)
    lbr, lbi = mag * jnp.cos(li * dt), mag * jnp.sin(li * dt)
    den = lr * lr + li * li
    cr = ((lbr - 1.0) * lr + lbi * li) / den
    ci = (lbi * lr - (lbr - 1.0) * li) / den
    bbr = cr[..., None] * b_re - ci[..., None] * b_im
    bbi = cr[..., None] * b_im + ci[..., None] * b_re
    eye = jnp.eye(8, dtype=F32)

    def in_layout(m):
        m = m.reshape(8, 8, S5_STATE, S5_GROUP)
        return jnp.einsum('kgph,gG->kghGp', m, eye).reshape(8, 128, 512)

    def out_layout(m):
        m = m.reshape(8, 8, S5_GROUP, S5_STATE)
        return jnp.einsum('kghp,gG->kgpGh', m, eye).reshape(8, 512, 128)

    wb = jnp.concatenate([in_layout(bbr), in_layout(bbi)], axis=-1).astype(BF16)
    wcr = out_layout(c_re.astype(F32)).astype(BF16)
    wci = out_layout(-c_im.astype(F32)).astype(BF16)

    def seq_layout(v):
        return jnp.tile(v.reshape(1, 2, 2048), (nb, 1, 1)).reshape(2 * nb, 2048)

    return wb, seq_layout(lbr), seq_layout(lbi), wcr, wci


def _lru_kernel(u_ref, gate_ref, cw_ref, cb_ref, wax_ref, ba_ref, bx_ref, sp_ref, g_ref, o_ref,
                xpad_ref, a_ref, h_ref, st_ref, *, nb, tc, pitch):
    @pl.when(pl.program_id(0) == 0)
    def _():
        xpad_ref[:, 0:SUBLANES, :] = jnp.zeros((nb, SUBLANES, BRANCH), F32)
        st_ref[...] = jnp.zeros_like(st_ref)

    xpad_ref[:, SUBLANES:SUBLANES + tc, :] = u_ref[...].astype(F32)
    xc = cb_ref[...].reshape(1, 1, BRANCH)
    for k in range(LRU_CONV):
        off = SUBLANES - (LRU_CONV - 1) + k
        xc = xc + cw_ref[k:k + 1, :].reshape(1, 1, BRANCH) * xpad_ref[:, off:off + tc, :]
    xpad_ref[:, 0:SUBLANES, :] = xpad_ref[:, tc:tc + SUBLANES, :]
    x2 = xc.reshape(nb * tc, BRANCH)
    xb = x2.astype(BF16)

    for hb in range(LRU_BLOCKS):
        cols = slice(hb * 128, (hb + 1) * 128)
        ri = jnp.dot(xb[:, cols], wax_ref[hb], preferred_element_type=F32)
        r = jax.nn.sigmoid(ri[:, :128] + ba_ref[:, cols])
        ig = jax.nn.sigmoid(ri[:, 128:] + bx_ref[:, cols])
        log_a = (-LRU_C) * r * sp_ref[:, cols]
        a = jnp.exp(log_a)
        gated = jnp.sqrt(-jnp.tanh(log_a) * (a * a + 1.0)) * (ig * x2[:, cols])
        half, j = hb // 4, hb % 4
        for b in range(nb):
            rows = pl.ds((2 * b + half) * pitch, tc)
            a_ref[j, rows, :] = a[b * tc:(b + 1) * tc]
            h_ref[j, rows, :] = gated[b * tc:(b + 1) * tc]

    def body(t, carry):
        new = []
        for j in range(4):
            idx = pl.ds(t, SUBLANES, stride=pitch)
            h = a_ref[j, idx, :] * carry[j] + h_ref[j, idx, :]
            h_ref[j, idx, :] = h
            new.append(h)
        return tuple(new)

    init = tuple(st_ref[:, j * 128:(j + 1) * 128] for j in range(4))
    fin = lax.fori_loop(0, tc, body, init, unroll=8)
    for j in range(4):
        st_ref[:, j * 128:(j + 1) * 128] = fin[j]

    for b in range(nb):
        parts = [h_ref[j, pl.ds((2 * b + half) * pitch, tc), :] for half in range(2) for j in range(4)]
        hb_ = jnp.concatenate(parts, axis=-1)
        o_ref[b] = (_rms(hb_, g_ref[...]) * _silu(gate_ref[b].astype(F32))).astype(BF16)


def _lru(z3, cw, cb, wax, ba, bx, sp, g, *, tc=128):
    nb, seq, _ = z3.shape
    assert 2 * nb == SUBLANES
    pitch = tc + SUBLANES
    full = lambda a: pl.BlockSpec(a.shape, lambda i, _n=a.ndim: (0,) * _n)
    kern = functools.partial(_lru_kernel, nb=nb, tc=tc, pitch=pitch)
    return pl.pallas_call(
        kern,
        grid=(seq // tc,),
        in_specs=[pl.BlockSpec((nb, tc, BRANCH), lambda i: (0, i, Z_LRU_U)),
                  pl.BlockSpec((nb, tc, BRANCH), lambda i: (0, i, Z_LRU_GATE)),
                  full(cw), full(cb), full(wax), full(ba), full(bx), full(sp), full(g)],
        out_specs=pl.BlockSpec((nb, tc, BRANCH), lambda i: (0, i, 0)),
        out_shape=jax.ShapeDtypeStruct((nb, seq, BRANCH), BF16),
        scratch_shapes=[pltpu.VMEM((nb, tc + SUBLANES, BRANCH), F32),
                        pltpu.VMEM((4, SUBLANES * pitch, LANES), F32),
                        pltpu.VMEM((4, SUBLANES * pitch, LANES), F32),
                        pltpu.VMEM((SUBLANES, 512), F32)],
        compiler_params=_cparams(1),
        name="rglru",
    )(z3, z3, cw, cb, wax, ba, bx, sp, g)


def _gla_kernel(q_ref, k_ref, v_ref, glr_ref, gate_ref, wg_ref, bg_ref, og_ref, tril_ref, o_ref,
                st_ref, *, tg):
    @pl.when(pl.program_id(1) == 0)
    def _():
        st_ref[...] = jnp.zeros_like(st_ref)

    n_ch = tg // GLA_CHUNK
    x = jnp.dot(glr_ref[...], wg_ref[...], preferred_element_type=F32) + bg_ref[...]
    log_a = (jnp.minimum(x, 0.0) - jnp.log1p(jnp.exp(-jnp.abs(x)))) / GLA_TAU

    tril = tril_ref[...]
    p0 = log_a.astype(BF16)
    r0 = log_a - p0.astype(F32)
    p1 = r0.astype(BF16)
    p2 = (r0 - p1.astype(F32)).astype(BF16)
    cum = (jnp.dot(tril, p0, preferred_element_type=F32)
           + jnp.dot(tril, p1, preferred_element_type=F32)
           + jnp.dot(tril, p2, preferred_element_type=F32))

    row = lax.broadcasted_iota(jnp.int32, (tg, tg), 0)
    col = lax.broadcasted_iota(jnp.int32, (tg, tg), 1)
    causal = (col <= row) & ((row // GLA_CHUNK) == (col // GLA_CHUNK))
    nt = (((1,), (1,)), ((), ()))
    tn = (((0,), (0,)), ((), ()))

    for hh in range(GLA_HEADS):
        kc = slice(hh * GLA_DK, (hh + 1) * GLA_DK)
        vc = slice(hh * GLA_DV, (hh + 1) * GLA_DV)
        cumh = cum[:, kc]
        kh = k_ref[:, kc].astype(F32)
        qd = (q_ref[:, kc].astype(F32) * (GLA_DK ** -0.5)) * jnp.exp(cumh)
        ki = kh * jnp.exp(-cumh)
        qdb = qd.astype(BF16)
        att = lax.dot_general(qdb, ki.astype(BF16), nt, preferred_element_type=F32)
        att = jnp.where(causal, att, 0.0)
        vb = v_ref[:, vc]
        o = jnp.dot(att.astype(BF16), vb, preferred_element_type=F32)
        st = st_ref[hh]
        inter = []
        for c in range(n_ch):
            rows = slice(c * GLA_CHUNK, (c + 1) * GLA_CHUNK)
            tot = cumh[(c + 1) * GLA_CHUNK - 1:(c + 1) * GLA_CHUNK, :]
            kd = kh[rows] * jnp.exp(tot - cumh[rows])
            inter.append(lax.dot_general(qdb[rows], st.astype(BF16), nt, preferred_element_type=F32))
            kv_t = lax.dot_general(vb[rows], kd.astype(BF16), tn, preferred_element_type=F32)
            st = st * jnp.exp(tot) + kv_t
        st_ref[hh] = st
        o = o + jnp.concatenate(inter, axis=0)
        gate = gate_ref[:, vc].astype(F32)
        o_ref[:, vc] = (_rms(o, og_ref[:, vc]) * _silu(gate)).astype(BF16)


def _gla(z3, wg, bg, og, tril, *, tg=256):
    nb, seq, _ = z3.shape
    kern = functools.partial(_gla_kernel, tg=tg)
    full = lambda a: pl.BlockSpec(a.shape, lambda b, i, _n=a.ndim: (0,) * _n)
    return pl.pallas_call(
        kern,
        grid=(nb, seq // tg),
        in_specs=[pl.BlockSpec((None, tg, 512), lambda b, i: (b, i, Z_GLA_Q_512)),
                  pl.BlockSpec((None, tg, 512), lambda b, i: (b, i, Z_GLA_K_512)),
                  pl.BlockSpec((None, tg, BRANCH), lambda b, i: (b, i, Z_GLA_V)),
                  pl.BlockSpec((None, tg, 128), lambda b, i: (b, i, Z_GLR_128)),
                  pl.BlockSpec((None, tg, BRANCH), lambda b, i: (b, i, Z_GLA_GATE)),
                  full(wg), full(bg), full(og), full(tril)],
        out_specs=pl.BlockSpec((None, tg, BRANCH), lambda b, i: (b, i, 0)),
        out_shape=jax.ShapeDtypeStruct((nb, seq, BRANCH), BF16),
        scratch_shapes=[pltpu.VMEM((GLA_HEADS, GLA_DV, GLA_DK), F32)],
        compiler_params=_cparams(2),
        name="gla",
    )(z3, z3, z3, z3, z3, wg, bg, og, tril)


def _mla_proj_kernel(cq_ref, ckv_ref, kr_ref, pos_ref, gq_ref, gkv_ref, wq_ref, wk_ref, wvt_ref,
                     invf_ref, sgn_ref, msk_ref, q_out, k_out, vt_out):
    cqn = _rms(cq_ref[...].astype(F32), gq_ref[...]).astype(BF16)
    qa = jnp.dot(cqn, wq_ref[...], preferred_element_type=F32)
    ang = pos_ref[...].astype(F32) * invf_ref[...]
    cosm = jnp.cos(ang) * msk_ref[...]
    sinm = jnp.sin(ang) * sgn_ref[...]
    nh = MLA_HEADS * 128
    qscale = MLA_SCALE * LOG2_E
    for h in range(MLA_HEADS):
        q_out[:, h * 256:h * 256 + 128] = (qa[:, h * 128:(h + 1) * 128] * qscale).astype(BF16)
        qr = qa[:, nh + h * 128:nh + (h + 1) * 128] * cosm + qa[:, 2 * nh + h * 128:2 * nh + (h + 1) * 128] * sinm
        q_out[:, h * 256 + 128:(h + 1) * 256] = (qr * qscale).astype(BF16)
    c = _rms(ckv_ref[...].astype(F32), gkv_ref[...]).astype(BF16)
    kn = jnp.dot(c, wk_ref[...], preferred_element_type=F32)
    kr = kr_ref[...].astype(F32)
    krp = (kr * cosm + pltpu.roll(kr, 64, 1) * sinm).astype(BF16)
    for h in range(MLA_HEADS):
        k_out[:, h * 256:h * 256 + 128] = kn[:, h * 128:(h + 1) * 128].astype(BF16)
        k_out[:, h * 256 + 128:(h + 1) * 256] = krp
    nt = (((1,), (1,)), ((), ()))
    vt_out[...] = lax.dot_general(wvt_ref[...], c, nt, preferred_element_type=F32).astype(BF16)


def _mla_proj(z2, pos2, gq, gkv, wq, wk, wvt, invf, sgn, msk, *, tm):
    t = z2.shape[0]
    full = lambda a: pl.BlockSpec(a.shape, lambda i, _n=a.ndim: (0,) * _n)
    return pl.pallas_call(
        _mla_proj_kernel,
        grid=(t // tm,),
        in_specs=[pl.BlockSpec((tm, MLA_Q_RANK), lambda i: (i, Z_CQ_768)),
                  pl.BlockSpec((tm, MLA_KV_RANK), lambda i: (i, Z_CKV_512)),
                  pl.BlockSpec((tm, 128), lambda i: (i, Z_KR_128)),
                  pl.BlockSpec((tm, 1), lambda i: (i, 0)),
                  full(gq), full(gkv), full(wq), full(wk), full(wvt), full(invf), full(sgn), full(msk)],
        out_specs=[pl.BlockSpec((tm, 2048), lambda i: (i, 0)),
                   pl.BlockSpec((tm, 2048), lambda i: (i, 0)),
                   pl.BlockSpec((None, BRANCH, tm), lambda i: (i, 0, 0))],
        out_shape=[jax.ShapeDtypeStruct((t, 2048), BF16),
                   jax.ShapeDtypeStruct((t, 2048), BF16),
                   jax.ShapeDtypeStruct((t // tm, BRANCH, tm), BF16)],
        compiler_params=_cparams(1),
        name="mla_proj",
    )(z2, z2, z2, pos2, gq, gkv, wq, wk, wvt, invf, sgn, msk)


def _mla_attn_kernel(q_ref, k_ref, vt_ref, gate_ref, g_ref, o_ref, m_ref, l_ref, acc_ref, o_scr, *, tq):
    qi = pl.program_id(1)
    nt = (((1,), (1,)), ((), ()))
    key = lax.broadcasted_iota(jnp.int32, (tq, tq), 0)
    qry = lax.broadcasted_iota(jnp.int32, (tq, tq), 1)

    m_ref[...] = jnp.full(m_ref.shape, NEG_INF, F32)
    l_ref[...] = jnp.zeros_like(l_ref)
    acc_ref[...] = jnp.zeros_like(acc_ref)

    def step(j, masked):
        r0 = pl.multiple_of(j * tq, tq)
        for h in range(MLA_HEADS):
            st = lax.dot_general(k_ref[pl.ds(r0, tq), h * 256:(h + 1) * 256],
                                 q_ref[:, h * 256:(h + 1) * 256], nt,
                                 preferred_element_type=F32)
            if masked:
                st = jnp.where(key <= qry, st, NEG_INF)
            m_old = m_ref[h]
            m_new = jnp.maximum(m_old, jnp.max(st, axis=0, keepdims=True))
            alpha = jnp.exp2(m_old - m_new)
            pt = jnp.exp2(st - m_new)
            l_ref[h] = alpha * l_ref[h] + jnp.sum(pt, axis=0, keepdims=True)
            acc_ref[h] = alpha * acc_ref[h] + jnp.dot(
                vt_ref[j, h * 128:(h + 1) * 128, :], pt.astype(BF16), preferred_element_type=F32)
            m_ref[h] = m_new

    def body(j, c):
        step(j, False)
        return c

    lax.fori_loop(0, qi, body, 0)
    step(qi, True)

    for h in range(MLA_HEADS):
        o_scr[:, h * 128:(h + 1) * 128] = (acc_ref[h] / l_ref[h]).T
    o_ref[...] = (_rms(o_scr[...], g_ref[...]) * _silu(gate_ref[...].astype(F32))).astype(BF16)


def _mla_attn(q3, k3, vt4, z3, g, *, tq):
    nb, seq, _ = q3.shape
    kern = functools.partial(_mla_attn_kernel, tq=tq)
    return pl.pallas_call(
        kern,
        grid=(nb, seq // tq),
        in_specs=[pl.BlockSpec((None, tq, 2048), lambda b, i: (b, i, 0)),
                  pl.BlockSpec((None, seq, 2048), lambda b, i: (b, 0, 0)),
                  pl.BlockSpec((None, seq // tq, BRANCH, tq), lambda b, i: (b, 0, 0, 0)),
                  pl.BlockSpec((None, tq, BRANCH), lambda b, i: (b, i, Z_MLA_GATE)),
                  pl.BlockSpec((1, BRANCH), lambda b, i: (0, 0))],
        out_specs=pl.BlockSpec((None, tq, BRANCH), lambda b, i: (b, i, 0)),
        out_shape=jax.ShapeDtypeStruct((nb, seq, BRANCH), BF16),
        scratch_shapes=[pltpu.VMEM((MLA_HEADS, 1, tq), F32),
                        pltpu.VMEM((MLA_HEADS, 1, tq), F32),
                        pltpu.VMEM((MLA_HEADS, 128, tq), F32),
                        pltpu.VMEM((tq, BRANCH), F32)],
        compiler_params=_cparams(2),
        name="mla_attn",
    )(q3, k3, vt4, z3, g)


def _outproj_kernel(y0_ref, y1_ref, y2_ref, y3_ref, w_ref, x_ref, o_ref):
    acc = x_ref[...]
    for n, y_ref in enumerate((y0_ref, y1_ref, y2_ref, y3_ref)):
        acc = acc + jnp.dot(y_ref[...], w_ref[n * BRANCH:(n + 1) * BRANCH, :], preferred_element_type=F32)
    o_ref[...] = acc


def _outproj(ys, w_all, layer, x2, *, tm=1024, tn=512):
    t, d = x2.shape
    yspec = pl.BlockSpec((tm, BRANCH), lambda i, j: (i, 0))
    return pl.pallas_call(
        _outproj_kernel,
        grid=(t // tm, d // tn),
        in_specs=[yspec, yspec, yspec, yspec,
                  pl.BlockSpec((None, 4 * BRANCH, tn), lambda i, j: (layer, 0, j)),
                  pl.BlockSpec((tm, tn), lambda i, j: (i, j))],
        out_specs=pl.BlockSpec((tm, tn), lambda i, j: (i, j)),
        out_shape=jax.ShapeDtypeStruct((t, d), F32),
        compiler_params=_cparams(2),
        name="outproj",
    )(*ys, w_all, x2)


def _final_norm_kernel(x_ref, g_ref, o_ref):
    o_ref[...] = _rms(x_ref[...], g_ref[...])


def _final_norm(x2, g, *, tm=512):
    t, d = x2.shape
    return pl.pallas_call(
        _final_norm_kernel,
        grid=(t // tm,),
        in_specs=[pl.BlockSpec((tm, d), lambda i: (i, 0)), pl.BlockSpec((1, d), lambda i: (0, 0))],
        out_specs=pl.BlockSpec((tm, d), lambda i: (i, 0)),
        out_shape=jax.ShapeDtypeStruct((t, d), F32),
        compiler_params=_cparams(1),
        name="final_norm",
    )(x2, g)


def _swap_halves(w, axis=-1):
    a, b = jnp.split(w, 2, axis=axis)
    return jnp.concatenate([b, a], axis=axis)


def _mla_q_weight(w_uq):
    w = w_uq.reshape(MLA_Q_RANK, MLA_HEADS, MLA_NOPE + MLA_ROPE)
    nope = w[:, :, :MLA_NOPE].reshape(MLA_Q_RANK, MLA_HEADS * MLA_NOPE)
    rope = w[:, :, MLA_NOPE:]
    pad = ((0, 0), (0, 0), (0, 128 - MLA_ROPE))
    rope_p = jnp.pad(rope, pad).reshape(MLA_Q_RANK, MLA_HEADS * 128)
    rope_s = jnp.pad(_swap_halves(rope), pad).reshape(MLA_Q_RANK, MLA_HEADS * 128)
    return jnp.concatenate([nope, rope_p, rope_s], axis=1).astype(BF16)


def _rope_tables():
    half = MLA_ROPE // 2
    inv = 1.0 / (ROPE_THETA ** (jnp.arange(half, dtype=F32) / half))
    zeros = jnp.zeros((128 - MLA_ROPE,), F32)
    invf = jnp.concatenate([inv, inv, zeros]).reshape(1, 128)
    sgn = jnp.concatenate([-jnp.ones((half,), F32), jnp.ones((half,), F32), zeros]).reshape(1, 128)
    msk = jnp.concatenate([jnp.ones((MLA_ROPE,), F32), zeros]).reshape(1, 128)
    return invf, sgn, msk


def _gla_tril(tg):
    r = np.arange(tg)
    m = (r[None, :] <= r[:, None]) & ((r[:, None] // GLA_CHUNK) == (r[None, :] // GLA_CHUNK))
    return jnp.asarray(m, dtype=BF16)


def kernel(x, positions, norm_g, w_in, s5_lambda_re, s5_lambda_im, s5_b_re, s5_b_im, s5_c_re, s5_c_im,
           s5_d, s5_log_dt, s5_w_glu, s5_b_glu, s5_out_g, mla_q_norm_g, mla_kv_norm_g, mla_w_uq,
           mla_w_uk, mla_w_uv, mla_out_g, gla_w_gate, gla_b_gate, gla_out_g, lru_conv_w, lru_conv_b,
           lru_w_a, lru_b_a, lru_w_x, lru_b_x, lru_lambda, lru_out_g, w_out, final_g):
    nb, seq, d = x.shape
    depth = w_in.shape[0]
    t = nb * seq
    row = lambda v: v.reshape(1, -1).astype(F32)
    x2 = x.reshape(t, d)
    pos2 = positions.reshape(t, 1)
    invf, sgn, msk = _rope_tables()
    gla_tg = 256
    tril = _gla_tril(gla_tg)
    w_out_bf = w_out.astype(BF16)
    w_in_blk = _w_in_prep(w_in)

    for l in range(depth):
        z2 = _inproj(x2, row(norm_g[l]), w_in_blk, l)
        z3 = z2.reshape(nb, seq, Z_COLS)

        wb, lamr, lami, wcr, wci = _s5_params(s5_lambda_re[l], s5_lambda_im[l], s5_b_re[l], s5_b_im[l],
                                              s5_c_re[l], s5_c_im[l], s5_log_dt[l], nb)
        y_s5 = _s5(z3, wb, lamr, lami, wcr, wci, row(s5_d[l]), s5_w_glu[l].astype(BF16),
                   row(s5_b_glu[l]), row(s5_out_g[l]))

        q2, k2, vt3 = _mla_proj(z2, pos2, row(mla_q_norm_g[l]), row(mla_kv_norm_g[l]),
                                _mla_q_weight(mla_w_uq[l]), mla_w_uk[l].astype(BF16),
                                mla_w_uv[l].T.astype(BF16), invf, sgn, msk, tm=MLA_TQ)
        y_mla = _mla_attn(q2.reshape(nb, seq, 2048), k2.reshape(nb, seq, 2048),
                          vt3.reshape(nb, seq // MLA_TQ, BRANCH, MLA_TQ), z3, row(mla_out_g[l]), tq=MLA_TQ)

        wg = jnp.pad(gla_w_gate[l], ((0, 128 - GLA_GATE_RANK), (0, 0))).astype(BF16)
        y_gla = _gla(z3, wg, row(gla_b_gate[l]), row(gla_out_g[l]), tril, tg=gla_tg)

        wax = jnp.concatenate([lru_w_a[l], lru_w_x[l]], axis=-1).astype(BF16)
        sp = row(jax.nn.softplus(-lru_lambda[l].astype(F32)))
        y_lru = _lru(z3, lru_conv_w[l].astype(F32), row(lru_conv_b[l]), wax, row(lru_b_a[l]),
                     row(lru_b_x[l]), sp, row(lru_out_g[l]))

        ys = [y.reshape(t, BRANCH) for y in (y_s5, y_mla, y_gla, y_lru)]
        x2 = _outproj(ys, w_out_bf, l, x2)

    return _final_norm(x2, row(final_g)).reshape(nb, seq, d)
```

```python
import functools

import jax
import jax.numpy as jnp
import numpy as np
from jax import lax
from jax.experimental import pallas as pl
from jax.experimental.pallas import tpu as pltpu

F32 = jnp.float32
BF16 = jnp.bfloat16

D_MODEL = 4096
BRANCH = 1024
NORM_EPS = 1e-6

S5_GROUP = 16
S5_GROUPS = 64
S5_STATE = 64

MLA_NOPE = 128
MLA_ROPE = 64
MLA_HEADS = 8
MLA_Q_RANK = 768
MLA_KV_RANK = 512
ROPE_THETA = 10000.0
MLA_SCALE = (MLA_NOPE + MLA_ROPE) ** -0.5
NEG_INF = -1e30
LOG2_E = 1.4426950408889634
MLA_TQ = 512

GLA_HEADS = 4
GLA_DV = 256
GLA_DK = 128
GLA_GATE_RANK = 16
GLA_TAU = 16.0
GLA_CHUNK = 64

LRU_BLOCKS = 8
LRU_BLOCK_W = 128
LRU_CONV = 4
LRU_C = 8.0

LANES = 128
SUBLANES = 8

Z_S5_U, Z_S5_GATE, Z_MLA_GATE, Z_GLA_V, Z_GLA_GATE, Z_LRU_U, Z_LRU_GATE = range(7)
Z_CKV_512 = 14
Z_CQ_768 = 10
Z_KR_128 = 66
Z_GLR_128 = 67
Z_GLA_Q_512 = 17
Z_GLA_K_512 = 18
Z_COLS = 9728

VMEM_LIMIT = 56 * 1024 * 1024
NORM_ROWS = 128
S5_SCAN_W = 8


def _cparams(n_axes):
    return pltpu.CompilerParams(dimension_semantics=("arbitrary",) * n_axes,
                                vmem_limit_bytes=VMEM_LIMIT)


def _rms(x, g):
    return x * lax.rsqrt(jnp.mean(x * x, axis=-1, keepdims=True) + NORM_EPS) * g


def _silu(x):
    return x * jax.nn.sigmoid(x)


def _inproj_kernel(rows_ref, x_ref, g_ref, w_ref, o_ref, h_ref):
    del rows_ref
    @pl.when(pl.program_id(1) == 0)
    def _():
        def norm_rows(r, c):
            rows = pl.ds(pl.multiple_of(r * NORM_ROWS, NORM_ROWS), NORM_ROWS)
            h_ref[rows, :] = _rms(x_ref[rows, :], g_ref[...]).astype(BF16)
            return c

        lax.fori_loop(0, x_ref.shape[0] // NORM_ROWS, norm_rows, 0)

    nt = (((1,), (1,)), ((), ()))
    o_ref[...] = lax.dot_general(h_ref[...], w_ref[0], nt, preferred_element_type=F32).astype(o_ref.dtype)


W_IN_TN = 512
W_IN_ROWS = 9552
_W_IN_BLOCK_ROWS = (0, 512, 1024, 1536,
                    3392, 3904,
                    5440, 5952, 6480, 6992,
                    7504, 8016, 8528, 9040,
                    2816, 2048,
                    W_IN_ROWS,
                    4416, 4928)


def _inproj(x2, g, w_t, layer, *, tm=1024):
    t, d = x2.shape
    tn = W_IN_TN
    nblk = len(_W_IN_BLOCK_ROWS)
    rows = jnp.asarray(_W_IN_BLOCK_ROWS, jnp.int32)
    grid_spec = pltpu.PrefetchScalarGridSpec(
        num_scalar_prefetch=1,
        grid=(t // tm, nblk),
        in_specs=[pl.BlockSpec((tm, d), lambda i, j, r: (i, 0)),
                  pl.BlockSpec((1, d), lambda i, j, r: (0, 0)),
                  pl.BlockSpec((pl.Element(1), pl.Element(tn), pl.Element(d)),
                               lambda i, j, r: (layer, pl.multiple_of(r[j], 16), 0))],
        out_specs=pl.BlockSpec((tm, tn), lambda i, j, r: (i, j)),
        scratch_shapes=[pltpu.VMEM((tm, d), BF16)])
    return pl.pallas_call(
        _inproj_kernel,
        grid_spec=grid_spec,
        out_shape=jax.ShapeDtypeStruct((t, nblk * tn), BF16),
        compiler_params=_cparams(2),
        name="inproj",
    )(rows, x2, g, w_t)


def _w_in_t(w_in):
    wt = jnp.swapaxes(w_in, 1, 2)
    depth, _, d = wt.shape
    extra = [wt[:, 2560:2816], wt[:, 3328:3392], wt[:, 3360:3392], wt[:, 3328:3360],
             wt[:, 6464:6480], jnp.zeros((depth, 128 - GLA_GATE_RANK, d), wt.dtype)]
    return jnp.concatenate([wt] + extra, axis=1).astype(BF16)


def _s5_kernel(u_ref, gate_ref, wb_ref, lamr_ref, lami_ref, wcr_ref, wci_ref, d_ref,
               wglu_ref, bglu_ref, g_ref, o_ref,
               bre_ref, bim_ref, hre_ref, him_ref, y_ref, ush_ref, ysh_ref, *, nb, tc, pitch):
    shift = pitch - tc
    tcs = tc + 2 * shift

    @pl.when(pl.program_id(0) == 0)
    def _():
        hre_ref[...] = jnp.zeros_like(hre_ref)
        him_ref[...] = jnp.zeros_like(him_ref)
        ush_ref[...] = jnp.zeros_like(ush_ref)

    ub = u_ref[...].reshape(nb * tc, BRANCH)
    ush_ref[:, shift:shift + tc, :] = u_ref[...].astype(F32)
    ubs = ush_ref[...].reshape(nb * tcs, BRANCH).astype(BF16)

    def block_rows(b, half):
        if half:
            return pl.ds((2 * b + 1) * pitch - shift, tcs)
        return pl.ds(2 * b * pitch, tc)

    for kb in range(8):
        half = kb // 4
        lhs, m = (ubs, tcs) if half else (ub, tc)
        res = jnp.dot(lhs[:, kb * 128:(kb + 1) * 128], wb_ref[kb], preferred_element_type=F32)
        for b in range(nb):
            rows = block_rows(b, half)
            for jj in range(4):
                j = (kb % 4) * 4 + jj
                bre_ref[j, rows, :] = res[b * m:(b + 1) * m, jj * 128:(jj + 1) * 128]
                bim_ref[j, rows, :] = res[b * m:(b + 1) * m, 512 + jj * 128:512 + (jj + 1) * 128]

    for cb in range(16 // S5_SCAN_W):
        slabs = [cb * S5_SCAN_W + jj for jj in range(S5_SCAN_W)]
        lr = [lamr_ref[:, j * 128:(j + 1) * 128] for j in slabs]
        li = [lami_ref[:, j * 128:(j + 1) * 128] for j in slabs]
        init = []
        for j in slabs:
            init.append(hre_ref[:, j * 128:(j + 1) * 128])
            init.append(him_ref[:, j * 128:(j + 1) * 128])

        def body(t, carry, slabs=slabs, lr=lr, li=li):
            new = []
            idx = pl.ds(t, SUBLANES, stride=pitch)
            for jj, j in enumerate(slabs):
                hr, hi = carry[2 * jj], carry[2 * jj + 1]
                nr = lr[jj] * hr - li[jj] * hi + bre_ref[j, idx, :]
                ni = lr[jj] * hi + li[jj] * hr + bim_ref[j, idx, :]
                bre_ref[j, idx, :] = nr
                bim_ref[j, idx, :] = ni
                new += [nr, ni]
            return tuple(new)

        fin = lax.fori_loop(0, tc, body, tuple(init), unroll=2)
        for jj, j in enumerate(slabs):
            hre_ref[:, j * 128:(j + 1) * 128] = fin[2 * jj]
            him_ref[:, j * 128:(j + 1) * 128] = fin[2 * jj + 1]

    for kb in range(8):
        half = kb // 4
        sre, sim = [], []
        for b in range(nb):
            rows = block_rows(b, half)
            sre.append(jnp.concatenate([bre_ref[(kb % 4) * 4 + jj, rows, :] for jj in range(4)], axis=-1))
            sim.append(jnp.concatenate([bim_ref[(kb % 4) * 4 + jj, rows, :] for jj in range(4)], axis=-1))
        sre = jnp.concatenate(sre, axis=0).astype(BF16)
        sim = jnp.concatenate(sim, axis=0).astype(BF16)
        ykb = (jnp.dot(sre, wcr_ref[kb], preferred_element_type=F32)
               + jnp.dot(sim, wci_ref[kb], preferred_element_type=F32))
        if half:
            ysh_ref[...] = ykb.reshape(nb, tcs, 128)
            ykb = ysh_ref[:, shift:shift + tc, :].reshape(nb * tc, 128)
        cols = slice(kb * 128, (kb + 1) * 128)
        y_ref[:, cols] = ykb + d_ref[:, cols] * ub[:, cols].astype(F32)

    y = jax.nn.gelu(y_ref[...], approximate=True)
    zg = jnp.dot(y.astype(BF16), wglu_ref[...], preferred_element_type=F32) + bglu_ref[...]
    y = y * jax.nn.sigmoid(zg)
    gate = gate_ref[...].reshape(nb * tc, BRANCH).astype(F32)
    o_ref[...] = (_rms(y, g_ref[...]) * _silu(gate)).astype(BF16).reshape(nb, tc, BRANCH)


def _s5(z3, wb, lamr, lami, wcr, wci, d, wglu, bglu, g, *, tc=128):
    nb, seq, _ = z3.shape
    assert 2 * nb == SUBLANES
    pitch = tc + SUBLANES // 2
    full = lambda a: pl.BlockSpec(a.shape, lambda i, _n=a.ndim: (0,) * _n)
    kern = functools.partial(_s5_kernel, nb=nb, tc=tc, pitch=pitch)
    return pl.pallas_call(
        kern,
        grid=(seq // tc,),
        in_specs=[pl.BlockSpec((nb, tc, BRANCH), lambda i: (0, i, Z_S5_U)),
                  pl.BlockSpec((nb, tc, BRANCH), lambda i: (0, i, Z_S5_GATE)),
                  full(wb), full(lamr), full(lami), full(wcr), full(wci), full(d),
                  full(wglu), full(bglu), full(g)],
        out_specs=pl.BlockSpec((nb, tc, BRANCH), lambda i: (0, i, 0)),
        out_shape=jax.ShapeDtypeStruct((nb, seq, BRANCH), BF16),
        scratch_shapes=[pltpu.VMEM((16, SUBLANES * pitch, LANES), F32),
                        pltpu.VMEM((16, SUBLANES * pitch, LANES), F32),
                        pltpu.VMEM((SUBLANES, 2048), F32),
                        pltpu.VMEM((SUBLANES, 2048), F32),
                        pltpu.VMEM((nb * tc, BRANCH), F32),
                        pltpu.VMEM((nb, tc + SUBLANES, BRANCH), F32),
                        pltpu.VMEM((nb, tc + SUBLANES, LANES), F32)],
        compiler_params=_cparams(1),
        name="s5",
    )(z3, z3, wb, lamr, lami, wcr, wci, d, wglu, bglu, g)


def _s5_params(lam_re, lam_im, b_re, b_im, c_re, c_im, log_dt, nb):
    dt = jnp.exp(log_dt.astype(F32))[:, None]
    lr, li = lam_re.astype(F32), lam_im.astype(F32)
    mag = jnp.exp(lr * dt)
    lbr, lbi = mag * jnp.cos(li * dt), mag * jnp.sin(li * dt)
    den = lr * lr + li * li
    cr = ((lbr - 1.0) * lr + lbi * li) / den
    ci = (lbi * lr - (lbr - 1.0) * li) / den
    bbr = cr[..., None] * b_re - ci[..., None] * b_im
    bbi = cr[..., None] * b_im + ci[..., None] * b_re
    eye = jnp.eye(8, dtype=F32)

    def in_layout(m):
        m = m.reshape(8, 8, S5_STATE, S5_GROUP)
        return jnp.einsum('kgph,gG->kghGp', m, eye).reshape(8, 128, 512)

    def out_layout(m):
        m = m.reshape(8, 8, S5_GROUP, S5_STATE)
        return jnp.einsum('kghp,gG->kgpGh', m, eye).reshape(8, 512, 128)

    wb = jnp.concatenate([in_layout(bbr), in_layout(bbi)], axis=-1).astype(BF16)
    wcr = out_layout(c_re.astype(F32)).astype(BF16)
    wci = out_layout(-c_im.astype(F32)).astype(BF16)

    def seq_layout(v):
        return jnp.tile(v.reshape(1, 2, 2048), (nb, 1, 1)).reshape(2 * nb, 2048)

    return wb, seq_layout(lbr), seq_layout(lbi), wcr, wci


def _lru_kernel(u_ref, gate_ref, cw_ref, cb_ref, wax_ref, ba_ref, bx_ref, sp_ref, g_ref, o_ref,
                xpad_ref, a_ref, h_ref, st_ref, *, nb, tc, pitch):
    @pl.when(pl.program_id(0) == 0)
    def _():
        xpad_ref[:, 0:SUBLANES, :] = jnp.zeros((nb, SUBLANES, BRANCH), F32)
        st_ref[...] = jnp.zeros_like(st_ref)

    xpad_ref[:, SUBLANES:SUBLANES + tc, :] = u_ref[...].astype(F32)
    xc = cb_ref[...].reshape(1, 1, BRANCH)
    for k in range(LRU_CONV):
        off = SUBLANES - (LRU_CONV - 1) + k
        xc = xc + cw_ref[k:k + 1, :].reshape(1, 1, BRANCH) * xpad_ref[:, off:off + tc, :]
    xpad_ref[:, 0:SUBLANES, :] = xpad_ref[:, tc:tc + SUBLANES, :]
    x2 = xc.reshape(nb * tc, BRANCH)
    xb = x2.astype(BF16)

    for hb in range(LRU_BLOCKS):
        cols = slice(hb * 128, (hb + 1) * 128)
        ri = jnp.dot(xb[:, cols], wax_ref[hb], preferred_element_type=F32)
        r = jax.nn.sigmoid(ri[:, :128] + ba_ref[:, cols])
        ig = jax.nn.sigmoid(ri[:, 128:] + bx_ref[:, cols])
        log_a = (-LRU_C) * r * sp_ref[:, cols]
        a = jnp.exp(log_a)
        gated = jnp.sqrt(-jnp.tanh(log_a) * (a * a + 1.0)) * (ig * x2[:, cols])
        half, j = hb // 4, hb % 4
        for b in range(nb):
            rows = pl.ds((2 * b + half) * pitch, tc)
            a_ref[j, rows, :] = a[b * tc:(b + 1) * tc]
            h_ref[j, rows, :] = gated[b * tc:(b + 1) * tc]

    def body(t, carry):
        new = []
        for j in range(4):
            idx = pl.ds(t, SUBLANES, stride=pitch)
            h = a_ref[j, idx, :] * carry[j] + h_ref[j, idx, :]
            h_ref[j, idx, :] = h
            new.append(h)
        return tuple(new)

    init = tuple(st_ref[:, j * 128:(j + 1) * 128] for j in range(4))
    fin = lax.fori_loop(0, tc, body, init, unroll=8)
    for j in range(4):
        st_ref[:, j * 128:(j + 1) * 128] = fin[j]

    for b in range(nb):
        parts = [h_ref[j, pl.ds((2 * b + half) * pitch, tc), :] for half in range(2) for j in range(4)]
        hb_ = jnp.concatenate(parts, axis=-1)
        o_ref[b] = (_rms(hb_, g_ref[...]) * _silu(gate_ref[b].astype(F32))).astype(BF16)


def _lru(z3, cw, cb, wax, ba, bx, sp, g, *, tc=128):
    nb, seq, _ = z3.shape
    assert 2 * nb == SUBLANES
    pitch = tc + SUBLANES
    full = lambda a: pl.BlockSpec(a.shape, lambda i, _n=a.ndim: (0,) * _n)
    kern = functools.partial(_lru_kernel, nb=nb, tc=tc, pitch=pitch)
    return pl.pallas_call(
        kern,
        grid=(seq // tc,),
        in_specs=[pl.BlockSpec((nb, tc, BRANCH), lambda i: (0, i, Z_LRU_U)),
                  pl.BlockSpec((nb, tc, BRANCH), lambda i: (0, i, Z_LRU_GATE)),
                  full(cw), full(cb), full(wax), full(ba), full(bx), full(sp), full(g)],
        out_specs=pl.BlockSpec((nb, tc, BRANCH), lambda i: (0, i, 0)),
        out_shape=jax.ShapeDtypeStruct((nb, seq, BRANCH), BF16),
        scratch_shapes=[pltpu.VMEM((nb, tc + SUBLANES, BRANCH), F32),
                        pltpu.VMEM((4, SUBLANES * pitch, LANES), F32),
                        pltpu.VMEM((4, SUBLANES * pitch, LANES), F32),
                        pltpu.VMEM((SUBLANES, 512), F32)],
        compiler_params=_cparams(1),
        name="rglru",
    )(z3, z3, cw, cb, wax, ba, bx, sp, g)


def _gla_kernel(q_ref, k_ref, v_ref, glr_ref, gate_ref, wg_ref, bg_ref, og_ref, tril_ref, o_ref,
                st_ref, *, tg):
    @pl.when(pl.program_id(1) == 0)
    def _():
        st_ref[...] = jnp.zeros_like(st_ref)

    n_ch = tg // GLA_CHUNK
    x = jnp.dot(glr_ref[...], wg_ref[...], preferred_element_type=F32) + bg_ref[...]
    log_a = (jnp.minimum(x, 0.0) - jnp.log1p(jnp.exp(-jnp.abs(x)))) / GLA_TAU

    tril = tril_ref[...]
    p0 = log_a.astype(BF16)
    r0 = log_a - p0.astype(F32)
    p1 = r0.astype(BF16)
    p2 = (r0 - p1.astype(F32)).astype(BF16)
    cum = (jnp.dot(tril, p0, preferred_element_type=F32)
           + jnp.dot(tril, p1, preferred_element_type=F32)
           + jnp.dot(tril, p2, preferred_element_type=F32))

    row = lax.broadcasted_iota(jnp.int32, (tg, tg), 0)
    col = lax.broadcasted_iota(jnp.int32, (tg, tg), 1)
    causal = (col <= row) & ((row // GLA_CHUNK) == (col // GLA_CHUNK))
    nt = (((1,), (1,)), ((), ()))
    tn = (((0,), (0,)), ((), ()))

    for hh in range(GLA_HEADS):
        kc = slice(hh * GLA_DK, (hh + 1) * GLA_DK)
        vc = slice(hh * GLA_DV, (hh + 1) * GLA_DV)
        cumh = cum[:, kc]
        kh = k_ref[:, kc].astype(F32)
        qd = (q_ref[:, kc].astype(F32) * (GLA_DK ** -0.5)) * jnp.exp(cumh)
        ki = kh * jnp.exp(-cumh)
        qdb = qd.astype(BF16)
        att = lax.dot_general(qdb, ki.astype(BF16), nt, preferred_element_type=F32)
        att = jnp.where(causal, att, 0.0)
        vb = v_ref[:, vc]
        o = jnp.dot(att.astype(BF16), vb, preferred_element_type=F32)
        st = st_ref[hh]
        inter = []
        for c in range(n_ch):
            rows = slice(c * GLA_CHUNK, (c + 1) * GLA_CHUNK)
            tot = cumh[(c + 1) * GLA_CHUNK - 1:(c + 1) * GLA_CHUNK, :]
            kd = kh[rows] * jnp.exp(tot - cumh[rows])
            inter.append(lax.dot_general(qdb[rows], st.astype(BF16), nt, preferred_element_type=F32))
            kv_t = lax.dot_general(vb[rows], kd.astype(BF16), tn, preferred_element_type=F32)
            st = st * jnp.exp(tot) + kv_t
        st_ref[hh] = st
        o = o + jnp.concatenate(inter, axis=0)
        gate = gate_ref[:, vc].astype(F32)
        o_ref[:, vc] = (_rms(o, og_ref[:, vc]) * _silu(gate)).astype(BF16)


def _gla(z3, wg, bg, og, tril, *, tg=256):
    nb, seq, _ = z3.shape
    kern = functools.partial(_gla_kernel, tg=tg)
    full = lambda a: pl.BlockSpec(a.shape, lambda b, i, _n=a.ndim: (0,) * _n)
    return pl.pallas_call(
        kern,
        grid=(nb, seq // tg),
        in_specs=[pl.BlockSpec((None, tg, 512), lambda b, i: (b, i, Z_GLA_Q_512)),
                  pl.BlockSpec((None, tg, 512), lambda b, i: (b, i, Z_GLA_K_512)),
                  pl.BlockSpec((None, tg, BRANCH), lambda b, i: (b, i, Z_GLA_V)),
                  pl.BlockSpec((None, tg, 128), lambda b, i: (b, i, Z_GLR_128)),
                  pl.BlockSpec((None, tg, BRANCH), lambda b, i: (b, i, Z_GLA_GATE)),
                  full(wg), full(bg), full(og), full(tril)],
        out_specs=pl.BlockSpec((None, tg, BRANCH), lambda b, i: (b, i, 0)),
        out_shape=jax.ShapeDtypeStruct((nb, seq, BRANCH), BF16),
        scratch_shapes=[pltpu.VMEM((GLA_HEADS, GLA_DV, GLA_DK), F32)],
        compiler_params=_cparams(2),
        name="gla",
    )(z3, z3, z3, z3, z3, wg, bg, og, tril)


def _mla_proj_kernel(cq_ref, ckv_ref, kr_ref, pos_ref, gq_ref, gkv_ref, wq_ref, wk_ref, wvt_ref,
                     invf_ref, sgn_ref, msk_ref, q_out, k_out, vt_out):
    cqn = _rms(cq_ref[...].astype(F32), gq_ref[...]).astype(BF16)
    qa = jnp.dot(cqn, wq_ref[...], preferred_element_type=F32)
    ang = pos_ref[...].astype(F32) * invf_ref[...]
    cosm = jnp.cos(ang) * msk_ref[...]
    sinm = jnp.sin(ang) * sgn_ref[...]
    nh = MLA_HEADS * 128
    qscale = MLA_SCALE * LOG2_E
    for h in range(MLA_HEADS):
        q_out[:, h * 256:h * 256 + 128] = (qa[:, h * 128:(h + 1) * 128] * qscale).astype(BF16)
        qr = qa[:, nh + h * 128:nh + (h + 1) * 128] * cosm + qa[:, 2 * nh + h * 128:2 * nh + (h + 1) * 128] * sinm
        q_out[:, h * 256 + 128:(h + 1) * 256] = (qr * qscale).astype(BF16)
    c = _rms(ckv_ref[...].astype(F32), gkv_ref[...]).astype(BF16)
    kn = jnp.dot(c, wk_ref[...], preferred_element_type=F32)
    kr = kr_ref[...].astype(F32)
    krp = (kr * cosm + pltpu.roll(kr, 64, 1) * sinm).astype(BF16)
    for h in range(MLA_HEADS):
        k_out[:, h * 256:h * 256 + 128] = kn[:, h * 128:(h + 1) * 128].astype(BF16)
        k_out[:, h * 256 + 128:(h + 1) * 256] = krp
    nt = (((1,), (1,)), ((), ()))
    vt_out[...] = lax.dot_general(wvt_ref[...], c, nt, preferred_element_type=F32).astype(BF16)


def _mla_proj(z2, pos2, gq, gkv, wq, wk, wvt, invf, sgn, msk, *, tm):
    t = z2.shape[0]
    full = lambda a: pl.BlockSpec(a.shape, lambda i, _n=a.ndim: (0,) * _n)
    return pl.pallas_call(
        _mla_proj_kernel,
        grid=(t // tm,),
        in_specs=[pl.BlockSpec((tm, MLA_Q_RANK), lambda i: (i, Z_CQ_768)),
                  pl.BlockSpec((tm, MLA_KV_RANK), lambda i: (i, Z_CKV_512)),
                  pl.BlockSpec((tm, 128), lambda i: (i, Z_KR_128)),
                  pl.BlockSpec((tm, 1), lambda i: (i, 0)),
                  full(gq), full(gkv), full(wq), full(wk), full(wvt), full(invf), full(sgn), full(msk)],
        out_specs=[pl.BlockSpec((tm, 2048), lambda i: (i, 0)),
                   pl.BlockSpec((tm, 2048), lambda i: (i, 0)),
                   pl.BlockSpec((None, BRANCH, tm), lambda i: (i, 0, 0))],
        out_shape=[jax.ShapeDtypeStruct((t, 2048), BF16),
                   jax.ShapeDtypeStruct((t, 2048), BF16),
                   jax.ShapeDtypeStruct((t // tm, BRANCH, tm), BF16)],
        compiler_params=_cparams(1),
        name="mla_proj",
    )(z2, z2, z2, pos2, gq, gkv, wq, wk, wvt, invf, sgn, msk)


def _mla_attn_kernel(q_ref, k_ref, vt_ref, gate_ref, g_ref, o_ref, m_ref, l_ref, acc_ref, o_scr, *, tq):
    qi = pl.program_id(1)
    nt = (((1,), (1,)), ((), ()))
    key = lax.broadcasted_iota(jnp.int32, (tq, tq), 0)
    qry = lax.broadcasted_iota(jnp.int32, (tq, tq), 1)

    m_ref[...] = jnp.full(m_ref.shape, NEG_INF, F32)
    l_ref[...] = jnp.zeros_like(l_ref)
    acc_ref[...] = jnp.zeros_like(acc_ref)

    def step(j, masked):
        r0 = pl.multiple_of(j * tq, tq)

        def scores(h):
            return lax.dot_general(k_ref[pl.ds(r0, tq), h * 256:(h + 1) * 256],
                                   q_ref[:, h * 256:(h + 1) * 256], nt, preferred_element_type=F32)

        st_next = scores(0)
        for h in range(MLA_HEADS):
            st = st_next
            if h + 1 < MLA_HEADS:
                st_next = scores(h + 1)
            if masked:
                st = jnp.where(key <= qry, st, NEG_INF)
            m_old = m_ref[h]
            m_new = jnp.maximum(m_old, jnp.max(st, axis=0, keepdims=True))
            alpha = jnp.exp2(m_old - m_new)
            pt = jnp.exp2(st - m_new)
            l_ref[h] = alpha * l_ref[h] + jnp.sum(pt, axis=0, keepdims=True)
            acc_ref[h] = alpha * acc_ref[h] + jnp.dot(
                vt_ref[j, h * 128:(h + 1) * 128, :], pt.astype(BF16), preferred_element_type=F32)
            m_ref[h] = m_new

    def body(j, c):
        step(j, False)
        return c

    lax.fori_loop(0, qi, body, 0)
    step(qi, True)

    for h in range(MLA_HEADS):
        o_scr[:, h * 128:(h + 1) * 128] = (acc_ref[h] / l_ref[h]).T
    o_ref[...] = (_rms(o_scr[...], g_ref[...]) * _silu(gate_ref[...].astype(F32))).astype(BF16)


def _mla_attn(q3, k3, vt4, z3, g, *, tq):
    nb, seq, _ = q3.shape
    kern = functools.partial(_mla_attn_kernel, tq=tq)
    return pl.pallas_call(
        kern,
        grid=(nb, seq // tq),
        in_specs=[pl.BlockSpec((None, tq, 2048), lambda b, i: (b, i, 0)),
                  pl.BlockSpec((None, seq, 2048), lambda b, i: (b, 0, 0)),
                  pl.BlockSpec((None, seq // tq, BRANCH, tq), lambda b, i: (b, 0, 0, 0)),
                  pl.BlockSpec((None, tq, BRANCH), lambda b, i: (b, i, Z_MLA_GATE)),
                  pl.BlockSpec((1, BRANCH), lambda b, i: (0, 0))],
        out_specs=pl.BlockSpec((None, tq, BRANCH), lambda b, i: (b, i, 0)),
        out_shape=jax.ShapeDtypeStruct((nb, seq, BRANCH), BF16),
        scratch_shapes=[pltpu.VMEM((MLA_HEADS, 1, tq), F32),
                        pltpu.VMEM((MLA_HEADS, 1, tq), F32),
                        pltpu.VMEM((MLA_HEADS, 128, tq), F32),
                        pltpu.VMEM((tq, BRANCH), F32)],
        compiler_params=_cparams(2),
        name="mla_attn",
    )(q3, k3, vt4, z3, g)


def _outproj_kernel(y0_ref, y1_ref, y2_ref, y3_ref, w_ref, x_ref, o_ref):
    acc = x_ref[...]
    for n, y_ref in enumerate((y0_ref, y1_ref, y2_ref, y3_ref)):
        acc = acc + jnp.dot(y_ref[...], w_ref[n * BRANCH:(n + 1) * BRANCH, :], preferred_element_type=F32)
    o_ref[...] = acc


def _outproj(ys, w_all, layer, x2, *, tm=1024, tn=512):
    t, d = x2.shape
    yspec = pl.BlockSpec((tm, BRANCH), lambda i, j: (i, 0))
    return pl.pallas_call(
        _outproj_kernel,
        grid=(t // tm, d // tn),
        in_specs=[yspec, yspec, yspec, yspec,
                  pl.BlockSpec((None, 4 * BRANCH, tn), lambda i, j: (layer, 0, j)),
                  pl.BlockSpec((tm, tn), lambda i, j: (i, j))],
        out_specs=pl.BlockSpec((tm, tn), lambda i, j: (i, j)),
        out_shape=jax.ShapeDtypeStruct((t, d), F32),
        compiler_params=_cparams(2),
        name="outproj",
    )(*ys, w_all, x2)


def _final_norm_kernel(x_ref, g_ref, o_ref):
    o_ref[...] = _rms(x_ref[...], g_ref[...])


def _final_norm(x2, g, *, tm=512):
    t, d = x2.shape
    return pl.pallas_call(
        _final_norm_kernel,
        grid=(t // tm,),
        in_specs=[pl.BlockSpec((tm, d), lambda i: (i, 0)), pl.BlockSpec((1, d), lambda i: (0, 0))],
        out_specs=pl.BlockSpec((tm, d), lambda i: (i, 0)),
        out_shape=jax.ShapeDtypeStruct((t, d), F32),
        compiler_params=_cparams(1),
        name="final_norm",
    )(x2, g)


def _swap_halves(w, axis=-1):
    a, b = jnp.split(w, 2, axis=axis)
    return jnp.concatenate([b, a], axis=axis)


def _mla_q_weight(w_uq):
    w = w_uq.reshape(MLA_Q_RANK, MLA_HEADS, MLA_NOPE + MLA_ROPE)
    nope = w[:, :, :MLA_NOPE].reshape(MLA_Q_RANK, MLA_HEADS * MLA_NOPE)
    rope = w[:, :, MLA_NOPE:]
    pad = ((0, 0), (0, 0), (0, 128 - MLA_ROPE))
    rope_p = jnp.pad(rope, pad).reshape(MLA_Q_RANK, MLA_HEADS * 128)
    rope_s = jnp.pad(_swap_halves(rope), pad).reshape(MLA_Q_RANK, MLA_HEADS * 128)
    return jnp.concatenate([nope, rope_p, rope_s], axis=1).astype(BF16)


def _rope_tables():
    half = MLA_ROPE // 2
    inv = 1.0 / (ROPE_THETA ** (jnp.arange(half, dtype=F32) / half))
    zeros = jnp.zeros((128 - MLA_ROPE,), F32)
    invf = jnp.concatenate([inv, inv, zeros]).reshape(1, 128)
    sgn = jnp.concatenate([-jnp.ones((half,), F32), jnp.ones((half,), F32), zeros]).reshape(1, 128)
    msk = jnp.concatenate([jnp.ones((MLA_ROPE,), F32), zeros]).reshape(1, 128)
    return invf, sgn, msk


def _gla_tril(tg):
    r = np.arange(tg)
    m = (r[None, :] <= r[:, None]) & ((r[:, None] // GLA_CHUNK) == (r[None, :] // GLA_CHUNK))
    return jnp.asarray(m, dtype=BF16)


def kernel(x, positions, norm_g, w_in, s5_lambda_re, s5_lambda_im, s5_b_re, s5_b_im, s5_c_re, s5_c_im,
           s5_d, s5_log_dt, s5_w_glu, s5_b_glu, s5_out_g, mla_q_norm_g, mla_kv_norm_g, mla_w_uq,
           mla_w_uk, mla_w_uv, mla_out_g, gla_w_gate, gla_b_gate, gla_out_g, lru_conv_w, lru_conv_b,
           lru_w_a, lru_b_a, lru_w_x, lru_b_x, lru_lambda, lru_out_g, w_out, final_g):
    nb, seq, d = x.shape
    depth = w_in.shape[0]
    t = nb * seq
    row = lambda v: v.reshape(1, -1).astype(F32)
    x2 = x.reshape(t, d)
    pos2 = positions.reshape(t, 1)
    invf, sgn, msk = _rope_tables()
    gla_tg = 256
    tril = _gla_tril(gla_tg)
    w_out_bf = w_out.astype(BF16)
    w_in_t = _w_in_t(w_in)

    for l in range(depth):
        z2 = _inproj(x2, row(norm_g[l]), w_in_t, l)
        z3 = z2.reshape(nb, seq, Z_COLS)

        wb, lamr, lami, wcr, wci = _s5_params(s5_lambda_re[l], s5_lambda_im[l], s5_b_re[l], s5_b_im[l],
                                              s5_c_re[l], s5_c_im[l], s5_log_dt[l], nb)
        y_s5 = _s5(z3, wb, lamr, lami, wcr, wci, row(s5_d[l]), s5_w_glu[l].astype(BF16),
                   row(s5_b_glu[l]), row(s5_out_g[l]))

        q2, k2, vt3 = _mla_proj(z2, pos2, row(mla_q_norm_g[l]), row(mla_kv_norm_g[l]),
                                _mla_q_weight(mla_w_uq[l]), mla_w_uk[l].astype(BF16),
                                mla_w_uv[l].T.astype(BF16), invf, sgn, msk, tm=MLA_TQ)
        y_mla = _mla_attn(q2.reshape(nb, seq, 2048), k2.reshape(nb, seq, 2048),
                          vt3.reshape(nb, seq // MLA_TQ, BRANCH, MLA_TQ), z3, row(mla_out_g[l]), tq=MLA_TQ)

        wg = jnp.pad(gla_w_gate[l], ((0, 128 - GLA_GATE_RANK), (0, 0))).astype(BF16)
        y_gla = _gla(z3, wg, row(gla_b_gate[l]), row(gla_out_g[l]), tril, tg=gla_tg)

        wax = jnp.concatenate([lru_w_a[l], lru_w_x[l]], axis=-1).astype(BF16)
        sp = row(jax.nn.softplus(-lru_lambda[l].astype(F32)))
        y_lru = _lru(z3, lru_conv_w[l].astype(F32), row(lru_conv_b[l]), wax, row(lru_b_a[l]),
                     row(lru_b_x[l]), sp, row(lru_out_g[l]))

        ys = [y.reshape(t, BRANCH) for y in (y_s5, y_mla, y_gla, y_lru)]
        x2 = _outproj(ys, w_out_bf, l, x2)

    return _final_norm(x2, row(final_g)).reshape(nb, seq, d)
```

```python
import functools

import jax
import jax.numpy as jnp
import numpy as np
from jax import lax
from jax.experimental import pallas as pl
from jax.experimental.pallas import tpu as pltpu

F32 = jnp.float32
BF16 = jnp.bfloat16

D_MODEL = 4096
BRANCH = 1024
NORM_EPS = 1e-6

S5_GROUP = 16
S5_GROUPS = 64
S5_STATE = 64

MLA_NOPE = 128
MLA_ROPE = 64
MLA_HEADS = 8
MLA_Q_RANK = 768
MLA_KV_RANK = 512
ROPE_THETA = 10000.0
MLA_SCALE = (MLA_NOPE + MLA_ROPE) ** -0.5
NEG_INF = -1e30
LOG2_E = 1.4426950408889634
MLA_TQ = 512
GLA_HEADS = 4
GLA_DV = 256
GLA_DK = 128
GLA_GATE_RANK = 16
GLA_TAU = 16.0
GLA_CHUNK = 64

LRU_BLOCKS = 8
LRU_BLOCK_W = 128
LRU_CONV = 4
LRU_C = 8.0

LANES = 128
SUBLANES = 8

Z_S5_U, Z_S5_GATE, Z_MLA_GATE, Z_GLA_V, Z_GLA_GATE, Z_LRU_U, Z_LRU_GATE = range(7)
Z_CKV_512 = 14
Z_CQ_768 = 10
Z_KR_128 = 66
Z_GLR_128 = 67
Z_GLA_Q_512 = 17
Z_GLA_K_512 = 18
Z_COLS = 9728

VMEM_LIMIT = 56 * 1024 * 1024
NORM_ROWS = 256
S5_SCAN_W = 8


def _cparams(n_axes):
    return pltpu.CompilerParams(dimension_semantics=("arbitrary",) * n_axes,
                                vmem_limit_bytes=VMEM_LIMIT)


def _rms(x, g):
    return x * lax.rsqrt(jnp.mean(x * x, axis=-1, keepdims=True) + NORM_EPS) * g


def _silu(x):
    return x * jax.nn.sigmoid(x)


def _inproj_kernel(rows_ref, x_ref, g_ref, w_ref, o_ref, h_ref):
    del rows_ref
    nt = (((1,), (1,)), ((), ()))
    first = pl.program_id(1) == 0

    @pl.when(first)
    def _():
        for r in range(x_ref.shape[0] // NORM_ROWS):
            rows = slice(r * NORM_ROWS, (r + 1) * NORM_ROWS)
            h = _rms(x_ref[rows, :], g_ref[...]).astype(BF16)
            h_ref[rows, :] = h
            o_ref[rows, :] = lax.dot_general(h, w_ref[0], nt, preferred_element_type=F32).astype(o_ref.dtype)

    @pl.when(jnp.logical_not(first))
    def _():
        o_ref[...] = lax.dot_general(h_ref[...], w_ref[0], nt,
                                     preferred_element_type=F32).astype(o_ref.dtype)


W_IN_TN = 512
W_IN_ROWS = 9552
_W_IN_BLOCK_ROWS = (0, 512, 1024, 1536,
                    3392, 3904,
                    5440, 5952, 6480, 6992,
                    7504, 8016, 8528, 9040,
                    2816, 2048,
                    W_IN_ROWS,
                    4416, 4928)


def _inproj(x2, g, w_t, layer, *, tm=1024):
    t, d = x2.shape
    tn = W_IN_TN
    nblk = len(_W_IN_BLOCK_ROWS)
    rows = jnp.asarray(_W_IN_BLOCK_ROWS, jnp.int32)
    grid_spec = pltpu.PrefetchScalarGridSpec(
        num_scalar_prefetch=1,
        grid=(t // tm, nblk),
        in_specs=[pl.BlockSpec((tm, d), lambda i, j, r: (i, 0)),
                  pl.BlockSpec((1, d), lambda i, j, r: (0, 0)),
                  pl.BlockSpec((pl.Element(1), pl.Element(tn), pl.Element(d)),
                               lambda i, j, r: (layer, pl.multiple_of(r[j], 16), 0))],
        out_specs=pl.BlockSpec((tm, tn), lambda i, j, r: (i, j)),
        scratch_shapes=[pltpu.VMEM((tm, d), BF16)])
    return pl.pallas_call(
        _inproj_kernel,
        grid_spec=grid_spec,
        out_shape=jax.ShapeDtypeStruct((t, nblk * tn), BF16),
        compiler_params=_cparams(2),
        name="inproj",
    )(rows, x2, g, w_t)


def _w_in_t(w_in):
    wt = jnp.swapaxes(w_in, 1, 2)
    depth, _, d = wt.shape
    extra = [wt[:, 2560:2816], wt[:, 3328:3392], wt[:, 3360:3392], wt[:, 3328:3360],
             wt[:, 6464:6480], jnp.zeros((depth, 128 - GLA_GATE_RANK, d), wt.dtype)]
    return jnp.concatenate([wt] + extra, axis=1).astype(BF16)


def _s5_kernel(u_ref, gate_ref, wb_ref, lamr_ref, lami_ref, wcr_ref, wci_ref, d_ref,
               wglu_ref, bglu_ref, g_ref, o_ref,
               bre_ref, bim_ref, hre_ref, him_ref, y_ref, ush_ref, ysh_ref, *, nb, tc, pitch):
    shift = pitch - tc
    tcs = tc + 2 * shift

    @pl.when(pl.program_id(0) == 0)
    def _():
        hre_ref[...] = jnp.zeros_like(hre_ref)
        him_ref[...] = jnp.zeros_like(him_ref)
        ush_ref[...] = jnp.zeros_like(ush_ref)

    ub = u_ref[...].reshape(nb * tc, BRANCH)
    ush_ref[:, shift:shift + tc, :] = u_ref[...].astype(F32)
    ubs = ush_ref[...].reshape(nb * tcs, BRANCH).astype(BF16)

    def block_rows(b, half):
        if half:
            return pl.ds((2 * b + 1) * pitch - shift, tcs)
        return pl.ds(2 * b * pitch, tc)

    for kb in range(8):
        half = kb // 4
        lhs, m = (ubs, tcs) if half else (ub, tc)
        res = jnp.dot(lhs[:, kb * 128:(kb + 1) * 128], wb_ref[kb], preferred_element_type=F32)
        for b in range(nb):
            rows = block_rows(b, half)
            for jj in range(4):
                j = (kb % 4) * 4 + jj
                bre_ref[j, rows, :] = res[b * m:(b + 1) * m, jj * 128:(jj + 1) * 128]
                bim_ref[j, rows, :] = res[b * m:(b + 1) * m, 512 + jj * 128:512 + (jj + 1) * 128]

    for cb in range(16 // S5_SCAN_W):
        slabs = [cb * S5_SCAN_W + jj for jj in range(S5_SCAN_W)]
        lr = [lamr_ref[:, j * 128:(j + 1) * 128] for j in slabs]
        li = [lami_ref[:, j * 128:(j + 1) * 128] for j in slabs]
        init = []
        for j in slabs:
            init.append(hre_ref[:, j * 128:(j + 1) * 128])
            init.append(him_ref[:, j * 128:(j + 1) * 128])

        def body(t, carry, slabs=slabs, lr=lr, li=li):
            new = []
            idx = pl.ds(t, SUBLANES, stride=pitch)
            for jj, j in enumerate(slabs):
                hr, hi = carry[2 * jj], carry[2 * jj + 1]
                nr = lr[jj] * hr - li[jj] * hi + bre_ref[j, idx, :]
                ni = lr[jj] * hi + li[jj] * hr + bim_ref[j, idx, :]
                bre_ref[j, idx, :] = nr
                bim_ref[j, idx, :] = ni
                new += [nr, ni]
            return tuple(new)

        fin = lax.fori_loop(0, tc, body, tuple(init), unroll=2)
        for jj, j in enumerate(slabs):
            hre_ref[:, j * 128:(j + 1) * 128] = fin[2 * jj]
            him_ref[:, j * 128:(j + 1) * 128] = fin[2 * jj + 1]

    for kb in range(8):
        half = kb // 4
        sre, sim = [], []
        for b in range(nb):
            rows = block_rows(b, half)
            sre.append(jnp.concatenate([bre_ref[(kb % 4) * 4 + jj, rows, :] for jj in range(4)], axis=-1))
            sim.append(jnp.concatenate([bim_ref[(kb % 4) * 4 + jj, rows, :] for jj in range(4)], axis=-1))
        sre = jnp.concatenate(sre, axis=0).astype(BF16)
        sim = jnp.concatenate(sim, axis=0).astype(BF16)
        ykb = (jnp.dot(sre, wcr_ref[kb], preferred_element_type=F32)
               + jnp.dot(sim, wci_ref[kb], preferred_element_type=F32))
        if half:
            ysh_ref[...] = ykb.reshape(nb, tcs, 128)
            ykb = ysh_ref[:, shift:shift + tc, :].reshape(nb * tc, 128)
        cols = slice(kb * 128, (kb + 1) * 128)
        y_ref[:, cols] = ykb + d_ref[:, cols] * ub[:, cols].astype(F32)

    y = jax.nn.gelu(y_ref[...], approximate=True)
    zg = jnp.dot(y.astype(BF16), wglu_ref[...], preferred_element_type=F32) + bglu_ref[...]
    y = y * jax.nn.sigmoid(zg)
    gate = gate_ref[...].reshape(nb * tc, BRANCH).astype(F32)
    o_ref[...] = (_rms(y, g_ref[...]) * _silu(gate)).astype(BF16).reshape(nb, tc, BRANCH)


def _s5(z3, wb, lamr, lami, wcr, wci, d, wglu, bglu, g, *, tc=128):
    nb, seq, _ = z3.shape
    assert 2 * nb == SUBLANES
    pitch = tc + SUBLANES // 2
    full = lambda a: pl.BlockSpec(a.shape, lambda i, _n=a.ndim: (0,) * _n)
    kern = functools.partial(_s5_kernel, nb=nb, tc=tc, pitch=pitch)
    return pl.pallas_call(
        kern,
        grid=(seq // tc,),
        in_specs=[pl.BlockSpec((nb, tc, BRANCH), lambda i: (0, i, Z_S5_U)),
                  pl.BlockSpec((nb, tc, BRANCH), lambda i: (0, i, Z_S5_GATE)),
                  full(wb), full(lamr), full(lami), full(wcr), full(wci), full(d),
                  full(wglu), full(bglu), full(g)],
        out_specs=pl.BlockSpec((nb, tc, BRANCH), lambda i: (0, i, 0)),
        out_shape=jax.ShapeDtypeStruct((nb, seq, BRANCH), BF16),
        scratch_shapes=[pltpu.VMEM((16, SUBLANES * pitch, LANES), F32),
                        pltpu.VMEM((16, SUBLANES * pitch, LANES), F32),
                        pltpu.VMEM((SUBLANES, 2048), F32),
                        pltpu.VMEM((SUBLANES, 2048), F32),
                        pltpu.VMEM((nb * tc, BRANCH), F32),
                        pltpu.VMEM((nb, tc + SUBLANES, BRANCH), F32),
                        pltpu.VMEM((nb, tc + SUBLANES, LANES), F32)],
        compiler_params=_cparams(1),
        name="s5",
    )(z3, z3, wb, lamr, lami, wcr, wci, d, wglu, bglu, g)


def _s5_params(lam_re, lam_im, b_re, b_im, c_re, c_im, log_dt, nb):
    dt = jnp.exp(log_dt.astype(F32))[:, None]
    lr, li = lam_re.astype(F32), lam_im.astype(F32)
    mag = jnp.exp(lr * dt)
    lbr, lbi = mag * jnp.cos(li * dt), mag * jnp.sin(li * dt)
    den = lr * lr + li * li
    cr = ((lbr - 1.0) * lr + lbi * li) / den
    ci = (lbi * lr - (lbr - 1.0) * li) / den
    bbr = cr[..., None] * b_re - ci[..., None] * b_im
    bbi = cr[..., None] * b_im + ci[..., None] * b_re
    eye = jnp.eye(8, dtype=F32)

    def in_layout(m):
        m = m.reshape(8, 8, S5_STATE, S5_GROUP)
        return jnp.einsum('kgph,gG->kghGp', m, eye).reshape(8, 128, 512)

    def out_layout(m):
        m = m.reshape(8, 8, S5_GROUP, S5_STATE)
        return jnp.einsum('kghp,gG->kgpGh', m, eye).reshape(8, 512, 128)

    wb = jnp.concatenate([in_layout(bbr), in_layout(bbi)], axis=-1).astype(BF16)
    wcr = out_layout(c_re.astype(F32)).astype(BF16)
    wci = out_layout(-c_im.astype(F32)).astype(BF16)

    def seq_layout(v):
        return jnp.tile(v.reshape(1, 2, 2048), (nb, 1, 1)).reshape(2 * nb, 2048)

    return wb, seq_layout(lbr), seq_layout(lbi), wcr, wci


def _lru_kernel(u_ref, gate_ref, cw_ref, cb_ref, wax_ref, ba_ref, bx_ref, sp_ref, g_ref, o_ref,
                xpad_ref, a_ref, h_ref, st_ref, *, nb, tc, pitch):
    @pl.when(pl.program_id(0) == 0)
    def _():
        xpad_ref[:, 0:SUBLANES, :] = jnp.zeros((nb, SUBLANES, BRANCH), F32)
        st_ref[...] = jnp.zeros_like(st_ref)

    xpad_ref[:, SUBLANES:SUBLANES + tc, :] = u_ref[...].astype(F32)
    xc = cb_ref[...].reshape(1, 1, BRANCH)
    for k in range(LRU_CONV):
        off = SUBLANES - (LRU_CONV - 1) + k
        xc = xc + cw_ref[k:k + 1, :].reshape(1, 1, BRANCH) * xpad_ref[:, off:off + tc, :]
    xpad_ref[:, 0:SUBLANES, :] = xpad_ref[:, tc:tc + SUBLANES, :]
    x2 = xc.reshape(nb * tc, BRANCH)
    xb = x2.astype(BF16)

    for hb in range(LRU_BLOCKS):
        cols = slice(hb * 128, (hb + 1) * 128)
        ri = jnp.dot(xb[:, cols], wax_ref[hb], preferred_element_type=F32)
        r = jax.nn.sigmoid(ri[:, :128] + ba_ref[:, cols])
        ig = jax.nn.sigmoid(ri[:, 128:] + bx_ref[:, cols])
        log_a = (-LRU_C) * r * sp_ref[:, cols]
        a = jnp.exp(log_a)
        gated = jnp.sqrt(-jnp.tanh(log_a) * (a * a + 1.0)) * (ig * x2[:, cols])
        half, j = hb // 4, hb % 4
        for b in range(nb):
            rows = pl.ds((2 * b + half) * pitch, tc)
            a_ref[j, rows, :] = a[b * tc:(b + 1) * tc]
            h_ref[j, rows, :] = gated[b * tc:(b + 1) * tc]

    def body(t, carry):
        new = []
        for j in range(4):
            idx = pl.ds(t, SUBLANES, stride=pitch)
            h = a_ref[j, idx, :] * carry[j] + h_ref[j, idx, :]
            h_ref[j, idx, :] = h
            new.append(h)
        return tuple(new)

    init = tuple(st_ref[:, j * 128:(j + 1) * 128] for j in range(4))
    fin = lax.fori_loop(0, tc, body, init, unroll=8)
    for j in range(4):
        st_ref[:, j * 128:(j + 1) * 128] = fin[j]

    for b in range(nb):
        parts = [h_ref[j, pl.ds((2 * b + half) * pitch, tc), :] for half in range(2) for j in range(4)]
        hb_ = jnp.concatenate(parts, axis=-1)
        o_ref[b] = (_rms(hb_, g_ref[...]) * _silu(gate_ref[b].astype(F32))).astype(BF16)


def _lru(z3, cw, cb, wax, ba, bx, sp, g, *, tc=128):
    nb, seq, _ = z3.shape
    assert 2 * nb == SUBLANES
    pitch = tc + SUBLANES
    full = lambda a: pl.BlockSpec(a.shape, lambda i, _n=a.ndim: (0,) * _n)
    kern = functools.partial(_lru_kernel, nb=nb, tc=tc, pitch=pitch)
    return pl.pallas_call(
        kern,
        grid=(seq // tc,),
        in_specs=[pl.BlockSpec((nb, tc, BRANCH), lambda i: (0, i, Z_LRU_U)),
                  pl.BlockSpec((nb, tc, BRANCH), lambda i: (0, i, Z_LRU_GATE)),
                  full(cw), full(cb), full(wax), full(ba), full(bx), full(sp), full(g)],
        out_specs=pl.BlockSpec((nb, tc, BRANCH), lambda i: (0, i, 0)),
        out_shape=jax.ShapeDtypeStruct((nb, seq, BRANCH), BF16),
        scratch_shapes=[pltpu.VMEM((nb, tc + SUBLANES, BRANCH), F32),
                        pltpu.VMEM((4, SUBLANES * pitch, LANES), F32),
                        pltpu.VMEM((4, SUBLANES * pitch, LANES), F32),
                        pltpu.VMEM((SUBLANES, 512), F32)],
        compiler_params=_cparams(1),
        name="rglru",
    )(z3, z3, cw, cb, wax, ba, bx, sp, g)


def _gla_kernel(q_ref, k_ref, v_ref, glr_ref, gate_ref, wg_ref, bg_ref, og_ref, tril_ref, o_ref,
                st_ref, *, tg):
    @pl.when(pl.program_id(1) == 0)
    def _():
        st_ref[...] = jnp.zeros_like(st_ref)

    n_ch = tg // GLA_CHUNK
    x = jnp.dot(glr_ref[...], wg_ref[...], preferred_element_type=F32) + bg_ref[...]
    log_a = (jnp.minimum(x, 0.0) - jnp.log1p(jnp.exp(-jnp.abs(x)))) / GLA_TAU

    tril = tril_ref[...]
    p0 = log_a.astype(BF16)
    r0 = log_a - p0.astype(F32)
    p1 = r0.astype(BF16)
    p2 = (r0 - p1.astype(F32)).astype(BF16)
    cum = (jnp.dot(tril, p0, preferred_element_type=F32)
           + jnp.dot(tril, p1, preferred_element_type=F32)
           + jnp.dot(tril, p2, preferred_element_type=F32))

    row = lax.broadcasted_iota(jnp.int32, (tg, tg), 0)
    col = lax.broadcasted_iota(jnp.int32, (tg, tg), 1)
    causal = (col <= row) & ((row // GLA_CHUNK) == (col // GLA_CHUNK))
    nt = (((1,), (1,)), ((), ()))
    tn = (((0,), (0,)), ((), ()))

    for hh in range(GLA_HEADS):
        kc = slice(hh * GLA_DK, (hh + 1) * GLA_DK)
        vc = slice(hh * GLA_DV, (hh + 1) * GLA_DV)
        cumh = cum[:, kc]
        kh = k_ref[:, kc].astype(F32)
        qd = (q_ref[:, kc].astype(F32) * (GLA_DK ** -0.5)) * jnp.exp(cumh)
        ki = kh * jnp.exp(-cumh)
        qdb = qd.astype(BF16)
        att = lax.dot_general(qdb, ki.astype(BF16), nt, preferred_element_type=F32)
        att = jnp.where(causal, att, 0.0)
        vb = v_ref[:, vc]
        o = jnp.dot(att.astype(BF16), vb, preferred_element_type=F32)
        st = st_ref[hh]
        inter = []
        for c in range(n_ch):
            rows = slice(c * GLA_CHUNK, (c + 1) * GLA_CHUNK)
            tot = cumh[(c + 1) * GLA_CHUNK - 1:(c + 1) * GLA_CHUNK, :]
            kd = kh[rows] * jnp.exp(tot - cumh[rows])
            inter.append(lax.dot_general(qdb[rows], st.astype(BF16), nt, preferred_element_type=F32))
            kv_t = lax.dot_general(vb[rows], kd.astype(BF16), tn, preferred_element_type=F32)
            st = st * jnp.exp(tot) + kv_t
        st_ref[hh] = st
        o = o + jnp.concatenate(inter, axis=0)
        gate = gate_ref[:, vc].astype(F32)
        o_ref[:, vc] = (_rms(o, og_ref[:, vc]) * _silu(gate)).astype(BF16)


def _gla(z3, wg, bg, og, tril, *, tg=256):
    nb, seq, _ = z3.shape
    kern = functools.partial(_gla_kernel, tg=tg)
    full = lambda a: pl.BlockSpec(a.shape, lambda b, i, _n=a.ndim: (0,) * _n)
    return pl.pallas_call(
        kern,
        grid=(nb, seq // tg),
        in_specs=[pl.BlockSpec((None, tg, 512), lambda b, i: (b, i, Z_GLA_Q_512)),
                  pl.BlockSpec((None, tg, 512), lambda b, i: (b, i, Z_GLA_K_512)),
                  pl.BlockSpec((None, tg, BRANCH), lambda b, i: (b, i, Z_GLA_V)),
                  pl.BlockSpec((None, tg, 128), lambda b, i: (b, i, Z_GLR_128)),
                  pl.BlockSpec((None, tg, BRANCH), lambda b, i: (b, i, Z_GLA_GATE)),
                  full(wg), full(bg), full(og), full(tril)],
        out_specs=pl.BlockSpec((None, tg, BRANCH), lambda b, i: (b, i, 0)),
        out_shape=jax.ShapeDtypeStruct((nb, seq, BRANCH), BF16),
        scratch_shapes=[pltpu.VMEM((GLA_HEADS, GLA_DV, GLA_DK), F32)],
        compiler_params=_cparams(2),
        name="gla",
    )(z3, z3, z3, z3, z3, wg, bg, og, tril)


def _mla_proj_kernel(cq_ref, ckv_ref, kr_ref, pos_ref, gq_ref, gkv_ref, wq_ref, wk_ref, wvt_ref,
                     invf_ref, sgn_ref, msk_ref, q_out, k_out, vt_out):
    cqn = _rms(cq_ref[...].astype(F32), gq_ref[...]).astype(BF16)
    qa = jnp.dot(cqn, wq_ref[...], preferred_element_type=F32)
    ang = pos_ref[...].astype(F32) * invf_ref[...]
    cosm = jnp.cos(ang) * msk_ref[...]
    sinm = jnp.sin(ang) * sgn_ref[...]
    nh = MLA_HEADS * 128
    qscale = MLA_SCALE * LOG2_E
    for h in range(MLA_HEADS):
        q_out[:, h * 256:h * 256 + 128] = (qa[:, h * 128:(h + 1) * 128] * qscale).astype(BF16)
        qr = qa[:, nh + h * 128:nh + (h + 1) * 128] * cosm + qa[:, 2 * nh + h * 128:2 * nh + (h + 1) * 128] * sinm
        q_out[:, h * 256 + 128:(h + 1) * 256] = (qr * qscale).astype(BF16)
    c = _rms(ckv_ref[...].astype(F32), gkv_ref[...]).astype(BF16)
    kn = jnp.dot(c, wk_ref[...], preferred_element_type=F32)
    kr = kr_ref[...].astype(F32)
    krp = (kr * cosm + pltpu.roll(kr, 64, 1) * sinm).astype(BF16)
    for h in range(MLA_HEADS):
        k_out[:, h * 256:h * 256 + 128] = kn[:, h * 128:(h + 1) * 128].astype(BF16)
        k_out[:, h * 256 + 128:(h + 1) * 256] = krp
    nt = (((1,), (1,)), ((), ()))
    vt_out[...] = lax.dot_general(wvt_ref[...], c, nt, preferred_element_type=F32).astype(BF16)


def _mla_proj(z2, pos2, gq, gkv, wq, wk, wvt, invf, sgn, msk, *, tm):
    t = z2.shape[0]
    full = lambda a: pl.BlockSpec(a.shape, lambda i, _n=a.ndim: (0,) * _n)
    return pl.pallas_call(
        _mla_proj_kernel,
        grid=(t // tm,),
        in_specs=[pl.BlockSpec((tm, MLA_Q_RANK), lambda i: (i, Z_CQ_768)),
                  pl.BlockSpec((tm, MLA_KV_RANK), lambda i: (i, Z_CKV_512)),
                  pl.BlockSpec((tm, 128), lambda i: (i, Z_KR_128)),
                  pl.BlockSpec((tm, 1), lambda i: (i, 0)),
                  full(gq), full(gkv), full(wq), full(wk), full(wvt), full(invf), full(sgn), full(msk)],
        out_specs=[pl.BlockSpec((tm, 2048), lambda i: (i, 0)),
                   pl.BlockSpec((tm, 2048), lambda i: (i, 0)),
                   pl.BlockSpec((None, BRANCH, tm), lambda i: (i, 0, 0))],
        out_shape=[jax.ShapeDtypeStruct((t, 2048), BF16),
                   jax.ShapeDtypeStruct((t, 2048), BF16),
                   jax.ShapeDtypeStruct((t // tm, BRANCH, tm), BF16)],
        compiler_params=_cparams(1),
        name="mla_proj",
    )(z2, z2, z2, pos2, gq, gkv, wq, wk, wvt, invf, sgn, msk)


def _mla_attn_kernel(q_ref, k_ref, vt_ref, gate_ref, g_ref, o_ref, m_ref, l_ref, acc_ref, o_scr, *, tq):
    qi = pl.program_id(1)
    nt = (((1,), (1,)), ((), ()))
    key = lax.broadcasted_iota(jnp.int32, (tq, tq), 0)
    qry = lax.broadcasted_iota(jnp.int32, (tq, tq), 1)

    m_ref[...] = jnp.full(m_ref.shape, NEG_INF, F32)
    l_ref[...] = jnp.zeros_like(l_ref)
    acc_ref[...] = jnp.zeros_like(acc_ref)

    def step(j, masked):
        r0 = pl.multiple_of(j * tq, tq)

        def scores(h):
            st = lax.dot_general(k_ref[pl.ds(r0, tq), h * 256:(h + 1) * 256],
                                 q_ref[:, h * 256:(h + 1) * 256], nt, preferred_element_type=F32)
            if masked:
                st = jnp.where(key <= qry, st, NEG_INF)
            m_old = m_ref[h]
            return st, m_old, jnp.maximum(m_old, jnp.max(st, axis=0, keepdims=True))

        nxt = scores(0)
        for h in range(MLA_HEADS):
            st, m_old, m_new = nxt
            if h + 1 < MLA_HEADS:
                nxt = scores(h + 1)
            alpha = jnp.exp2(m_old - m_new)
            pt = jnp.exp2(st - m_new)
            l_ref[h] = alpha * l_ref[h] + jnp.sum(pt, axis=0, keepdims=True)
            acc_ref[h] = alpha * acc_ref[h] + jnp.dot(
                vt_ref[j, h * 128:(h + 1) * 128, :], pt.astype(BF16), preferred_element_type=F32)
            m_ref[h] = m_new

    def body(j, c):
        step(j, False)
        return c

    lax.fori_loop(0, qi, body, 0)
    step(qi, True)

    for h in range(MLA_HEADS):
        o_scr[:, h * 128:(h + 1) * 128] = (acc_ref[h] / l_ref[h]).T
    o_ref[...] = (_rms(o_scr[...], g_ref[...]) * _silu(gate_ref[...].astype(F32))).astype(BF16)


def _mla_attn(q3, k3, vt4, z3, g, *, tq):
    nb, seq, _ = q3.shape
    kern = functools.partial(_mla_attn_kernel, tq=tq)
    return pl.pallas_call(
        kern,
        grid=(nb, seq // tq),
        in_specs=[pl.BlockSpec((None, tq, 2048), lambda b, i: (b, i, 0)),
                  pl.BlockSpec((None, seq, 2048), lambda b, i: (b, 0, 0)),
                  pl.BlockSpec((None, seq // tq, BRANCH, tq), lambda b, i: (b, 0, 0, 0)),
                  pl.BlockSpec((None, tq, BRANCH), lambda b, i: (b, i, Z_MLA_GATE)),
                  pl.BlockSpec((1, BRANCH), lambda b, i: (0, 0))],
        out_specs=pl.BlockSpec((None, tq, BRANCH), lambda b, i: (b, i, 0)),
        out_shape=jax.ShapeDtypeStruct((nb, seq, BRANCH), BF16),
        scratch_shapes=[pltpu.VMEM((MLA_HEADS, 1, tq), F32),
                        pltpu.VMEM((MLA_HEADS, 1, tq), F32),
                        pltpu.VMEM((MLA_HEADS, 128, tq), F32),
                        pltpu.VMEM((tq, BRANCH), F32)],
        compiler_params=_cparams(2),
        name="mla_attn",
    )(q3, k3, vt4, z3, g)


def _outproj_kernel(y0_ref, y1_ref, y2_ref, y3_ref, w_ref, x_ref, o_ref):
    acc = x_ref[...]
    for n, y_ref in enumerate((y0_ref, y1_ref, y2_ref, y3_ref)):
        acc = acc + jnp.dot(y_ref[...], w_ref[n * BRANCH:(n + 1) * BRANCH, :].astype(BF16),
                            preferred_element_type=F32)
    o_ref[...] = acc


def _outproj(ys, w_all, layer, x2, *, tm=1024, tn=512):
    t, d = x2.shape
    yspec = pl.BlockSpec((tm, BRANCH), lambda i, j: (i, 0))
    return pl.pallas_call(
        _outproj_kernel,
        grid=(t // tm, d // tn),
        in_specs=[yspec, yspec, yspec, yspec,
                  pl.BlockSpec((None, 4 * BRANCH, tn), lambda i, j: (layer, 0, j)),
                  pl.BlockSpec((tm, tn), lambda i, j: (i, j))],
        out_specs=pl.BlockSpec((tm, tn), lambda i, j: (i, j)),
        out_shape=jax.ShapeDtypeStruct((t, d), F32),
        compiler_params=_cparams(2),
        name="outproj",
    )(*ys, w_all, x2)


def _final_norm_kernel(x_ref, g_ref, o_ref):
    o_ref[...] = _rms(x_ref[...], g_ref[...])


def _final_norm(x2, g, *, tm=512):
    t, d = x2.shape
    return pl.pallas_call(
        _final_norm_kernel,
        grid=(t // tm,),
        in_specs=[pl.BlockSpec((tm, d), lambda i: (i, 0)), pl.BlockSpec((1, d), lambda i: (0, 0))],
        out_specs=pl.BlockSpec((tm, d), lambda i: (i, 0)),
        out_shape=jax.ShapeDtypeStruct((t, d), F32),
        compiler_params=_cparams(1),
        name="final_norm",
    )(x2, g)


def _swap_halves(w, axis=-1):
    a, b = jnp.split(w, 2, axis=axis)
    return jnp.concatenate([b, a], axis=axis)


def _mla_q_weight(w_uq):
    w = w_uq.reshape(MLA_Q_RANK, MLA_HEADS, MLA_NOPE + MLA_ROPE)
    nope = w[:, :, :MLA_NOPE].reshape(MLA_Q_RANK, MLA_HEADS * MLA_NOPE)
    rope = w[:, :, MLA_NOPE:]
    pad = ((0, 0), (0, 0), (0, 128 - MLA_ROPE))
    rope_p = jnp.pad(rope, pad).reshape(MLA_Q_RANK, MLA_HEADS * 128)
    rope_s = jnp.pad(_swap_halves(rope), pad).reshape(MLA_Q_RANK, MLA_HEADS * 128)
    return jnp.concatenate([nope, rope_p, rope_s], axis=1).astype(BF16)


def _rope_tables():
    half = MLA_ROPE // 2
    inv = 1.0 / (ROPE_THETA ** (jnp.arange(half, dtype=F32) / half))
    zeros = jnp.zeros((128 - MLA_ROPE,), F32)
    invf = jnp.concatenate([inv, inv, zeros]).reshape(1, 128)
    sgn = jnp.concatenate([-jnp.ones((half,), F32), jnp.ones((half,), F32), zeros]).reshape(1, 128)
    msk = jnp.concatenate([jnp.ones((MLA_ROPE,), F32), zeros]).reshape(1, 128)
    return invf, sgn, msk


def _gla_tril(tg):
    r = np.arange(tg)
    m = (r[None, :] <= r[:, None]) & ((r[:, None] // GLA_CHUNK) == (r[None, :] // GLA_CHUNK))
    return jnp.asarray(m, dtype=BF16)


def kernel(x, positions, norm_g, w_in, s5_lambda_re, s5_lambda_im, s5_b_re, s5_b_im, s5_c_re, s5_c_im,
           s5_d, s5_log_dt, s5_w_glu, s5_b_glu, s5_out_g, mla_q_norm_g, mla_kv_norm_g, mla_w_uq,
           mla_w_uk, mla_w_uv, mla_out_g, gla_w_gate, gla_b_gate, gla_out_g, lru_conv_w, lru_conv_b,
           lru_w_a, lru_b_a, lru_w_x, lru_b_x, lru_lambda, lru_out_g, w_out, final_g):
    nb, seq, d = x.shape
    depth = w_in.shape[0]
    t = nb * seq
    row = lambda v: v.reshape(1, -1).astype(F32)
    x2 = x.reshape(t, d)
    pos2 = positions.reshape(t, 1)
    invf, sgn, msk = _rope_tables()
    gla_tg = 256
    tril = _gla_tril(gla_tg)
    w_in_t = _w_in_t(w_in)

    for l in range(depth):
        z2 = _inproj(x2, row(norm_g[l]), w_in_t, l)
        z3 = z2.reshape(nb, seq, Z_COLS)

        wb, lamr, lami, wcr, wci = _s5_params(s5_lambda_re[l], s5_lambda_im[l], s5_b_re[l], s5_b_im[l],
                                              s5_c_re[l], s5_c_im[l], s5_log_dt[l], nb)
        y_s5 = _s5(z3, wb, lamr, lami, wcr, wci, row(s5_d[l]), s5_w_glu[l].astype(BF16),
                   row(s5_b_glu[l]), row(s5_out_g[l]))

        q2, k2, vt3 = _mla_proj(z2, pos2, row(mla_q_norm_g[l]), row(mla_kv_norm_g[l]),
                                _mla_q_weight(mla_w_uq[l]), mla_w_uk[l].astype(BF16),
                                mla_w_uv[l].T.astype(BF16), invf, sgn, msk, tm=MLA_TQ)
        y_mla = _mla_attn(q2.reshape(nb, seq, 2048), k2.reshape(nb, seq, 2048),
                          vt3.reshape(nb, seq // MLA_TQ, BRANCH, MLA_TQ), z3, row(mla_out_g[l]), tq=MLA_TQ)

        wg = jnp.pad(gla_w_gate[l], ((0, 128 - GLA_GATE_RANK), (0, 0))).astype(BF16)
        y_gla = _gla(z3, wg, row(gla_b_gate[l]), row(gla_out_g[l]), tril, tg=gla_tg)

        wax = jnp.concatenate([lru_w_a[l], lru_w_x[l]], axis=-1).astype(BF16)
        sp = row(jax.nn.softplus(-lru_lambda[l].astype(F32)))
        y_lru = _lru(z3, lru_conv_w[l].astype(F32), row(lru_conv_b[l]), wax, row(lru_b_a[l]),
                     row(lru_b_x[l]), sp, row(lru_out_g[l]))

        ys = [y.reshape(t, BRANCH) for y in (y_s5, y_mla, y_gla, y_lru)]
        x2 = _outproj(ys, w_out, l, x2)

    return _final_norm(x2, row(final_g)).reshape(nb, seq, d)
```

```python
import functools

import jax
import jax.numpy as jnp
import numpy as np
from jax import lax
from jax.experimental import pallas as pl
from jax.experimental.pallas import tpu as pltpu

F32 = jnp.float32
BF16 = jnp.bfloat16

D_MODEL = 4096
BRANCH = 1024
NORM_EPS = 1e-6

S5_GROUP = 16
S5_GROUPS = 64
S5_STATE = 64

MLA_NOPE = 128
MLA_ROPE = 64
MLA_HEADS = 8
MLA_Q_RANK = 768
MLA_KV_RANK = 512
ROPE_THETA = 10000.0
MLA_SCALE = (MLA_NOPE + MLA_ROPE) ** -0.5
NEG_INF = -1e30
LOG2_E = 1.4426950408889634
MLA_TQ = 512
GLA_HEADS = 4
GLA_DV = 256
GLA_DK = 128
GLA_GATE_RANK = 16
GLA_TAU = 16.0
GLA_CHUNK = 64

LRU_BLOCKS = 8
LRU_BLOCK_W = 128
LRU_CONV = 4
LRU_C = 8.0

LANES = 128
SUBLANES = 8

Z_S5_U, Z_S5_GATE, Z_MLA_GATE, Z_GLA_V, Z_GLA_GATE, Z_LRU_U, Z_LRU_GATE = range(7)
Z_CKV_512 = 14
Z_CQ_768 = 10
Z_KR_128 = 66
Z_GLR_128 = 67
Z_GLA_Q_512 = 17
Z_GLA_K_512 = 18
Z_COLS = 9728

VMEM_LIMIT = 56 * 1024 * 1024
NORM_ROWS = 256
S5_SCAN_W = 8


def _cparams(n_axes):
    return pltpu.CompilerParams(dimension_semantics=("arbitrary",) * n_axes,
                                vmem_limit_bytes=VMEM_LIMIT)


def _layer_spec(a, layer):
    zeros = (0,) * (a.ndim - 1)
    return pl.BlockSpec((None,) + tuple(a.shape[1:]), lambda *_: (layer,) + zeros)


def _whole_spec(a):
    return pl.BlockSpec(a.shape, lambda *_: (0,) * a.ndim)


def _rms(x, g):
    return x * lax.rsqrt(jnp.mean(x * x, axis=-1, keepdims=True) + NORM_EPS) * g


def _silu(x):
    return x * jax.nn.sigmoid(x)


def _inproj_kernel(rows_ref, x_ref, g_ref, w_ref, o_ref, h_ref):
    del rows_ref
    nt = (((1,), (1,)), ((), ()))
    first = pl.program_id(1) == 0

    @pl.when(first)
    def _():
        for r in range(x_ref.shape[0] // NORM_ROWS):
            rows = slice(r * NORM_ROWS, (r + 1) * NORM_ROWS)
            h = _rms(x_ref[rows, :], g_ref[...]).astype(BF16)
            h_ref[rows, :] = h
            o_ref[rows, :] = lax.dot_general(h, w_ref[0], nt, preferred_element_type=F32).astype(o_ref.dtype)

    @pl.when(jnp.logical_not(first))
    def _():
        o_ref[...] = lax.dot_general(h_ref[...], w_ref[0], nt,
                                     preferred_element_type=F32).astype(o_ref.dtype)


W_IN_TN = 512
W_IN_ROWS = 9552
_W_IN_BLOCK_ROWS = (0, 512, 1024, 1536,
                    3392, 3904,
                    5440, 5952, 6480, 6992,
                    7504, 8016, 8528, 9040,
                    2816, 2048,
                    W_IN_ROWS,
                    4416, 4928)


def _inproj(x2, g, w_t, layer, *, tm=1024):
    t, d = x2.shape
    tn = W_IN_TN
    nblk = len(_W_IN_BLOCK_ROWS)
    rows = jnp.asarray(_W_IN_BLOCK_ROWS, jnp.int32)
    grid_spec = pltpu.PrefetchScalarGridSpec(
        num_scalar_prefetch=1,
        grid=(t // tm, nblk),
        in_specs=[pl.BlockSpec((tm, d), lambda i, j, r: (i, 0)),
                  _layer_spec(g, layer),
                  pl.BlockSpec((pl.Element(1), pl.Element(tn), pl.Element(d)),
                               lambda i, j, r: (layer, pl.multiple_of(r[j], 16), 0))],
        out_specs=pl.BlockSpec((tm, tn), lambda i, j, r: (i, j)),
        scratch_shapes=[pltpu.VMEM((tm, d), BF16)])
    return pl.pallas_call(
        _inproj_kernel,
        grid_spec=grid_spec,
        out_shape=jax.ShapeDtypeStruct((t, nblk * tn), BF16),
        compiler_params=_cparams(2),
        name="inproj",
    )(rows, x2, g, w_t)


def _w_in_t(w_in):
    wt = jnp.swapaxes(w_in, 1, 2)
    depth, _, d = wt.shape
    extra = [wt[:, 2560:2816], wt[:, 3328:3392], wt[:, 3360:3392], wt[:, 3328:3360],
             wt[:, 6464:6480], jnp.zeros((depth, 128 - GLA_GATE_RANK, d), wt.dtype)]
    return jnp.concatenate([wt] + extra, axis=1).astype(BF16)


def _s5_kernel(u_ref, gate_ref, wb_ref, lamr_ref, lami_ref, wcr_ref, wci_ref, d_ref,
               wglu_ref, bglu_ref, g_ref, o_ref,
               bre_ref, bim_ref, hre_ref, him_ref, y_ref, ush_ref, ysh_ref, *, nb, tc, pitch):
    shift = pitch - tc
    tcs = tc + 2 * shift

    @pl.when(pl.program_id(0) == 0)
    def _():
        hre_ref[...] = jnp.zeros_like(hre_ref)
        him_ref[...] = jnp.zeros_like(him_ref)
        ush_ref[...] = jnp.zeros_like(ush_ref)

    ub = u_ref[...].reshape(nb * tc, BRANCH)
    ush_ref[:, shift:shift + tc, :] = u_ref[...].astype(F32)
    ubs = ush_ref[...].reshape(nb * tcs, BRANCH).astype(BF16)

    def block_rows(b, half):
        if half:
            return pl.ds((2 * b + 1) * pitch - shift, tcs)
        return pl.ds(2 * b * pitch, tc)

    for kb in range(8):
        half = kb // 4
        lhs, m = (ubs, tcs) if half else (ub, tc)
        res = jnp.dot(lhs[:, kb * 128:(kb + 1) * 128], wb_ref[kb], preferred_element_type=F32)
        for b in range(nb):
            rows = block_rows(b, half)
            for jj in range(4):
                j = (kb % 4) * 4 + jj
                bre_ref[j, rows, :] = res[b * m:(b + 1) * m, jj * 128:(jj + 1) * 128]
                bim_ref[j, rows, :] = res[b * m:(b + 1) * m, 512 + jj * 128:512 + (jj + 1) * 128]

    for cb in range(16 // S5_SCAN_W):
        slabs = [cb * S5_SCAN_W + jj for jj in range(S5_SCAN_W)]
        lr = [lamr_ref[:, j * 128:(j + 1) * 128] for j in slabs]
        li = [lami_ref[:, j * 128:(j + 1) * 128] for j in slabs]
        init = []
        for j in slabs:
            init.append(hre_ref[:, j * 128:(j + 1) * 128])
            init.append(him_ref[:, j * 128:(j + 1) * 128])

        def body(t, carry, slabs=slabs, lr=lr, li=li):
            new = []
            idx = pl.ds(t, SUBLANES, stride=pitch)
            for jj, j in enumerate(slabs):
                hr, hi = carry[2 * jj], carry[2 * jj + 1]
                nr = lr[jj] * hr - li[jj] * hi + bre_ref[j, idx, :]
                ni = lr[jj] * hi + li[jj] * hr + bim_ref[j, idx, :]
                bre_ref[j, idx, :] = nr
                bim_ref[j, idx, :] = ni
                new += [nr, ni]
            return tuple(new)

        fin = lax.fori_loop(0, tc, body, tuple(init), unroll=2)
        for jj, j in enumerate(slabs):
            hre_ref[:, j * 128:(j + 1) * 128] = fin[2 * jj]
            him_ref[:, j * 128:(j + 1) * 128] = fin[2 * jj + 1]

    for kb in range(8):
        half = kb // 4
        sre, sim = [], []
        for b in range(nb):
            rows = block_rows(b, half)
            sre.append(jnp.concatenate([bre_ref[(kb % 4) * 4 + jj, rows, :] for jj in range(4)], axis=-1))
            sim.append(jnp.concatenate([bim_ref[(kb % 4) * 4 + jj, rows, :] for jj in range(4)], axis=-1))
        sre = jnp.concatenate(sre, axis=0).astype(BF16)
        sim = jnp.concatenate(sim, axis=0).astype(BF16)
        ykb = (jnp.dot(sre, wcr_ref[kb], preferred_element_type=F32)
               + jnp.dot(sim, wci_ref[kb], preferred_element_type=F32))
        if half:
            ysh_ref[...] = ykb.reshape(nb, tcs, 128)
            ykb = ysh_ref[:, shift:shift + tc, :].reshape(nb * tc, 128)
        cols = slice(kb * 128, (kb + 1) * 128)
        y_ref[:, cols] = ykb + d_ref[:, cols] * ub[:, cols].astype(F32)

    y = jax.nn.gelu(y_ref[...], approximate=True)
    zg = jnp.dot(y.astype(BF16), wglu_ref[...], preferred_element_type=F32) + bglu_ref[...]
    y = y * jax.nn.sigmoid(zg)
    gate = gate_ref[...].reshape(nb * tc, BRANCH).astype(F32)
    o_ref[...] = (_rms(y, g_ref[...]) * _silu(gate)).astype(BF16).reshape(nb, tc, BRANCH)


def _s5(z3, layer, wb, lamr, lami, wcr, wci, d, wglu, bglu, g, *, tc=128):
    nb, seq, _ = z3.shape
    assert 2 * nb == SUBLANES
    pitch = tc + SUBLANES // 2
    full = functools.partial(_layer_spec, layer=layer)
    kern = functools.partial(_s5_kernel, nb=nb, tc=tc, pitch=pitch)
    return pl.pallas_call(
        kern,
        grid=(seq // tc,),
        in_specs=[pl.BlockSpec((nb, tc, BRANCH), lambda i: (0, i, Z_S5_U)),
                  pl.BlockSpec((nb, tc, BRANCH), lambda i: (0, i, Z_S5_GATE)),
                  full(wb), full(lamr), full(lami), full(wcr), full(wci), full(d),
                  full(wglu), full(bglu), full(g)],
        out_specs=pl.BlockSpec((nb, tc, BRANCH), lambda i: (0, i, 0)),
        out_shape=jax.ShapeDtypeStruct((nb, seq, BRANCH), BF16),
        scratch_shapes=[pltpu.VMEM((16, SUBLANES * pitch, LANES), F32),
                        pltpu.VMEM((16, SUBLANES * pitch, LANES), F32),
                        pltpu.VMEM((SUBLANES, 2048), F32),
                        pltpu.VMEM((SUBLANES, 2048), F32),
                        pltpu.VMEM((nb * tc, BRANCH), F32),
                        pltpu.VMEM((nb, tc + SUBLANES, BRANCH), F32),
                        pltpu.VMEM((nb, tc + SUBLANES, LANES), F32)],
        compiler_params=_cparams(1),
        name="s5",
    )(z3, z3, wb, lamr, lami, wcr, wci, d, wglu, bglu, g)


def _s5_params(lam_re, lam_im, b_re, b_im, c_re, c_im, log_dt, nb):
    dt = jnp.exp(log_dt.astype(F32))[:, None]
    lr, li = lam_re.astype(F32), lam_im.astype(F32)
    mag = jnp.exp(lr * dt)
    lbr, lbi = mag * jnp.cos(li * dt), mag * jnp.sin(li * dt)
    den = lr * lr + li * li
    cr = ((lbr - 1.0) * lr + lbi * li) / den
    ci = (lbi * lr - (lbr - 1.0) * li) / den
    bbr = cr[..., None] * b_re - ci[..., None] * b_im
    bbi = cr[..., None] * b_im + ci[..., None] * b_re
    eye = jnp.eye(8, dtype=F32)

    def in_layout(m):
        m = m.reshape(8, 8, S5_STATE, S5_GROUP)
        return jnp.einsum('kgph,gG->kghGp', m, eye).reshape(8, 128, 512)

    def out_layout(m):
        m = m.reshape(8, 8, S5_GROUP, S5_STATE)
        return jnp.einsum('kghp,gG->kgpGh', m, eye).reshape(8, 512, 128)

    wb = jnp.concatenate([in_layout(bbr), in_layout(bbi)], axis=-1).astype(BF16)
    wcr = out_layout(c_re.astype(F32)).astype(BF16)
    wci = out_layout(-c_im.astype(F32)).astype(BF16)

    def seq_layout(v):
        return jnp.tile(v.reshape(1, 2, 2048), (nb, 1, 1)).reshape(2 * nb, 2048)

    return wb, seq_layout(lbr), seq_layout(lbi), wcr, wci


def _lru_kernel(u_ref, gate_ref, cw_ref, cb_ref, wax_ref, ba_ref, bx_ref, sp_ref, g_ref, o_ref,
                xpad_ref, a_ref, h_ref, st_ref, *, nb, tc, pitch):
    @pl.when(pl.program_id(0) == 0)
    def _():
        xpad_ref[:, 0:SUBLANES, :] = jnp.zeros((nb, SUBLANES, BRANCH), F32)
        st_ref[...] = jnp.zeros_like(st_ref)

    xpad_ref[:, SUBLANES:SUBLANES + tc, :] = u_ref[...].astype(F32)
    xc = cb_ref[...].reshape(1, 1, BRANCH)
    for k in range(LRU_CONV):
        off = SUBLANES - (LRU_CONV - 1) + k
        xc = xc + cw_ref[k:k + 1, :].reshape(1, 1, BRANCH) * xpad_ref[:, off:off + tc, :]
    xpad_ref[:, 0:SUBLANES, :] = xpad_ref[:, tc:tc + SUBLANES, :]
    x2 = xc.reshape(nb * tc, BRANCH)
    xb = x2.astype(BF16)

    for hb in range(LRU_BLOCKS):
        cols = slice(hb * 128, (hb + 1) * 128)
        ri = jnp.dot(xb[:, cols], wax_ref[hb], preferred_element_type=F32)
        r = jax.nn.sigmoid(ri[:, :128] + ba_ref[:, cols])
        ig = jax.nn.sigmoid(ri[:, 128:] + bx_ref[:, cols])
        log_a = (-LRU_C) * r * sp_ref[:, cols]
        a = jnp.exp(log_a)
        gated = jnp.sqrt(-jnp.tanh(log_a) * (a * a + 1.0)) * (ig * x2[:, cols])
        half, j = hb // 4, hb % 4
        for b in range(nb):
            rows = pl.ds((2 * b + half) * pitch, tc)
            a_ref[j, rows, :] = a[b * tc:(b + 1) * tc]
            h_ref[j, rows, :] = gated[b * tc:(b + 1) * tc]

    def body(t, carry):
        new = []
        for j in range(4):
            idx = pl.ds(t, SUBLANES, stride=pitch)
            h = a_ref[j, idx, :] * carry[j] + h_ref[j, idx, :]
            h_ref[j, idx, :] = h
            new.append(h)
        return tuple(new)

    init = tuple(st_ref[:, j * 128:(j + 1) * 128] for j in range(4))
    fin = lax.fori_loop(0, tc, body, init, unroll=8)
    for j in range(4):
        st_ref[:, j * 128:(j + 1) * 128] = fin[j]

    for b in range(nb):
        parts = [h_ref[j, pl.ds((2 * b + half) * pitch, tc), :] for half in range(2) for j in range(4)]
        hb_ = jnp.concatenate(parts, axis=-1)
        o_ref[b] = (_rms(hb_, g_ref[...]) * _silu(gate_ref[b].astype(F32))).astype(BF16)


def _lru(z3, layer, cw, cb, wax, ba, bx, sp, g, *, tc=128):
    nb, seq, _ = z3.shape
    assert 2 * nb == SUBLANES
    pitch = tc + SUBLANES
    full = functools.partial(_layer_spec, layer=layer)
    kern = functools.partial(_lru_kernel, nb=nb, tc=tc, pitch=pitch)
    return pl.pallas_call(
        kern,
        grid=(seq // tc,),
        in_specs=[pl.BlockSpec((nb, tc, BRANCH), lambda i: (0, i, Z_LRU_U)),
                  pl.BlockSpec((nb, tc, BRANCH), lambda i: (0, i, Z_LRU_GATE)),
                  full(cw), full(cb), full(wax), full(ba), full(bx), full(sp), full(g)],
        out_specs=pl.BlockSpec((nb, tc, BRANCH), lambda i: (0, i, 0)),
        out_shape=jax.ShapeDtypeStruct((nb, seq, BRANCH), BF16),
        scratch_shapes=[pltpu.VMEM((nb, tc + SUBLANES, BRANCH), F32),
                        pltpu.VMEM((4, SUBLANES * pitch, LANES), F32),
                        pltpu.VMEM((4, SUBLANES * pitch, LANES), F32),
                        pltpu.VMEM((SUBLANES, 512), F32)],
        compiler_params=_cparams(1),
        name="rglru",
    )(z3, z3, cw, cb, wax, ba, bx, sp, g)


def _gla_kernel(q_ref, k_ref, v_ref, glr_ref, gate_ref, wg_ref, bg_ref, og_ref, tril_ref, o_ref,
                st_ref, *, tg):
    @pl.when(pl.program_id(1) == 0)
    def _():
        st_ref[...] = jnp.zeros_like(st_ref)

    n_ch = tg // GLA_CHUNK
    x = jnp.dot(glr_ref[...], wg_ref[...], preferred_element_type=F32) + bg_ref[...]
    log_a = (jnp.minimum(x, 0.0) - jnp.log1p(jnp.exp(-jnp.abs(x)))) / GLA_TAU

    tril = tril_ref[...]
    p0 = log_a.astype(BF16)
    r0 = log_a - p0.astype(F32)
    p1 = r0.astype(BF16)
    p2 = (r0 - p1.astype(F32)).astype(BF16)
    cum = (jnp.dot(tril, p0, preferred_element_type=F32)
           + jnp.dot(tril, p1, preferred_element_type=F32)
           + jnp.dot(tril, p2, preferred_element_type=F32))

    row = lax.broadcasted_iota(jnp.int32, (tg, tg), 0)
    col = lax.broadcasted_iota(jnp.int32, (tg, tg), 1)
    causal = (col <= row) & ((row // GLA_CHUNK) == (col // GLA_CHUNK))
    nt = (((1,), (1,)), ((), ()))
    tn = (((0,), (0,)), ((), ()))

    for hh in range(GLA_HEADS):
        kc = slice(hh * GLA_DK, (hh + 1) * GLA_DK)
        vc = slice(hh * GLA_DV, (hh + 1) * GLA_DV)
        cumh = cum[:, kc]
        kh = k_ref[:, kc].astype(F32)
        qd = (q_ref[:, kc].astype(F32) * (GLA_DK ** -0.5)) * jnp.exp(cumh)
        ki = kh * jnp.exp(-cumh)
        qdb = qd.astype(BF16)
        att = lax.dot_general(qdb, ki.astype(BF16), nt, preferred_element_type=F32)
        att = jnp.where(causal, att, 0.0)
        vb = v_ref[:, vc]
        o = jnp.dot(att.astype(BF16), vb, preferred_element_type=F32)
        st = st_ref[hh]
        inter = []
        for c in range(n_ch):
            rows = slice(c * GLA_CHUNK, (c + 1) * GLA_CHUNK)
            tot = cumh[(c + 1) * GLA_CHUNK - 1:(c + 1) * GLA_CHUNK, :]
            kd = kh[rows] * jnp.exp(tot - cumh[rows])
            inter.append(lax.dot_general(qdb[rows], st.astype(BF16), nt, preferred_element_type=F32))
            kv_t = lax.dot_general(vb[rows], kd.astype(BF16), tn, preferred_element_type=F32)
            st = st * jnp.exp(tot) + kv_t
        st_ref[hh] = st
        o = o + jnp.concatenate(inter, axis=0)
        gate = gate_ref[:, vc].astype(F32)
        o_ref[:, vc] = (_rms(o, og_ref[:, vc]) * _silu(gate)).astype(BF16)


def _gla(z3, layer, wg, bg, og, tril, *, tg=256):
    nb, seq, _ = z3.shape
    kern = functools.partial(_gla_kernel, tg=tg)
    full = functools.partial(_layer_spec, layer=layer)
    return pl.pallas_call(
        kern,
        grid=(nb, seq // tg),
        in_specs=[pl.BlockSpec((None, tg, 512), lambda b, i: (b, i, Z_GLA_Q_512)),
                  pl.BlockSpec((None, tg, 512), lambda b, i: (b, i, Z_GLA_K_512)),
                  pl.BlockSpec((None, tg, BRANCH), lambda b, i: (b, i, Z_GLA_V)),
                  pl.BlockSpec((None, tg, 128), lambda b, i: (b, i, Z_GLR_128)),
                  pl.BlockSpec((None, tg, BRANCH), lambda b, i: (b, i, Z_GLA_GATE)),
                  full(wg), full(bg), full(og), _whole_spec(tril)],
        out_specs=pl.BlockSpec((None, tg, BRANCH), lambda b, i: (b, i, 0)),
        out_shape=jax.ShapeDtypeStruct((nb, seq, BRANCH), BF16),
        scratch_shapes=[pltpu.VMEM((GLA_HEADS, GLA_DV, GLA_DK), F32)],
        compiler_params=_cparams(2),
        name="gla",
    )(z3, z3, z3, z3, z3, wg, bg, og, tril)


def _mla_proj_kernel(cq_ref, ckv_ref, kr_ref, pos_ref, gq_ref, gkv_ref, wq_ref, wk_ref, wvt_ref,
                     invf_ref, sgn_ref, msk_ref, q_out, k_out, vt_out):
    cqn = _rms(cq_ref[...].astype(F32), gq_ref[...]).astype(BF16)
    qa = jnp.dot(cqn, wq_ref[...], preferred_element_type=F32)
    ang = pos_ref[...].astype(F32) * invf_ref[...]
    cosm = jnp.cos(ang) * msk_ref[...]
    sinm = jnp.sin(ang) * sgn_ref[...]
    nh = MLA_HEADS * 128
    half = MLA_ROPE // 2
    lo_half = lax.broadcasted_iota(jnp.int32, (1, 128), 1) < half
    qscale = MLA_SCALE * LOG2_E
    for h in range(MLA_HEADS):
        q_out[:, h * 256:h * 256 + 128] = (qa[:, h * 128:(h + 1) * 128] * qscale).astype(BF16)
        qr = qa[:, nh + h * 128:nh + (h + 1) * 128]
        swapped = jnp.where(lo_half, pltpu.roll(qr, 128 - half, 1), pltpu.roll(qr, half, 1))
        q_out[:, h * 256 + 128:(h + 1) * 256] = ((qr * cosm + swapped * sinm) * qscale).astype(BF16)
    c = _rms(ckv_ref[...].astype(F32), gkv_ref[...]).astype(BF16)
    kn = jnp.dot(c, wk_ref[...], preferred_element_type=F32)
    kr = kr_ref[...].astype(F32)
    krp = (kr * cosm + pltpu.roll(kr, 64, 1) * sinm).astype(BF16)
    for h in range(MLA_HEADS):
        k_out[:, h * 256:h * 256 + 128] = kn[:, h * 128:(h + 1) * 128].astype(BF16)
        k_out[:, h * 256 + 128:(h + 1) * 256] = krp
    nt = (((1,), (1,)), ((), ()))
    vt_out[...] = lax.dot_general(wvt_ref[...], c, nt, preferred_element_type=F32).astype(BF16)


def _mla_proj(z2, pos2, layer, gq, gkv, wq, wk, wvt, invf, sgn, msk, *, tm):
    t = z2.shape[0]
    full = functools.partial(_layer_spec, layer=layer)
    return pl.pallas_call(
        _mla_proj_kernel,
        grid=(t // tm,),
        in_specs=[pl.BlockSpec((tm, MLA_Q_RANK), lambda i: (i, Z_CQ_768)),
                  pl.BlockSpec((tm, MLA_KV_RANK), lambda i: (i, Z_CKV_512)),
                  pl.BlockSpec((tm, 128), lambda i: (i, Z_KR_128)),
                  pl.BlockSpec((tm, 1), lambda i: (i, 0)),
                  full(gq), full(gkv), full(wq), full(wk), full(wvt),
                  _whole_spec(invf), _whole_spec(sgn), _whole_spec(msk)],
        out_specs=[pl.BlockSpec((tm, 2048), lambda i: (i, 0)),
                   pl.BlockSpec((tm, 2048), lambda i: (i, 0)),
                   pl.BlockSpec((None, BRANCH, tm), lambda i: (i, 0, 0))],
        out_shape=[jax.ShapeDtypeStruct((t, 2048), BF16),
                   jax.ShapeDtypeStruct((t, 2048), BF16),
                   jax.ShapeDtypeStruct((t // tm, BRANCH, tm), BF16)],
        compiler_params=_cparams(1),
        name="mla_proj",
    )(z2, z2, z2, pos2, gq, gkv, wq, wk, wvt, invf, sgn, msk)


def _mla_attn_kernel(q_ref, k_ref, vt_ref, gate_ref, g_ref, o_ref, m_ref, l_ref, acc_ref, o_scr, *, tq):
    qi = pl.program_id(1)
    nt = (((1,), (1,)), ((), ()))
    key = lax.broadcasted_iota(jnp.int32, (tq, tq), 0)
    qry = lax.broadcasted_iota(jnp.int32, (tq, tq), 1)

    m_ref[...] = jnp.full(m_ref.shape, NEG_INF, F32)
    l_ref[...] = jnp.zeros_like(l_ref)
    acc_ref[...] = jnp.zeros_like(acc_ref)

    def step(j, masked):
        r0 = pl.multiple_of(j * tq, tq)

        def scores(h):
            st = lax.dot_general(k_ref[pl.ds(r0, tq), h * 256:(h + 1) * 256],
                                 q_ref[:, h * 256:(h + 1) * 256], nt, preferred_element_type=F32)
            if masked:
                st = jnp.where(key <= qry, st, NEG_INF)
            m_old = m_ref[h]
            return st, m_old, jnp.maximum(m_old, jnp.max(st, axis=0, keepdims=True))

        nxt = scores(0)
        for h in range(MLA_HEADS):
            st, m_old, m_new = nxt
            if h + 1 < MLA_HEADS:
                nxt = scores(h + 1)
            alpha = jnp.exp2(m_old - m_new)
            pt = jnp.exp2(st - m_new)
            l_ref[h] = alpha * l_ref[h] + jnp.sum(pt, axis=0, keepdims=True)
            acc_ref[h] = alpha * acc_ref[h] + jnp.dot(
                vt_ref[j, h * 128:(h + 1) * 128, :], pt.astype(BF16), preferred_element_type=F32)
            m_ref[h] = m_new

    def body(j, c):
        step(j, False)
        return c

    lax.fori_loop(0, qi, body, 0)
    step(qi, True)

    for h in range(MLA_HEADS):
        o_scr[:, h * 128:(h + 1) * 128] = (acc_ref[h] / l_ref[h]).T
    o_ref[...] = (_rms(o_scr[...], g_ref[...]) * _silu(gate_ref[...].astype(F32))).astype(BF16)


def _mla_attn(q3, k3, vt4, z3, layer, g, *, tq):
    nb, seq, _ = q3.shape
    kern = functools.partial(_mla_attn_kernel, tq=tq)
    return pl.pallas_call(
        kern,
        grid=(nb, seq // tq),
        in_specs=[pl.BlockSpec((None, tq, 2048), lambda b, i: (b, i, 0)),
                  pl.BlockSpec((None, seq, 2048), lambda b, i: (b, 0, 0)),
                  pl.BlockSpec((None, seq // tq, BRANCH, tq), lambda b, i: (b, 0, 0, 0)),
                  pl.BlockSpec((None, tq, BRANCH), lambda b, i: (b, i, Z_MLA_GATE)),
                  _layer_spec(g, layer)],
        out_specs=pl.BlockSpec((None, tq, BRANCH), lambda b, i: (b, i, 0)),
        out_shape=jax.ShapeDtypeStruct((nb, seq, BRANCH), BF16),
        scratch_shapes=[pltpu.VMEM((MLA_HEADS, 1, tq), F32),
                        pltpu.VMEM((MLA_HEADS, 1, tq), F32),
                        pltpu.VMEM((MLA_HEADS, 128, tq), F32),
                        pltpu.VMEM((tq, BRANCH), F32)],
        compiler_params=_cparams(2),
        name="mla_attn",
    )(q3, k3, vt4, z3, g)


def _outproj_kernel(y0_ref, y1_ref, y2_ref, y3_ref, w_ref, x_ref, o_ref, wb_ref):
    first = pl.program_id(1) == 0
    ys = (y0_ref, y1_ref, y2_ref, y3_ref)

    @pl.when(first)
    def _():
        acc = x_ref[...]
        for n, y_ref in enumerate(ys):
            wb = w_ref[n * BRANCH:(n + 1) * BRANCH, :].astype(BF16)
            wb_ref[n * BRANCH:(n + 1) * BRANCH, :] = wb
            acc = acc + jnp.dot(y_ref[...], wb, preferred_element_type=F32)
        o_ref[...] = acc

    @pl.when(jnp.logical_not(first))
    def _():
        acc = x_ref[...]
        for n, y_ref in enumerate(ys):
            acc = acc + jnp.dot(y_ref[...], wb_ref[n * BRANCH:(n + 1) * BRANCH, :], preferred_element_type=F32)
        o_ref[...] = acc


def _outproj(ys, w_all, layer, x2, *, tm=1024, tn=512):
    t, d = x2.shape
    yspec = pl.BlockSpec((tm, BRANCH), lambda j, i: (i, 0))
    return pl.pallas_call(
        _outproj_kernel,
        grid=(d // tn, t // tm),
        in_specs=[yspec, yspec, yspec, yspec,
                  pl.BlockSpec((None, 4 * BRANCH, tn), lambda j, i: (layer, 0, j)),
                  pl.BlockSpec((tm, tn), lambda j, i: (i, j))],
        out_specs=pl.BlockSpec((tm, tn), lambda j, i: (i, j)),
        out_shape=jax.ShapeDtypeStruct((t, d), F32),
        scratch_shapes=[pltpu.VMEM((4 * BRANCH, tn), BF16)],
        compiler_params=_cparams(2),
        name="outproj",
    )(*ys, w_all, x2)


def _final_norm_kernel(x_ref, g_ref, o_ref):
    o_ref[...] = _rms(x_ref[...], g_ref[...])


def _final_norm(x2, g, *, tm=512):
    t, d = x2.shape
    return pl.pallas_call(
        _final_norm_kernel,
        grid=(t // tm,),
        in_specs=[pl.BlockSpec((tm, d), lambda i: (i, 0)), pl.BlockSpec((1, d), lambda i: (0, 0))],
        out_specs=pl.BlockSpec((tm, d), lambda i: (i, 0)),
        out_shape=jax.ShapeDtypeStruct((t, d), F32),
        compiler_params=_cparams(1),
        name="final_norm",
    )(x2, g)


def _mla_q_weight(w_uq):
    w = w_uq.reshape(MLA_Q_RANK, MLA_HEADS, MLA_NOPE + MLA_ROPE)
    nope = w[:, :, :MLA_NOPE].reshape(MLA_Q_RANK, MLA_HEADS * MLA_NOPE)
    pad = ((0, 0), (0, 0), (0, 128 - MLA_ROPE))
    rope_p = jnp.pad(w[:, :, MLA_NOPE:], pad).reshape(MLA_Q_RANK, MLA_HEADS * 128)
    return jnp.concatenate([nope, rope_p], axis=1).astype(BF16)


def _rope_tables():
    half = MLA_ROPE // 2
    inv = 1.0 / (ROPE_THETA ** (jnp.arange(half, dtype=F32) / half))
    zeros = jnp.zeros((128 - MLA_ROPE,), F32)
    invf = jnp.concatenate([inv, inv, zeros]).reshape(1, 128)
    sgn = jnp.concatenate([-jnp.ones((half,), F32), jnp.ones((half,), F32), zeros]).reshape(1, 128)
    msk = jnp.concatenate([jnp.ones((MLA_ROPE,), F32), zeros]).reshape(1, 128)
    return invf, sgn, msk


def _gla_tril(tg):
    r = np.arange(tg)
    m = (r[None, :] <= r[:, None]) & ((r[:, None] // GLA_CHUNK) == (r[None, :] // GLA_CHUNK))
    return jnp.asarray(m, dtype=BF16)


def kernel(x, positions, norm_g, w_in, s5_lambda_re, s5_lambda_im, s5_b_re, s5_b_im, s5_c_re, s5_c_im,
           s5_d, s5_log_dt, s5_w_glu, s5_b_glu, s5_out_g, mla_q_norm_g, mla_kv_norm_g, mla_w_uq,
           mla_w_uk, mla_w_uv, mla_out_g, gla_w_gate, gla_b_gate, gla_out_g, lru_conv_w, lru_conv_b,
           lru_w_a, lru_b_a, lru_w_x, lru_b_x, lru_lambda, lru_out_g, w_out, final_g):
    nb, seq, d = x.shape
    depth = w_in.shape[0]
    t = nb * seq
    x2 = x.reshape(t, d)
    pos2 = positions.reshape(t, 1)
    invf, sgn, msk = _rope_tables()
    gla_tg = 256
    tril = _gla_tril(gla_tg)

    rows = lambda v: v.reshape(depth, 1, -1).astype(F32)
    w_in_t = _w_in_t(w_in)
    norm_g_r = rows(norm_g)
    s5_wb, s5_lamr, s5_lami, s5_wcr, s5_wci = jax.vmap(functools.partial(_s5_params, nb=nb))(
        s5_lambda_re, s5_lambda_im, s5_b_re, s5_b_im, s5_c_re, s5_c_im, s5_log_dt)
    s5_rest = (rows(s5_d), s5_w_glu.astype(BF16), rows(s5_b_glu), rows(s5_out_g))
    mla_w = (rows(mla_q_norm_g), rows(mla_kv_norm_g), jax.vmap(_mla_q_weight)(mla_w_uq),
             mla_w_uk.astype(BF16), jnp.swapaxes(mla_w_uv, 1, 2).astype(BF16))
    mla_out_g_r = rows(mla_out_g)
    gla_w = (jnp.pad(gla_w_gate, ((0, 0), (0, 128 - GLA_GATE_RANK), (0, 0))).astype(BF16),
             rows(gla_b_gate), rows(gla_out_g))
    lru_w = (lru_conv_w.astype(F32), rows(lru_conv_b),
             jnp.concatenate([lru_w_a, lru_w_x], axis=-1).astype(BF16),
             rows(lru_b_a), rows(lru_b_x), rows(jax.nn.softplus(-lru_lambda.astype(F32))),
             rows(lru_out_g))

    for l in range(depth):
        z2 = _inproj(x2, norm_g_r, w_in_t, l)
        z3 = z2.reshape(nb, seq, Z_COLS)
        y_s5 = _s5(z3, l, s5_wb, s5_lamr, s5_lami, s5_wcr, s5_wci, *s5_rest)
        q2, k2, vt3 = _mla_proj(z2, pos2, l, *mla_w, invf, sgn, msk, tm=MLA_TQ)
        y_mla = _mla_attn(q2.reshape(nb, seq, 2048), k2.reshape(nb, seq, 2048),
                          vt3.reshape(nb, seq // MLA_TQ, BRANCH, MLA_TQ), z3, l, mla_out_g_r, tq=MLA_TQ)
        y_gla = _gla(z3, l, *gla_w, tril, tg=gla_tg)
        y_lru = _lru(z3, l, *lru_w)
        ys = [y.reshape(t, BRANCH) for y in (y_s5, y_mla, y_gla, y_lru)]
        x2 = _outproj(ys, w_out, l, x2)

    return _final_norm(x2, final_g.reshape(1, d).astype(F32)).reshape(nb, seq, d)
```

```python
import functools

import jax
import jax.numpy as jnp
import numpy as np
from jax import lax
from jax.experimental import pallas as pl
from jax.experimental.pallas import tpu as pltpu

F32 = jnp.float32
BF16 = jnp.bfloat16

D_MODEL = 4096
BRANCH = 1024
NORM_EPS = 1e-6

S5_GROUP = 16
S5_GROUPS = 64
S5_STATE = 64

MLA_NOPE = 128
MLA_ROPE = 64
MLA_HEADS = 8
MLA_Q_RANK = 768
MLA_KV_RANK = 512
ROPE_THETA = 10000.0
MLA_SCALE = (MLA_NOPE + MLA_ROPE) ** -0.5
NEG_INF = -1e30
LOG2_E = 1.4426950408889634
MLA_TQ = 512
GLA_HEADS = 4
GLA_DV = 256
GLA_DK = 128
GLA_GATE_RANK = 16
GLA_TAU = 16.0
GLA_CHUNK = 64

LRU_BLOCKS = 8
LRU_BLOCK_W = 128
LRU_CONV = 4
LRU_C = 8.0

LANES = 128
SUBLANES = 8

Z_S5_U, Z_S5_GATE, Z_MLA_GATE, Z_GLA_V, Z_GLA_GATE, Z_LRU_U, Z_LRU_GATE = range(7)
Z_CKV_512 = 14
Z_CQ_768 = 10
Z_KR_128 = 66
Z_GLR_128 = 67
Z_GLA_Q_512 = 17
Z_GLA_K_512 = 18
Z_COLS = 9728

VMEM_LIMIT = 56 * 1024 * 1024
NORM_ROWS = 256
S5_SCAN_W = 8


def _cparams(n_axes):
    return pltpu.CompilerParams(dimension_semantics=("arbitrary",) * n_axes,
                                vmem_limit_bytes=VMEM_LIMIT)


def _layer_spec(a, layer):
    zeros = (0,) * (a.ndim - 1)
    return pl.BlockSpec((None,) + tuple(a.shape[1:]), lambda *_: (layer,) + zeros)


def _whole_spec(a):
    return pl.BlockSpec(a.shape, lambda *_: (0,) * a.ndim)


def _rms(x, g):
    return x * lax.rsqrt(jnp.mean(x * x, axis=-1, keepdims=True) + NORM_EPS) * g


def _silu(x):
    return x * jax.nn.sigmoid(x)


def _inproj_kernel(rows_ref, x_ref, g_ref, w_ref, o_ref, h_ref):
    del rows_ref
    nt = (((1,), (1,)), ((), ()))
    first = pl.program_id(1) == 0

    @pl.when(first)
    def _():
        for r in range(x_ref.shape[0] // NORM_ROWS):
            rows = slice(r * NORM_ROWS, (r + 1) * NORM_ROWS)
            h = _rms(x_ref[rows, :], g_ref[...]).astype(BF16)
            h_ref[rows, :] = h
            o_ref[rows, :] = lax.dot_general(h, w_ref[0], nt, preferred_element_type=F32).astype(o_ref.dtype)

    @pl.when(jnp.logical_not(first))
    def _():
        o_ref[...] = lax.dot_general(h_ref[...], w_ref[0], nt,
                                     preferred_element_type=F32).astype(o_ref.dtype)


W_IN_TN = 512
W_IN_ROWS = 9552
_W_IN_BLOCK_ROWS = (0, 512, 1024, 1536,
                    3392, 3904,
                    5440, 5952, 6480, 6992,
                    7504, 8016, 8528, 9040,
                    2816, 2048,
                    W_IN_ROWS,
                    4416, 4928)


def _inproj(x2, g, w_t, layer, *, tm=1024):
    t, d = x2.shape
    tn = W_IN_TN
    nblk = len(_W_IN_BLOCK_ROWS)
    rows = jnp.asarray(_W_IN_BLOCK_ROWS, jnp.int32)
    grid_spec = pltpu.PrefetchScalarGridSpec(
        num_scalar_prefetch=1,
        grid=(t // tm, nblk),
        in_specs=[pl.BlockSpec((tm, d), lambda i, j, r: (i, 0)),
                  _layer_spec(g, layer),
                  pl.BlockSpec((pl.Element(1), pl.Element(tn), pl.Element(d)),
                               lambda i, j, r: (layer, pl.multiple_of(r[j], 16), 0))],
        out_specs=pl.BlockSpec((tm, tn), lambda i, j, r: (i, j)),
        scratch_shapes=[pltpu.VMEM((tm, d), BF16)])
    return pl.pallas_call(
        _inproj_kernel,
        grid_spec=grid_spec,
        out_shape=jax.ShapeDtypeStruct((t, nblk * tn), BF16),
        compiler_params=_cparams(2),
        name="inproj",
    )(rows, x2, g, w_t)


def _w_in_t(w_in):
    wt = jnp.swapaxes(w_in, 1, 2)
    depth, _, d = wt.shape
    extra = [wt[:, 2560:2816], wt[:, 3328:3392], wt[:, 3360:3392], wt[:, 3328:3360],
             wt[:, 6464:6480], jnp.zeros((depth, 128 - GLA_GATE_RANK, d), wt.dtype)]
    return jnp.concatenate([wt] + extra, axis=1).astype(BF16)


def _s5_phases(u_ref, gate_ref, wb_ref, lamr_ref, lami_ref, wcr_ref, wci_ref, d_ref,
               wglu_ref, bglu_ref, g_ref, o_ref,
               bre_ref, bim_ref, hre_ref, him_ref, y_ref, ush_ref, ysh_ref, *, nb, tc, pitch):
    shift = pitch - tc
    tcs = tc + 2 * shift

    @pl.when(pl.program_id(0) == 0)
    def _():
        hre_ref[...] = jnp.zeros_like(hre_ref)
        him_ref[...] = jnp.zeros_like(him_ref)
        ush_ref[...] = jnp.zeros_like(ush_ref)

    yield _PHASE
    ub = u_ref[...].reshape(nb * tc, BRANCH)
    ush_ref[:, shift:shift + tc, :] = u_ref[...].astype(F32)
    ubs = ush_ref[...].reshape(nb * tcs, BRANCH).astype(BF16)

    def block_rows(b, half):
        if half:
            return pl.ds((2 * b + 1) * pitch - shift, tcs)
        return pl.ds(2 * b * pitch, tc)

    for kb in range(8):
        half = kb // 4
        lhs, m = (ubs, tcs) if half else (ub, tc)
        res = jnp.dot(lhs[:, kb * 128:(kb + 1) * 128], wb_ref[kb], preferred_element_type=F32)
        for b in range(nb):
            rows = block_rows(b, half)
            for jj in range(4):
                j = (kb % 4) * 4 + jj
                bre_ref[j, rows, :] = res[b * m:(b + 1) * m, jj * 128:(jj + 1) * 128]
                bim_ref[j, rows, :] = res[b * m:(b + 1) * m, 512 + jj * 128:512 + (jj + 1) * 128]
        yield _STEP

    yield _PHASE
    for cb in range(16 // S5_SCAN_W):
        slabs = [cb * S5_SCAN_W + jj for jj in range(S5_SCAN_W)]
        lr = [lamr_ref[:, j * 128:(j + 1) * 128] for j in slabs]
        li = [lami_ref[:, j * 128:(j + 1) * 128] for j in slabs]
        init = []
        for j in slabs:
            init.append(hre_ref[:, j * 128:(j + 1) * 128])
            init.append(him_ref[:, j * 128:(j + 1) * 128])

        def body(t, carry, slabs=slabs, lr=lr, li=li):
            new = []
            idx = pl.ds(t, SUBLANES, stride=pitch)
            for jj, j in enumerate(slabs):
                hr, hi = carry[2 * jj], carry[2 * jj + 1]
                nr = lr[jj] * hr - li[jj] * hi + bre_ref[j, idx, :]
                ni = lr[jj] * hi + li[jj] * hr + bim_ref[j, idx, :]
                bre_ref[j, idx, :] = nr
                bim_ref[j, idx, :] = ni
                new += [nr, ni]
            return tuple(new)

        fin = lax.fori_loop(0, tc, body, tuple(init), unroll=2)
        for jj, j in enumerate(slabs):
            hre_ref[:, j * 128:(j + 1) * 128] = fin[2 * jj]
            him_ref[:, j * 128:(j + 1) * 128] = fin[2 * jj + 1]

    yield _PHASE
    for kb in range(8):
        half = kb // 4
        sre, sim = [], []
        for b in range(nb):
            rows = block_rows(b, half)
            sre.append(jnp.concatenate([bre_ref[(kb % 4) * 4 + jj, rows, :] for jj in range(4)], axis=-1))
            sim.append(jnp.concatenate([bim_ref[(kb % 4) * 4 + jj, rows, :] for jj in range(4)], axis=-1))
        sre = jnp.concatenate(sre, axis=0).astype(BF16)
        sim = jnp.concatenate(sim, axis=0).astype(BF16)
        ykb = (jnp.dot(sre, wcr_ref[kb], preferred_element_type=F32)
               + jnp.dot(sim, wci_ref[kb], preferred_element_type=F32))
        if half:
            ysh_ref[...] = ykb.reshape(nb, tcs, 128)
            ykb = ysh_ref[:, shift:shift + tc, :].reshape(nb * tc, 128)
        cols = slice(kb * 128, (kb + 1) * 128)
        y_ref[:, cols] = ykb + d_ref[:, cols] * ub[:, cols].astype(F32)
        yield _STEP

    y = jax.nn.gelu(y_ref[...], approximate=True)
    zg = jnp.dot(y.astype(BF16), wglu_ref[...], preferred_element_type=F32) + bglu_ref[...]
    y = y * jax.nn.sigmoid(zg)
    gate = gate_ref[...].reshape(nb * tc, BRANCH).astype(F32)
    o_ref[...] = (_rms(y, g_ref[...]) * _silu(gate)).astype(BF16).reshape(nb, tc, BRANCH)


def _s5_params(lam_re, lam_im, b_re, b_im, c_re, c_im, log_dt, nb):
    dt = jnp.exp(log_dt.astype(F32))[:, None]
    lr, li = lam_re.astype(F32), lam_im.astype(F32)
    mag = jnp.exp(lr * dt)
    lbr, lbi = mag * jnp.cos(li * dt), mag * jnp.sin(li * dt)
    den = lr * lr + li * li
    cr = ((lbr - 1.0) * lr + lbi * li) / den
    ci = (lbi * lr - (lbr - 1.0) * li) / den
    bbr = cr[..., None] * b_re - ci[..., None] * b_im
    bbi = cr[..., None] * b_im + ci[..., None] * b_re
    eye = jnp.eye(8, dtype=F32)

    def in_layout(m):
        m = m.reshape(8, 8, S5_STATE, S5_GROUP)
        return jnp.einsum('kgph,gG->kghGp', m, eye).reshape(8, 128, 512)

    def out_layout(m):
        m = m.reshape(8, 8, S5_GROUP, S5_STATE)
        return jnp.einsum('kghp,gG->kgpGh', m, eye).reshape(8, 512, 128)

    wb = jnp.concatenate([in_layout(bbr), in_layout(bbi)], axis=-1).astype(BF16)
    wcr = out_layout(c_re.astype(F32)).astype(BF16)
    wci = out_layout(-c_im.astype(F32)).astype(BF16)

    def seq_layout(v):
        return jnp.tile(v.reshape(1, 2, 2048), (nb, 1, 1)).reshape(2 * nb, 2048)

    return wb, seq_layout(lbr), seq_layout(lbi), wcr, wci


def _lru_phases(u_ref, gate_ref, cw_ref, cb_ref, wax_ref, ba_ref, bx_ref, sp_ref, g_ref, o_ref,
                xpad_ref, a_ref, h_ref, st_ref, *, nb, tc, pitch):
    @pl.when(pl.program_id(0) == 0)
    def _():
        xpad_ref[:, 0:SUBLANES, :] = jnp.zeros((nb, SUBLANES, BRANCH), F32)
        st_ref[...] = jnp.zeros_like(st_ref)

    yield _PHASE
    xpad_ref[:, SUBLANES:SUBLANES + tc, :] = u_ref[...].astype(F32)
    xc = cb_ref[...].reshape(1, 1, BRANCH)
    for k in range(LRU_CONV):
        off = SUBLANES - (LRU_CONV - 1) + k
        xc = xc + cw_ref[k:k + 1, :].reshape(1, 1, BRANCH) * xpad_ref[:, off:off + tc, :]
    xpad_ref[:, 0:SUBLANES, :] = xpad_ref[:, tc:tc + SUBLANES, :]
    x2 = xc.reshape(nb * tc, BRANCH)
    xb = x2.astype(BF16)
    yield _STEP

    for hb in range(LRU_BLOCKS):
        cols = slice(hb * 128, (hb + 1) * 128)
        ri = jnp.dot(xb[:, cols], wax_ref[hb], preferred_element_type=F32)
        r = jax.nn.sigmoid(ri[:, :128] + ba_ref[:, cols])
        ig = jax.nn.sigmoid(ri[:, 128:] + bx_ref[:, cols])
        log_a = (-LRU_C) * r * sp_ref[:, cols]
        a = jnp.exp(log_a)
        gated = jnp.sqrt(-jnp.tanh(log_a) * (a * a + 1.0)) * (ig * x2[:, cols])
        half, j = hb // 4, hb % 4
        for b in range(nb):
            rows = pl.ds((2 * b + half) * pitch, tc)
            a_ref[j, rows, :] = a[b * tc:(b + 1) * tc]
            h_ref[j, rows, :] = gated[b * tc:(b + 1) * tc]
        yield _STEP

    yield _PHASE

    def body(t, carry):
        new = []
        for j in range(4):
            idx = pl.ds(t, SUBLANES, stride=pitch)
            h = a_ref[j, idx, :] * carry[j] + h_ref[j, idx, :]
            h_ref[j, idx, :] = h
            new.append(h)
        return tuple(new)

    init = tuple(st_ref[:, j * 128:(j + 1) * 128] for j in range(4))
    fin = lax.fori_loop(0, tc, body, init, unroll=8)
    for j in range(4):
        st_ref[:, j * 128:(j + 1) * 128] = fin[j]

    yield _PHASE
    for b in range(nb):
        parts = [h_ref[j, pl.ds((2 * b + half) * pitch, tc), :] for half in range(2) for j in range(4)]
        hb_ = jnp.concatenate(parts, axis=-1)
        o_ref[b] = (_rms(hb_, g_ref[...]) * _silu(gate_ref[b].astype(F32))).astype(BF16)
        yield _STEP


_STEP, _PHASE = 0, 1
S5_N_IN, LRU_N_IN = 11, 9
S5_N_SCRATCH = 7


def _s5_lru_kernel(*refs, nb, tc, s5_pitch, lru_pitch):
    s5_in, refs = refs[:S5_N_IN], refs[S5_N_IN:]
    lru_in, refs = refs[:LRU_N_IN], refs[LRU_N_IN:]
    (o_s5, o_lru), refs = refs[:2], refs[2:]
    s5 = _s5_phases(*s5_in, o_s5, *refs[:S5_N_SCRATCH], nb=nb, tc=tc, pitch=s5_pitch)
    lru = _lru_phases(*lru_in, o_lru, *refs[S5_N_SCRATCH:], nb=nb, tc=tc, pitch=lru_pitch)
    _run_interleaved([s5, lru])


def _run_interleaved(gens):
    live = list(gens)
    while live:
        running = list(live)
        while running:
            for g in list(running):
                token = next(g, None)
                if token is None:
                    running.remove(g)
                    live.remove(g)
                elif token == _PHASE:
                    running.remove(g)


def _s5_lru(z3, layer, s5_params, lru_params, *, tc=128):
    nb, seq, _ = z3.shape
    assert 2 * nb == SUBLANES and len(s5_params) == S5_N_IN - 2 and len(lru_params) == LRU_N_IN - 2
    s5_pitch = tc + SUBLANES // 2
    lru_pitch = tc + SUBLANES
    full = functools.partial(_layer_spec, layer=layer)
    zblk = lambda c: pl.BlockSpec((nb, tc, BRANCH), lambda i: (0, i, c))
    oblk = pl.BlockSpec((nb, tc, BRANCH), lambda i: (0, i, 0))
    kern = functools.partial(_s5_lru_kernel, nb=nb, tc=tc, s5_pitch=s5_pitch, lru_pitch=lru_pitch)
    return pl.pallas_call(
        kern,
        grid=(seq // tc,),
        in_specs=([zblk(Z_S5_U), zblk(Z_S5_GATE)] + [full(a) for a in s5_params]
                  + [zblk(Z_LRU_U), zblk(Z_LRU_GATE)] + [full(a) for a in lru_params]),
        out_specs=[oblk, oblk],
        out_shape=[jax.ShapeDtypeStruct((nb, seq, BRANCH), BF16)] * 2,
        scratch_shapes=[pltpu.VMEM((16, SUBLANES * s5_pitch, LANES), F32),
                        pltpu.VMEM((16, SUBLANES * s5_pitch, LANES), F32),
                        pltpu.VMEM((SUBLANES, 2048), F32),
                        pltpu.VMEM((SUBLANES, 2048), F32),
                        pltpu.VMEM((nb * tc, BRANCH), F32),
                        pltpu.VMEM((nb, tc + SUBLANES, BRANCH), F32),
                        pltpu.VMEM((nb, tc + SUBLANES, LANES), F32),
                        pltpu.VMEM((nb, tc + SUBLANES, BRANCH), F32),
                        pltpu.VMEM((4, SUBLANES * lru_pitch, LANES), F32),
                        pltpu.VMEM((4, SUBLANES * lru_pitch, LANES), F32),
                        pltpu.VMEM((SUBLANES, 512), F32)],
        compiler_params=_cparams(1),
        name="s5_rglru",
    )(z3, z3, *s5_params, z3, z3, *lru_params)


def _gla_kernel(q_ref, k_ref, v_ref, glr_ref, gate_ref, wg_ref, bg_ref, og_ref, tril_ref, o_ref,
                st_ref, *, tg):
    @pl.when(pl.program_id(1) == 0)
    def _():
        st_ref[...] = jnp.zeros_like(st_ref)

    n_ch = tg // GLA_CHUNK
    x = jnp.dot(glr_ref[...], wg_ref[...], preferred_element_type=F32) + bg_ref[...]
    log_a = (jnp.minimum(x, 0.0) - jnp.log1p(jnp.exp(-jnp.abs(x)))) / GLA_TAU

    tril = tril_ref[...]
    p0 = log_a.astype(BF16)
    r0 = log_a - p0.astype(F32)
    p1 = r0.astype(BF16)
    p2 = (r0 - p1.astype(F32)).astype(BF16)
    cum = (jnp.dot(tril, p0, preferred_element_type=F32)
           + jnp.dot(tril, p1, preferred_element_type=F32)
           + jnp.dot(tril, p2, preferred_element_type=F32))

    row = lax.broadcasted_iota(jnp.int32, (tg, tg), 0)
    col = lax.broadcasted_iota(jnp.int32, (tg, tg), 1)
    causal = (col <= row) & ((row // GLA_CHUNK) == (col // GLA_CHUNK))
    nt = (((1,), (1,)), ((), ()))
    tn = (((0,), (0,)), ((), ()))

    for hh in range(GLA_HEADS):
        kc = slice(hh * GLA_DK, (hh + 1) * GLA_DK)
        vc = slice(hh * GLA_DV, (hh + 1) * GLA_DV)
        cumh = cum[:, kc]
        kh = k_ref[:, kc].astype(F32)
        qd = (q_ref[:, kc].astype(F32) * (GLA_DK ** -0.5)) * jnp.exp(cumh)
        ki = kh * jnp.exp(-cumh)
        qdb = qd.astype(BF16)
        att = lax.dot_general(qdb, ki.astype(BF16), nt, preferred_element_type=F32)
        att = jnp.where(causal, att, 0.0)
        vb = v_ref[:, vc]
        o = jnp.dot(att.astype(BF16), vb, preferred_element_type=F32)
        st = st_ref[hh]
        inter = []
        for c in range(n_ch):
            rows = slice(c * GLA_CHUNK, (c + 1) * GLA_CHUNK)
            tot = cumh[(c + 1) * GLA_CHUNK - 1:(c + 1) * GLA_CHUNK, :]
            kd = kh[rows] * jnp.exp(tot - cumh[rows])
            inter.append(lax.dot_general(qdb[rows], st.astype(BF16), nt, preferred_element_type=F32))
            kv_t = lax.dot_general(vb[rows], kd.astype(BF16), tn, preferred_element_type=F32)
            st = st * jnp.exp(tot) + kv_t
        st_ref[hh] = st
        o = o + jnp.concatenate(inter, axis=0)
        gate = gate_ref[:, vc].astype(F32)
        o_ref[:, vc] = (_rms(o, og_ref[:, vc]) * _silu(gate)).astype(BF16)


def _gla(z3, layer, wg, bg, og, tril, *, tg=256):
    nb, seq, _ = z3.shape
    kern = functools.partial(_gla_kernel, tg=tg)
    full = functools.partial(_layer_spec, layer=layer)
    return pl.pallas_call(
        kern,
        grid=(nb, seq // tg),
        in_specs=[pl.BlockSpec((None, tg, 512), lambda b, i: (b, i, Z_GLA_Q_512)),
                  pl.BlockSpec((None, tg, 512), lambda b, i: (b, i, Z_GLA_K_512)),
                  pl.BlockSpec((None, tg, BRANCH), lambda b, i: (b, i, Z_GLA_V)),
                  pl.BlockSpec((None, tg, 128), lambda b, i: (b, i, Z_GLR_128)),
                  pl.BlockSpec((None, tg, BRANCH), lambda b, i: (b, i, Z_GLA_GATE)),
                  full(wg), full(bg), full(og), _whole_spec(tril)],
        out_specs=pl.BlockSpec((None, tg, BRANCH), lambda b, i: (b, i, 0)),
        out_shape=jax.ShapeDtypeStruct((nb, seq, BRANCH), BF16),
        scratch_shapes=[pltpu.VMEM((GLA_HEADS, GLA_DV, GLA_DK), F32)],
        compiler_params=_cparams(2),
        name="gla",
    )(z3, z3, z3, z3, z3, wg, bg, og, tril)


def _mla_proj_kernel(cq_ref, ckv_ref, kr_ref, pos_ref, gq_ref, gkv_ref, wq_ref, wk_ref, wvt_ref,
                     invf_ref, sgn_ref, msk_ref, q_out, k_out, vt_out):
    cqn = _rms(cq_ref[...].astype(F32), gq_ref[...]).astype(BF16)
    qa = jnp.dot(cqn, wq_ref[...], preferred_element_type=F32)
    ang = pos_ref[...].astype(F32) * invf_ref[...]
    cosm = jnp.cos(ang) * msk_ref[...]
    sinm = jnp.sin(ang) * sgn_ref[...]
    nh = MLA_HEADS * 128
    half = MLA_ROPE // 2
    lo_half = lax.broadcasted_iota(jnp.int32, (1, 128), 1) < half
    qscale = MLA_SCALE * LOG2_E
    for h in range(MLA_HEADS):
        q_out[:, h * 256:h * 256 + 128] = (qa[:, h * 128:(h + 1) * 128] * qscale).astype(BF16)
        qr = qa[:, nh + h * 128:nh + (h + 1) * 128]
        swapped = jnp.where(lo_half, pltpu.roll(qr, 128 - half, 1), pltpu.roll(qr, half, 1))
        q_out[:, h * 256 + 128:(h + 1) * 256] = ((qr * cosm + swapped * sinm) * qscale).astype(BF16)
    c = _rms(ckv_ref[...].astype(F32), gkv_ref[...]).astype(BF16)
    kn = jnp.dot(c, wk_ref[...], preferred_element_type=F32)
    kr = kr_ref[...].astype(F32)
    krp = (kr * cosm + pltpu.roll(kr, 64, 1) * sinm).astype(BF16)
    for h in range(MLA_HEADS):
        k_out[:, h * 256:h * 256 + 128] = kn[:, h * 128:(h + 1) * 128].astype(BF16)
        k_out[:, h * 256 + 128:(h + 1) * 256] = krp
    nt = (((1,), (1,)), ((), ()))
    vt_out[...] = lax.dot_general(wvt_ref[...], c, nt, preferred_element_type=F32).astype(BF16)


def _mla_proj(z2, pos2, layer, gq, gkv, wq, wk, wvt, invf, sgn, msk, *, tm):
    t = z2.shape[0]
    full = functools.partial(_layer_spec, layer=layer)
    return pl.pallas_call(
        _mla_proj_kernel,
        grid=(t // tm,),
        in_specs=[pl.BlockSpec((tm, MLA_Q_RANK), lambda i: (i, Z_CQ_768)),
                  pl.BlockSpec((tm, MLA_KV_RANK), lambda i: (i, Z_CKV_512)),
                  pl.BlockSpec((tm, 128), lambda i: (i, Z_KR_128)),
                  pl.BlockSpec((tm, 1), lambda i: (i, 0)),
                  full(gq), full(gkv), full(wq), full(wk), full(wvt),
                  _whole_spec(invf), _whole_spec(sgn), _whole_spec(msk)],
        out_specs=[pl.BlockSpec((tm, 2048), lambda i: (i, 0)),
                   pl.BlockSpec((tm, 2048), lambda i: (i, 0)),
                   pl.BlockSpec((None, BRANCH, tm), lambda i: (i, 0, 0))],
        out_shape=[jax.ShapeDtypeStruct((t, 2048), BF16),
                   jax.ShapeDtypeStruct((t, 2048), BF16),
                   jax.ShapeDtypeStruct((t // tm, BRANCH, tm), BF16)],
        compiler_params=_cparams(1),
        name="mla_proj",
    )(z2, z2, z2, pos2, gq, gkv, wq, wk, wvt, invf, sgn, msk)


def _mla_attn_kernel(q_ref, k_ref, vt_ref, gate_ref, g_ref, o_ref, m_ref, l_ref, acc_ref, o_scr, *, tq):
    qi = pl.program_id(1)
    nt = (((1,), (1,)), ((), ()))
    key = lax.broadcasted_iota(jnp.int32, (tq, tq), 0)
    qry = lax.broadcasted_iota(jnp.int32, (tq, tq), 1)

    m_ref[...] = jnp.full(m_ref.shape, NEG_INF, F32)
    l_ref[...] = jnp.zeros_like(l_ref)
    acc_ref[...] = jnp.zeros_like(acc_ref)

    def step(j, masked):
        r0 = pl.multiple_of(j * tq, tq)

        def scores(h):
            st = lax.dot_general(k_ref[pl.ds(r0, tq), h * 256:(h + 1) * 256],
                                 q_ref[:, h * 256:(h + 1) * 256], nt, preferred_element_type=F32)
            if masked:
                st = jnp.where(key <= qry, st, NEG_INF)
            m_old = m_ref[h]
            return st, m_old, jnp.maximum(m_old, jnp.max(st, axis=0, keepdims=True))

        nxt = scores(0)
        for h in range(MLA_HEADS):
            st, m_old, m_new = nxt
            if h + 1 < MLA_HEADS:
                nxt = scores(h + 1)
            alpha = jnp.exp2(m_old - m_new)
            pt = jnp.exp2(st - m_new)
            l_ref[h] = alpha * l_ref[h] + jnp.sum(pt, axis=0, keepdims=True)
            acc_ref[h] = alpha * acc_ref[h] + jnp.dot(
                vt_ref[j, h * 128:(h + 1) * 128, :], pt.astype(BF16), preferred_element_type=F32)
            m_ref[h] = m_new

    def body(j, c):
        step(j, False)
        return c

    lax.fori_loop(0, qi, body, 0)
    step(qi, True)

    for h in range(MLA_HEADS):
        o_scr[:, h * 128:(h + 1) * 128] = (acc_ref[h] / l_ref[h]).T
    o_ref[...] = (_rms(o_scr[...], g_ref[...]) * _silu(gate_ref[...].astype(F32))).astype(BF16)


def _mla_attn(q3, k3, vt4, z3, layer, g, *, tq):
    nb, seq, _ = q3.shape
    kern = functools.partial(_mla_attn_kernel, tq=tq)
    return pl.pallas_call(
        kern,
        grid=(nb, seq // tq),
        in_specs=[pl.BlockSpec((None, tq, 2048), lambda b, i: (b, i, 0)),
                  pl.BlockSpec((None, seq, 2048), lambda b, i: (b, 0, 0)),
                  pl.BlockSpec((None, seq // tq, BRANCH, tq), lambda b, i: (b, 0, 0, 0)),
                  pl.BlockSpec((None, tq, BRANCH), lambda b, i: (b, i, Z_MLA_GATE)),
                  _layer_spec(g, layer)],
        out_specs=pl.BlockSpec((None, tq, BRANCH), lambda b, i: (b, i, 0)),
        out_shape=jax.ShapeDtypeStruct((nb, seq, BRANCH), BF16),
        scratch_shapes=[pltpu.VMEM((MLA_HEADS, 1, tq), F32),
                        pltpu.VMEM((MLA_HEADS, 1, tq), F32),
                        pltpu.VMEM((MLA_HEADS, 128, tq), F32),
                        pltpu.VMEM((tq, BRANCH), F32)],
        compiler_params=_cparams(2),
        name="mla_attn",
    )(q3, k3, vt4, z3, g)


def _outproj_kernel(y0_ref, y1_ref, y2_ref, y3_ref, w_ref, x_ref, o_ref, wb_ref):
    first = pl.program_id(1) == 0
    ys = (y0_ref, y1_ref, y2_ref, y3_ref)

    @pl.when(first)
    def _():
        acc = x_ref[...]
        for n, y_ref in enumerate(ys):
            wb = w_ref[n * BRANCH:(n + 1) * BRANCH, :].astype(BF16)
            wb_ref[n * BRANCH:(n + 1) * BRANCH, :] = wb
            acc = acc + jnp.dot(y_ref[...], wb, preferred_element_type=F32)
        o_ref[...] = acc

    @pl.when(jnp.logical_not(first))
    def _():
        acc = x_ref[...]
        for n, y_ref in enumerate(ys):
            acc = acc + jnp.dot(y_ref[...], wb_ref[n * BRANCH:(n + 1) * BRANCH, :], preferred_element_type=F32)
        o_ref[...] = acc


def _outproj(ys, w_all, layer, x2, *, tm=1024, tn=512):
    t, d = x2.shape
    yspec = pl.BlockSpec((tm, BRANCH), lambda j, i: (i, 0))
    return pl.pallas_call(
        _outproj_kernel,
        grid=(d // tn, t // tm),
        in_specs=[yspec, yspec, yspec, yspec,
                  pl.BlockSpec((None, 4 * BRANCH, tn), lambda j, i: (layer, 0, j)),
                  pl.BlockSpec((tm, tn), lambda j, i: (i, j))],
        out_specs=pl.BlockSpec((tm, tn), lambda j, i: (i, j)),
        out_shape=jax.ShapeDtypeStruct((t, d), F32),
        scratch_shapes=[pltpu.VMEM((4 * BRANCH, tn), BF16)],
        compiler_params=_cparams(2),
        name="outproj",
    )(*ys, w_all, x2)


def _final_norm_kernel(x_ref, g_ref, o_ref):
    o_ref[...] = _rms(x_ref[...], g_ref[...])


def _final_norm(x2, g, *, tm=512):
    t, d = x2.shape
    return pl.pallas_call(
        _final_norm_kernel,
        grid=(t // tm,),
        in_specs=[pl.BlockSpec((tm, d), lambda i: (i, 0)), pl.BlockSpec((1, d), lambda i: (0, 0))],
        out_specs=pl.BlockSpec((tm, d), lambda i: (i, 0)),
        out_shape=jax.ShapeDtypeStruct((t, d), F32),
        compiler_params=_cparams(1),
        name="final_norm",
    )(x2, g)


def _mla_q_weight(w_uq):
    w = w_uq.reshape(MLA_Q_RANK, MLA_HEADS, MLA_NOPE + MLA_ROPE)
    nope = w[:, :, :MLA_NOPE].reshape(MLA_Q_RANK, MLA_HEADS * MLA_NOPE)
    pad = ((0, 0), (0, 0), (0, 128 - MLA_ROPE))
    rope_p = jnp.pad(w[:, :, MLA_NOPE:], pad).reshape(MLA_Q_RANK, MLA_HEADS * 128)
    return jnp.concatenate([nope, rope_p], axis=1).astype(BF16)


def _rope_tables():
    half = MLA_ROPE // 2
    inv = 1.0 / (ROPE_THETA ** (jnp.arange(half, dtype=F32) / half))
    zeros = jnp.zeros((128 - MLA_ROPE,), F32)
    invf = jnp.concatenate([inv, inv, zeros]).reshape(1, 128)
    sgn = jnp.concatenate([-jnp.ones((half,), F32), jnp.ones((half,), F32), zeros]).reshape(1, 128)
    msk = jnp.concatenate([jnp.ones((MLA_ROPE,), F32), zeros]).reshape(1, 128)
    return invf, sgn, msk


def _gla_tril(tg):
    r = np.arange(tg)
    m = (r[None, :] <= r[:, None]) & ((r[:, None] // GLA_CHUNK) == (r[None, :] // GLA_CHUNK))
    return jnp.asarray(m, dtype=BF16)


def kernel(x, positions, norm_g, w_in, s5_lambda_re, s5_lambda_im, s5_b_re, s5_b_im, s5_c_re, s5_c_im,
           s5_d, s5_log_dt, s5_w_glu, s5_b_glu, s5_out_g, mla_q_norm_g, mla_kv_norm_g, mla_w_uq,
           mla_w_uk, mla_w_uv, mla_out_g, gla_w_gate, gla_b_gate, gla_out_g, lru_conv_w, lru_conv_b,
           lru_w_a, lru_b_a, lru_w_x, lru_b_x, lru_lambda, lru_out_g, w_out, final_g):
    nb, seq, d = x.shape
    depth = w_in.shape[0]
    t = nb * seq
    x2 = x.reshape(t, d)
    pos2 = positions.reshape(t, 1)
    invf, sgn, msk = _rope_tables()
    gla_tg = 256
    tril = _gla_tril(gla_tg)

    rows = lambda v: v.reshape(depth, 1, -1).astype(F32)
    w_in_t = _w_in_t(w_in)
    norm_g_r = rows(norm_g)
    s5_wb, s5_lamr, s5_lami, s5_wcr, s5_wci = jax.vmap(functools.partial(_s5_params, nb=nb))(
        s5_lambda_re, s5_lambda_im, s5_b_re, s5_b_im, s5_c_re, s5_c_im, s5_log_dt)
    s5_rest = (rows(s5_d), s5_w_glu.astype(BF16), rows(s5_b_glu), rows(s5_out_g))
    mla_w = (rows(mla_q_norm_g), rows(mla_kv_norm_g), jax.vmap(_mla_q_weight)(mla_w_uq),
             mla_w_uk.astype(BF16), jnp.swapaxes(mla_w_uv, 1, 2).astype(BF16))
    mla_out_g_r = rows(mla_out_g)
    gla_w = (jnp.pad(gla_w_gate, ((0, 0), (0, 128 - GLA_GATE_RANK), (0, 0))).astype(BF16),
             rows(gla_b_gate), rows(gla_out_g))
    lru_w = (lru_conv_w.astype(F32), rows(lru_conv_b),
             jnp.concatenate([lru_w_a, lru_w_x], axis=-1).astype(BF16),
             rows(lru_b_a), rows(lru_b_x), rows(jax.nn.softplus(-lru_lambda.astype(F32))),
             rows(lru_out_g))

    for l in range(depth):
        z2 = _inproj(x2, norm_g_r, w_in_t, l)
        z3 = z2.reshape(nb, seq, Z_COLS)
        y_s5, y_lru = _s5_lru(z3, l, (s5_wb, s5_lamr, s5_lami, s5_wcr, s5_wci) + s5_rest, lru_w)
        q2, k2, vt3 = _mla_proj(z2, pos2, l, *mla_w, invf, sgn, msk, tm=MLA_TQ)
        y_mla = _mla_attn(q2.reshape(nb, seq, 2048), k2.reshape(nb, seq, 2048),
                          vt3.reshape(nb, seq // MLA_TQ, BRANCH, MLA_TQ), z3, l, mla_out_g_r, tq=MLA_TQ)
        y_gla = _gla(z3, l, *gla_w, tril, tg=gla_tg)
        ys = [y.reshape(t, BRANCH) for y in (y_s5, y_mla, y_gla, y_lru)]
        x2 = _outproj(ys, w_out, l, x2)

    return _final_norm(x2, final_g.reshape(1, d).astype(F32)).reshape(nb, seq, d)
```

```python
import functools

import jax
import jax.numpy as jnp
import numpy as np
from jax import lax
from jax.experimental import pallas as pl
from jax.experimental.pallas import tpu as pltpu

F32 = jnp.float32
BF16 = jnp.bfloat16

D_MODEL = 4096
BRANCH = 1024
NORM_EPS = 1e-6

S5_GROUP = 16
S5_GROUPS = 64
S5_STATE = 64

MLA_NOPE = 128
MLA_ROPE = 64
MLA_HEADS = 8
MLA_Q_RANK = 768
MLA_KV_RANK = 512
ROPE_THETA = 10000.0
MLA_SCALE = (MLA_NOPE + MLA_ROPE) ** -0.5
NEG_INF = -1e30
LOG2_E = 1.4426950408889634
MLA_TQ = 512
GLA_HEADS = 4
GLA_DV = 256
GLA_DK = 128
GLA_GATE_RANK = 16
GLA_TAU = 16.0
GLA_CHUNK = 64

LRU_BLOCKS = 8
LRU_BLOCK_W = 128
LRU_CONV = 4
LRU_C = 8.0

LANES = 128
SUBLANES = 8

Z_S5_U, Z_S5_GATE, Z_MLA_GATE, Z_GLA_V, Z_GLA_GATE, Z_LRU_U, Z_LRU_GATE = range(7)
Z_CKV_512 = 14
Z_CQ_768 = 10
Z_KR_128 = 66
Z_GLR_128 = 67
Z_GLA_Q_512 = 17
Z_GLA_K_512 = 18
Z_COLS = 9728

VMEM_LIMIT = 56 * 1024 * 1024
NORM_ROWS = 256
S5_SCAN_W = 8


def _cparams(n_axes):
    return pltpu.CompilerParams(dimension_semantics=("arbitrary",) * n_axes,
                                vmem_limit_bytes=VMEM_LIMIT)


def _layer_spec(a, layer):
    zeros = (0,) * (a.ndim - 1)
    return pl.BlockSpec((None,) + tuple(a.shape[1:]), lambda *_: (layer,) + zeros)


def _whole_spec(a):
    return pl.BlockSpec(a.shape, lambda *_: (0,) * a.ndim)


def _rms(x, g):
    return x * lax.rsqrt(jnp.mean(x * x, axis=-1, keepdims=True) + NORM_EPS) * g


def _silu(x):
    return x * jax.nn.sigmoid(x)


W_IN_TN = 512
W_IN_EXTRA = 16
_W_IN_BLOCK_ROWS = (0, 512, 1024, 1536,
                    3392, 3904,
                    5440, 5952, 6480, 6992,
                    7504, 8016, 8528, 9040,
                    2816, 2048,
                    2048,
                    4416, 4928)
INPROJ_TAIL = 512
INPROJ_TM = 1536


def _norm_rows(x_ref, g_ref, h_ref):
    for r in range(x_ref.shape[0] // NORM_ROWS):
        rows = slice(r * NORM_ROWS, (r + 1) * NORM_ROWS)
        h_ref[rows, :] = _rms(x_ref[rows, :], g_ref[...]).astype(BF16)


def _inproj_cast_kernel(rows_ref, x_ref, g_ref, w_ref, wx_ref, o_ref, wb_ref, h_ref):
    del rows_ref
    nt = (((1,), (1,)), ((), ()))
    j = pl.program_id(0)

    @pl.when(j == 0)
    def _():
        _norm_rows(x_ref, g_ref, h_ref)

    def project(w_f32):
        wb = w_f32.astype(BF16)
        wb_ref[...] = wb
        o_ref[...] = lax.dot_general(h_ref[...], wb, nt, preferred_element_type=F32).astype(o_ref.dtype)

    @pl.when(j == W_IN_EXTRA)
    def _():
        project(wx_ref[...])

    @pl.when(j != W_IN_EXTRA)
    def _():
        project(w_ref[0])


def _inproj_kernel(x_ref, g_ref, w_ref, z_ref, o_ref, h_ref):
    del z_ref
    nt = (((1,), (1,)), ((), ()))
    first = pl.program_id(1) == 0

    @pl.when(first)
    def _():
        for r in range(x_ref.shape[0] // NORM_ROWS):
            rows = slice(r * NORM_ROWS, (r + 1) * NORM_ROWS)
            h = _rms(x_ref[rows, :], g_ref[...]).astype(BF16)
            h_ref[rows, :] = h
            o_ref[rows, :] = lax.dot_general(h, w_ref[...], nt, preferred_element_type=F32).astype(o_ref.dtype)

    @pl.when(jnp.logical_not(first))
    def _():
        o_ref[...] = lax.dot_general(h_ref[...], w_ref[...], nt,
                                     preferred_element_type=F32).astype(o_ref.dtype)


def _inproj(x2, g, w_t, w_extra, layer):
    t, d = x2.shape
    tn, nblk = W_IN_TN, len(_W_IN_BLOCK_ROWS)
    assert (t - INPROJ_TAIL) % INPROJ_TM == 0 and INPROJ_TM % INPROJ_TAIL == 0
    rows = jnp.asarray(_W_IN_BLOCK_ROWS, jnp.int32)
    tail_blk = t // INPROJ_TAIL - 1
    once = pl.Buffered(1)
    z_tail, w_bf = pl.pallas_call(
        _inproj_cast_kernel,
        grid_spec=pltpu.PrefetchScalarGridSpec(
            num_scalar_prefetch=1,
            grid=(nblk,),
            in_specs=[pl.BlockSpec((INPROJ_TAIL, d), lambda j, r: (tail_blk, 0), pipeline_mode=once),
                      _layer_spec(g, layer),
                      pl.BlockSpec((pl.Element(1), pl.Element(tn), pl.Element(d)),
                                   lambda j, r: (layer, pl.multiple_of(r[j], 16), 0)),
                      pl.BlockSpec((None, tn, d), lambda j, r: (layer, 0, 0), pipeline_mode=once)],
            out_specs=[pl.BlockSpec((INPROJ_TAIL, tn), lambda j, r: (tail_blk, j)),
                       pl.BlockSpec((None, tn, d), lambda j, r: (j, 0, 0))],
            scratch_shapes=[pltpu.VMEM((INPROJ_TAIL, d), BF16)]),
        out_shape=[jax.ShapeDtypeStruct((t, nblk * tn), BF16),
                   jax.ShapeDtypeStruct((nblk, tn, d), BF16)],
        compiler_params=_cparams(1),
        name="inproj_cast",
    )(rows, x2, g, w_t, w_extra)
    return pl.pallas_call(
        _inproj_kernel,
        grid=((t - INPROJ_TAIL) // INPROJ_TM, nblk),
        in_specs=[pl.BlockSpec((INPROJ_TM, d), lambda i, j: (i, 0), pipeline_mode=once),
                  _layer_spec(g, layer),
                  pl.BlockSpec((None, tn, d), lambda i, j: (j, 0, 0)),
                  pl.BlockSpec(memory_space=pl.ANY)],
        out_specs=pl.BlockSpec((INPROJ_TM, tn), lambda i, j: (i, j)),
        out_shape=jax.ShapeDtypeStruct((t, nblk * tn), BF16),
        scratch_shapes=[pltpu.VMEM((INPROJ_TM, d), BF16)],
        input_output_aliases={3: 0},
        compiler_params=_cparams(2),
        name="inproj",
    )(x2, g, w_bf, z_tail)


def _w_in_extra(w_t):
    depth, _, d = w_t.shape
    parts = [w_t[:, 2560:2816], w_t[:, 3328:3392], w_t[:, 3360:3392], w_t[:, 3328:3360],
             w_t[:, 6464:6480], jnp.zeros((depth, 128 - GLA_GATE_RANK, d), w_t.dtype)]
    return jnp.concatenate(parts, axis=1)


def _s5_phases(u_ref, gate_ref, wb_ref, lamr_ref, lami_ref, wcr_ref, wci_ref, d_ref,
               wglu_ref, bglu_ref, g_ref, o_ref,
               bre_ref, bim_ref, hre_ref, him_ref, y_ref, ush_ref, ysh_ref, *, nb, tc, pitch):
    shift = pitch - tc
    tcs = tc + 2 * shift

    @pl.when(pl.program_id(0) == 0)
    def _():
        hre_ref[...] = jnp.zeros_like(hre_ref)
        him_ref[...] = jnp.zeros_like(him_ref)
        ush_ref[...] = jnp.zeros_like(ush_ref)

    yield _PHASE
    ub = u_ref[...].reshape(nb * tc, BRANCH)
    ush_ref[:, shift:shift + tc, :] = u_ref[...].astype(F32)
    ubs = ush_ref[...].reshape(nb * tcs, BRANCH).astype(BF16)

    def block_rows(b, half):
        if half:
            return pl.ds((2 * b + 1) * pitch - shift, tcs)
        return pl.ds(2 * b * pitch, tc)

    for kb in range(8):
        half = kb // 4
        lhs, m = (ubs, tcs) if half else (ub, tc)
        res = jnp.dot(lhs[:, kb * 128:(kb + 1) * 128], wb_ref[kb], preferred_element_type=F32)
        for b in range(nb):
            rows = block_rows(b, half)
            for jj in range(4):
                j = (kb % 4) * 4 + jj
                bre_ref[j, rows, :] = res[b * m:(b + 1) * m, jj * 128:(jj + 1) * 128]
                bim_ref[j, rows, :] = res[b * m:(b + 1) * m, 512 + jj * 128:512 + (jj + 1) * 128]
        yield _STEP

    yield _PHASE
    for cb in range(16 // S5_SCAN_W):
        slabs = [cb * S5_SCAN_W + jj for jj in range(S5_SCAN_W)]
        lr = [lamr_ref[:, j * 128:(j + 1) * 128] for j in slabs]
        li = [lami_ref[:, j * 128:(j + 1) * 128] for j in slabs]
        init = []
        for j in slabs:
            init.append(hre_ref[:, j * 128:(j + 1) * 128])
            init.append(him_ref[:, j * 128:(j + 1) * 128])

        def body(t, carry, slabs=slabs, lr=lr, li=li):
            new = []
            idx = pl.ds(t, SUBLANES, stride=pitch)
            for jj, j in enumerate(slabs):
                hr, hi = carry[2 * jj], carry[2 * jj + 1]
                nr = lr[jj] * hr - li[jj] * hi + bre_ref[j, idx, :]
                ni = lr[jj] * hi + li[jj] * hr + bim_ref[j, idx, :]
                bre_ref[j, idx, :] = nr
                bim_ref[j, idx, :] = ni
                new += [nr, ni]
            return tuple(new)

        fin = lax.fori_loop(0, tc, body, tuple(init), unroll=2)
        for jj, j in enumerate(slabs):
            hre_ref[:, j * 128:(j + 1) * 128] = fin[2 * jj]
            him_ref[:, j * 128:(j + 1) * 128] = fin[2 * jj + 1]

    yield _PHASE
    for kb in range(8):
        half = kb // 4
        sre, sim = [], []
        for b in range(nb):
            rows = block_rows(b, half)
            sre.append(jnp.concatenate([bre_ref[(kb % 4) * 4 + jj, rows, :] for jj in range(4)], axis=-1))
            sim.append(jnp.concatenate([bim_ref[(kb % 4) * 4 + jj, rows, :] for jj in range(4)], axis=-1))
        sre = jnp.concatenate(sre, axis=0).astype(BF16)
        sim = jnp.concatenate(sim, axis=0).astype(BF16)
        ykb = (jnp.dot(sre, wcr_ref[kb], preferred_element_type=F32)
               + jnp.dot(sim, wci_ref[kb], preferred_element_type=F32))
        if half:
            ysh_ref[...] = ykb.reshape(nb, tcs, 128)
            ykb = ysh_ref[:, shift:shift + tc, :].reshape(nb * tc, 128)
        cols = slice(kb * 128, (kb + 1) * 128)
        y_ref[:, cols] = ykb + d_ref[:, cols] * ub[:, cols].astype(F32)
        yield _STEP

    y = jax.nn.gelu(y_ref[...], approximate=True)
    zg = jnp.dot(y.astype(BF16), wglu_ref[...], preferred_element_type=F32) + bglu_ref[...]
    y = y * jax.nn.sigmoid(zg)
    gate = gate_ref[...].reshape(nb * tc, BRANCH).astype(F32)
    o_ref[...] = (_rms(y, g_ref[...]) * _silu(gate)).astype(BF16).reshape(nb, tc, BRANCH)


def _s5_params(lam_re, lam_im, b_re, b_im, c_re, c_im, log_dt, nb):
    dt = jnp.exp(log_dt.astype(F32))[:, None]
    lr, li = lam_re.astype(F32), lam_im.astype(F32)
    mag = jnp.exp(lr * dt)
    lbr, lbi = mag * jnp.cos(li * dt), mag * jnp.sin(li * dt)
    den = lr * lr + li * li
    cr = ((lbr - 1.0) * lr + lbi * li) / den
    ci = (lbi * lr - (lbr - 1.0) * li) / den
    bbr = cr[..., None] * b_re - ci[..., None] * b_im
    bbi = cr[..., None] * b_im + ci[..., None] * b_re
    eye = jnp.eye(8, dtype=F32)

    def in_layout(m):
        m = m.reshape(8, 8, S5_STATE, S5_GROUP)
        return jnp.einsum('kgph,gG->kghGp', m, eye).reshape(8, 128, 512)

    def out_layout(m):
        m = m.reshape(8, 8, S5_GROUP, S5_STATE)
        return jnp.einsum('kghp,gG->kgpGh', m, eye).reshape(8, 512, 128)

    wb = jnp.concatenate([in_layout(bbr), in_layout(bbi)], axis=-1).astype(BF16)
    wcr = out_layout(c_re.astype(F32)).astype(BF16)
    wci = out_layout(-c_im.astype(F32)).astype(BF16)

    def seq_layout(v):
        return jnp.tile(v.reshape(1, 2, 2048), (nb, 1, 1)).reshape(2 * nb, 2048)

    return wb, seq_layout(lbr), seq_layout(lbi), wcr, wci


def _lru_phases(u_ref, gate_ref, cw_ref, cb_ref, wax_ref, ba_ref, bx_ref, sp_ref, g_ref, o_ref,
                xpad_ref, a_ref, h_ref, st_ref, *, nb, tc, pitch):
    @pl.when(pl.program_id(0) == 0)
    def _():
        xpad_ref[:, 0:SUBLANES, :] = jnp.zeros((nb, SUBLANES, BRANCH), F32)
        st_ref[...] = jnp.zeros_like(st_ref)

    yield _PHASE
    xpad_ref[:, SUBLANES:SUBLANES + tc, :] = u_ref[...].astype(F32)
    xc = cb_ref[...].reshape(1, 1, BRANCH)
    for k in range(LRU_CONV):
        off = SUBLANES - (LRU_CONV - 1) + k
        xc = xc + cw_ref[k:k + 1, :].reshape(1, 1, BRANCH) * xpad_ref[:, off:off + tc, :]
    xpad_ref[:, 0:SUBLANES, :] = xpad_ref[:, tc:tc + SUBLANES, :]
    x2 = xc.reshape(nb * tc, BRANCH)
    xb = x2.astype(BF16)
    yield _STEP

    for hb in range(LRU_BLOCKS):
        cols = slice(hb * 128, (hb + 1) * 128)
        ri = jnp.dot(xb[:, cols], wax_ref[hb], preferred_element_type=F32)
        r = jax.nn.sigmoid(ri[:, :128] + ba_ref[:, cols])
        ig = jax.nn.sigmoid(ri[:, 128:] + bx_ref[:, cols])
        log_a = (-LRU_C) * r * sp_ref[:, cols]
        a = jnp.exp(log_a)
        gated = jnp.sqrt(-jnp.tanh(log_a) * (a * a + 1.0)) * (ig * x2[:, cols])
        half, j = hb // 4, hb % 4
        for b in range(nb):
            rows = pl.ds((2 * b + half) * pitch, tc)
            a_ref[j, rows, :] = a[b * tc:(b + 1) * tc]
            h_ref[j, rows, :] = gated[b * tc:(b + 1) * tc]
        yield _STEP

    yield _PHASE

    def body(t, carry):
        new = []
        for j in range(4):
            idx = pl.ds(t, SUBLANES, stride=pitch)
            h = a_ref[j, idx, :] * carry[j] + h_ref[j, idx, :]
            h_ref[j, idx, :] = h
            new.append(h)
        return tuple(new)

    init = tuple(st_ref[:, j * 128:(j + 1) * 128] for j in range(4))
    fin = lax.fori_loop(0, tc, body, init, unroll=8)
    for j in range(4):
        st_ref[:, j * 128:(j + 1) * 128] = fin[j]

    yield _PHASE
    for b in range(nb):
        parts = [h_ref[j, pl.ds((2 * b + half) * pitch, tc), :] for half in range(2) for j in range(4)]
        hb_ = jnp.concatenate(parts, axis=-1)
        o_ref[b] = (_rms(hb_, g_ref[...]) * _silu(gate_ref[b].astype(F32))).astype(BF16)
        yield _STEP


_STEP, _PHASE = 0, 1
S5_N_IN, LRU_N_IN = 11, 9
S5_N_SCRATCH = 7


def _s5_lru_kernel(*refs, nb, tc, s5_pitch, lru_pitch):
    s5_in, refs = refs[:S5_N_IN], refs[S5_N_IN:]
    lru_in, refs = refs[:LRU_N_IN], refs[LRU_N_IN:]
    (o_s5, o_lru), refs = refs[:2], refs[2:]
    s5 = _s5_phases(*s5_in, o_s5, *refs[:S5_N_SCRATCH], nb=nb, tc=tc, pitch=s5_pitch)
    lru = _lru_phases(*lru_in, o_lru, *refs[S5_N_SCRATCH:], nb=nb, tc=tc, pitch=lru_pitch)
    _run_interleaved([s5, lru])


def _run_interleaved(gens):
    live = list(gens)
    while live:
        running = list(live)
        while running:
            for g in list(running):
                token = next(g, None)
                if token is None:
                    running.remove(g)
                    live.remove(g)
                elif token == _PHASE:
                    running.remove(g)


def _s5_lru(z3, layer, s5_params, lru_params, *, tc=128):
    nb, seq, _ = z3.shape
    assert 2 * nb == SUBLANES and len(s5_params) == S5_N_IN - 2 and len(lru_params) == LRU_N_IN - 2
    s5_pitch = tc + SUBLANES // 2
    lru_pitch = tc + SUBLANES
    full = functools.partial(_layer_spec, layer=layer)
    zblk = lambda c: pl.BlockSpec((nb, tc, BRANCH), lambda i: (0, i, c))
    oblk = pl.BlockSpec((nb, tc, BRANCH), lambda i: (0, i, 0))
    kern = functools.partial(_s5_lru_kernel, nb=nb, tc=tc, s5_pitch=s5_pitch, lru_pitch=lru_pitch)
    return pl.pallas_call(
        kern,
        grid=(seq // tc,),
        in_specs=([zblk(Z_S5_U), zblk(Z_S5_GATE)] + [full(a) for a in s5_params]
                  + [zblk(Z_LRU_U), zblk(Z_LRU_GATE)] + [full(a) for a in lru_params]),
        out_specs=[oblk, oblk],
        out_shape=[jax.ShapeDtypeStruct((nb, seq, BRANCH), BF16)] * 2,
        scratch_shapes=[pltpu.VMEM((16, SUBLANES * s5_pitch, LANES), F32),
                        pltpu.VMEM((16, SUBLANES * s5_pitch, LANES), F32),
                        pltpu.VMEM((SUBLANES, 2048), F32),
                        pltpu.VMEM((SUBLANES, 2048), F32),
                        pltpu.VMEM((nb * tc, BRANCH), F32),
                        pltpu.VMEM((nb, tc + SUBLANES, BRANCH), F32),
                        pltpu.VMEM((nb, tc + SUBLANES, LANES), F32),
                        pltpu.VMEM((nb, tc + SUBLANES, BRANCH), F32),
                        pltpu.VMEM((4, SUBLANES * lru_pitch, LANES), F32),
                        pltpu.VMEM((4, SUBLANES * lru_pitch, LANES), F32),
                        pltpu.VMEM((SUBLANES, 512), F32)],
        compiler_params=_cparams(1),
        name="s5_rglru",
    )(z3, z3, *s5_params, z3, z3, *lru_params)


def _gla_kernel(q_ref, k_ref, v_ref, glr_ref, gate_ref, wg_ref, bg_ref, og_ref, tril_ref, o_ref,
                st_ref, *, tg):
    @pl.when(pl.program_id(1) == 0)
    def _():
        st_ref[...] = jnp.zeros_like(st_ref)

    n_ch = tg // GLA_CHUNK
    x = jnp.dot(glr_ref[...], wg_ref[...], preferred_element_type=F32) + bg_ref[...]
    log_a = (jnp.minimum(x, 0.0) - jnp.log1p(jnp.exp(-jnp.abs(x)))) / GLA_TAU

    tril = tril_ref[...]
    p0 = log_a.astype(BF16)
    r0 = log_a - p0.astype(F32)
    p1 = r0.astype(BF16)
    p2 = (r0 - p1.astype(F32)).astype(BF16)
    cum = (jnp.dot(tril, p0, preferred_element_type=F32)
           + jnp.dot(tril, p1, preferred_element_type=F32)
           + jnp.dot(tril, p2, preferred_element_type=F32))

    row = lax.broadcasted_iota(jnp.int32, (tg, tg), 0)
    col = lax.broadcasted_iota(jnp.int32, (tg, tg), 1)
    causal = (col <= row) & ((row // GLA_CHUNK) == (col // GLA_CHUNK))
    nt = (((1,), (1,)), ((), ()))
    tn = (((0,), (0,)), ((), ()))

    for hh in range(GLA_HEADS):
        kc = slice(hh * GLA_DK, (hh + 1) * GLA_DK)
        vc = slice(hh * GLA_DV, (hh + 1) * GLA_DV)
        cumh = cum[:, kc]
        kh = k_ref[:, kc].astype(F32)
        qd = (q_ref[:, kc].astype(F32) * (GLA_DK ** -0.5)) * jnp.exp(cumh)
        ki = kh * jnp.exp(-cumh)
        qdb = qd.astype(BF16)
        att = lax.dot_general(qdb, ki.astype(BF16), nt, preferred_element_type=F32)
        att = jnp.where(causal, att, 0.0)
        vb = v_ref[:, vc]
        o = jnp.dot(att.astype(BF16), vb, preferred_element_type=F32)
        st = st_ref[hh]
        inter = []
        for c in range(n_ch):
            rows = slice(c * GLA_CHUNK, (c + 1) * GLA_CHUNK)
            tot = cumh[(c + 1) * GLA_CHUNK - 1:(c + 1) * GLA_CHUNK, :]
            kd = kh[rows] * jnp.exp(tot - cumh[rows])
            inter.append(lax.dot_general(qdb[rows], st.astype(BF16), nt, preferred_element_type=F32))
            kv_t = lax.dot_general(vb[rows], kd.astype(BF16), tn, preferred_element_type=F32)
            st = st * jnp.exp(tot) + kv_t
        st_ref[hh] = st
        o = o + jnp.concatenate(inter, axis=0)
        gate = gate_ref[:, vc].astype(F32)
        o_ref[:, vc] = (_rms(o, og_ref[:, vc]) * _silu(gate)).astype(BF16)


def _gla(z3, layer, wg, bg, og, tril, *, tg=256):
    nb, seq, _ = z3.shape
    kern = functools.partial(_gla_kernel, tg=tg)
    full = functools.partial(_layer_spec, layer=layer)
    return pl.pallas_call(
        kern,
        grid=(nb, seq // tg),
        in_specs=[pl.BlockSpec((None, tg, 512), lambda b, i: (b, i, Z_GLA_Q_512)),
                  pl.BlockSpec((None, tg, 512), lambda b, i: (b, i, Z_GLA_K_512)),
                  pl.BlockSpec((None, tg, BRANCH), lambda b, i: (b, i, Z_GLA_V)),
                  pl.BlockSpec((None, tg, 128), lambda b, i: (b, i, Z_GLR_128)),
                  pl.BlockSpec((None, tg, BRANCH), lambda b, i: (b, i, Z_GLA_GATE)),
                  full(wg), full(bg), full(og), _whole_spec(tril)],
        out_specs=pl.BlockSpec((None, tg, BRANCH), lambda b, i: (b, i, 0)),
        out_shape=jax.ShapeDtypeStruct((nb, seq, BRANCH), BF16),
        scratch_shapes=[pltpu.VMEM((GLA_HEADS, GLA_DV, GLA_DK), F32)],
        compiler_params=_cparams(2),
        name="gla",
    )(z3, z3, z3, z3, z3, wg, bg, og, tril)


def _mla_proj_kernel(cq_ref, ckv_ref, kr_ref, pos_ref, gq_ref, gkv_ref, wq_ref, wk_ref, wvt_ref,
                     invf_ref, sgn_ref, msk_ref, q_out, k_out, vt_out):
    cqn = _rms(cq_ref[...].astype(F32), gq_ref[...]).astype(BF16)
    qa = jnp.dot(cqn, wq_ref[...], preferred_element_type=F32)
    ang = pos_ref[...].astype(F32) * invf_ref[...]
    cosm = jnp.cos(ang) * msk_ref[...]
    sinm = jnp.sin(ang) * sgn_ref[...]
    nh = MLA_HEADS * 128
    half = MLA_ROPE // 2
    lo_half = lax.broadcasted_iota(jnp.int32, (1, 128), 1) < half
    qscale = MLA_SCALE * LOG2_E
    for h in range(MLA_HEADS):
        q_out[:, h * 256:h * 256 + 128] = (qa[:, h * 128:(h + 1) * 128] * qscale).astype(BF16)
        qr = qa[:, nh + h * 128:nh + (h + 1) * 128]
        swapped = jnp.where(lo_half, pltpu.roll(qr, 128 - half, 1), pltpu.roll(qr, half, 1))
        q_out[:, h * 256 + 128:(h + 1) * 256] = ((qr * cosm + swapped * sinm) * qscale).astype(BF16)
    c = _rms(ckv_ref[...].astype(F32), gkv_ref[...]).astype(BF16)
    kn = jnp.dot(c, wk_ref[...], preferred_element_type=F32)
    kr = kr_ref[...].astype(F32)
    krp = (kr * cosm + pltpu.roll(kr, 64, 1) * sinm).astype(BF16)
    for h in range(MLA_HEADS):
        k_out[:, h * 256:h * 256 + 128] = kn[:, h * 128:(h + 1) * 128].astype(BF16)
        k_out[:, h * 256 + 128:(h + 1) * 256] = krp
    nt = (((1,), (1,)), ((), ()))
    vt_out[...] = lax.dot_general(wvt_ref[...], c, nt, preferred_element_type=F32).astype(BF16)


def _mla_proj(z2, pos2, layer, gq, gkv, wq, wk, wvt, invf, sgn, msk, *, tm):
    t = z2.shape[0]
    full = functools.partial(_layer_spec, layer=layer)
    return pl.pallas_call(
        _mla_proj_kernel,
        grid=(t // tm,),
        in_specs=[pl.BlockSpec((tm, MLA_Q_RANK), lambda i: (i, Z_CQ_768)),
                  pl.BlockSpec((tm, MLA_KV_RANK), lambda i: (i, Z_CKV_512)),
                  pl.BlockSpec((tm, 128), lambda i: (i, Z_KR_128)),
                  pl.BlockSpec((tm, 1), lambda i: (i, 0)),
                  full(gq), full(gkv), full(wq), full(wk), full(wvt),
                  _whole_spec(invf), _whole_spec(sgn), _whole_spec(msk)],
        out_specs=[pl.BlockSpec((tm, 2048), lambda i: (i, 0)),
                   pl.BlockSpec((tm, 2048), lambda i: (i, 0)),
                   pl.BlockSpec((None, BRANCH, tm), lambda i: (i, 0, 0))],
        out_shape=[jax.ShapeDtypeStruct((t, 2048), BF16),
                   jax.ShapeDtypeStruct((t, 2048), BF16),
                   jax.ShapeDtypeStruct((t // tm, BRANCH, tm), BF16)],
        compiler_params=_cparams(1),
        name="mla_proj",
    )(z2, z2, z2, pos2, gq, gkv, wq, wk, wvt, invf, sgn, msk)


def _mla_attn_kernel(q_ref, k_ref, vt_ref, gate_ref, g_ref, o_ref, m_ref, l_ref, acc_ref, o_scr, *, tq):
    qi = pl.program_id(1)
    nt = (((1,), (1,)), ((), ()))
    key = lax.broadcasted_iota(jnp.int32, (tq, tq), 0)
    qry = lax.broadcasted_iota(jnp.int32, (tq, tq), 1)

    m_ref[...] = jnp.full(m_ref.shape, NEG_INF, F32)
    l_ref[...] = jnp.zeros_like(l_ref)
    acc_ref[...] = jnp.zeros_like(acc_ref)

    def step(j, masked):
        r0 = pl.multiple_of(j * tq, tq)

        def scores(h):
            st = lax.dot_general(k_ref[pl.ds(r0, tq), h * 256:(h + 1) * 256],
                                 q_ref[:, h * 256:(h + 1) * 256], nt, preferred_element_type=F32)
            if masked:
                st = jnp.where(key <= qry, st, NEG_INF)
            m_old = m_ref[h]
            return st, m_old, jnp.maximum(m_old, jnp.max(st, axis=0, keepdims=True))

        nxt = scores(0)
        for h in range(MLA_HEADS):
            st, m_old, m_new = nxt
            if h + 1 < MLA_HEADS:
                nxt = scores(h + 1)
            alpha = jnp.exp2(m_old - m_new)
            pt = jnp.exp2(st - m_new)
            l_ref[h] = alpha * l_ref[h] + jnp.sum(pt, axis=0, keepdims=True)
            acc_ref[h] = alpha * acc_ref[h] + jnp.dot(
                vt_ref[j, h * 128:(h + 1) * 128, :], pt.astype(BF16), preferred_element_type=F32)
            m_ref[h] = m_new

    def body(j, c):
        step(j, False)
        return c

    lax.fori_loop(0, qi, body, 0)
    step(qi, True)

    for h in range(MLA_HEADS):
        o_scr[:, h * 128:(h + 1) * 128] = (acc_ref[h] / l_ref[h]).T
    o_ref[...] = (_rms(o_scr[...], g_ref[...]) * _silu(gate_ref[...].astype(F32))).astype(BF16)


def _mla_attn(q3, k3, vt4, z3, layer, g, *, tq):
    nb, seq, _ = q3.shape
    kern = functools.partial(_mla_attn_kernel, tq=tq)
    return pl.pallas_call(
        kern,
        grid=(nb, seq // tq),
        in_specs=[pl.BlockSpec((None, tq, 2048), lambda b, i: (b, i, 0)),
                  pl.BlockSpec((None, seq, 2048), lambda b, i: (b, 0, 0)),
                  pl.BlockSpec((None, seq // tq, BRANCH, tq), lambda b, i: (b, 0, 0, 0)),
                  pl.BlockSpec((None, tq, BRANCH), lambda b, i: (b, i, Z_MLA_GATE)),
                  _layer_spec(g, layer)],
        out_specs=pl.BlockSpec((None, tq, BRANCH), lambda b, i: (b, i, 0)),
        out_shape=jax.ShapeDtypeStruct((nb, seq, BRANCH), BF16),
        scratch_shapes=[pltpu.VMEM((MLA_HEADS, 1, tq), F32),
                        pltpu.VMEM((MLA_HEADS, 1, tq), F32),
                        pltpu.VMEM((MLA_HEADS, 128, tq), F32),
                        pltpu.VMEM((tq, BRANCH), F32)],
        compiler_params=_cparams(2),
        name="mla_attn",
    )(q3, k3, vt4, z3, g)


def _outproj_kernel(y0_ref, y1_ref, y2_ref, y3_ref, w_ref, x_ref, o_ref, wb_ref):
    first = pl.program_id(1) == 0
    ys = (y0_ref, y1_ref, y2_ref, y3_ref)

    @pl.when(first)
    def _():
        acc = x_ref[...]
        for n, y_ref in enumerate(ys):
            wb = w_ref[n * BRANCH:(n + 1) * BRANCH, :].astype(BF16)
            wb_ref[n * BRANCH:(n + 1) * BRANCH, :] = wb
            acc = acc + jnp.dot(y_ref[...], wb, preferred_element_type=F32)
        o_ref[...] = acc

    @pl.when(jnp.logical_not(first))
    def _():
        acc = x_ref[...]
        for n, y_ref in enumerate(ys):
            acc = acc + jnp.dot(y_ref[...], wb_ref[n * BRANCH:(n + 1) * BRANCH, :], preferred_element_type=F32)
        o_ref[...] = acc


def _outproj(ys, w_all, layer, x2, *, tm=1024, tn=512):
    t, d = x2.shape
    yspec = pl.BlockSpec((tm, BRANCH), lambda j, i: (i, 0))
    return pl.pallas_call(
        _outproj_kernel,
        grid=(d // tn, t // tm),
        in_specs=[yspec, yspec, yspec, yspec,
                  pl.BlockSpec((None, 4 * BRANCH, tn), lambda j, i: (layer, 0, j)),
                  pl.BlockSpec((tm, tn), lambda j, i: (i, j))],
        out_specs=pl.BlockSpec((tm, tn), lambda j, i: (i, j)),
        out_shape=jax.ShapeDtypeStruct((t, d), F32),
        scratch_shapes=[pltpu.VMEM((4 * BRANCH, tn), BF16)],
        compiler_params=_cparams(2),
        name="outproj",
    )(*ys, w_all, x2)


def _final_norm_kernel(x_ref, g_ref, o_ref):
    o_ref[...] = _rms(x_ref[...], g_ref[...])


def _final_norm(x2, g, *, tm=512):
    t, d = x2.shape
    return pl.pallas_call(
        _final_norm_kernel,
        grid=(t // tm,),
        in_specs=[pl.BlockSpec((tm, d), lambda i: (i, 0)), pl.BlockSpec((1, d), lambda i: (0, 0))],
        out_specs=pl.BlockSpec((tm, d), lambda i: (i, 0)),
        out_shape=jax.ShapeDtypeStruct((t, d), F32),
        compiler_params=_cparams(1),
        name="final_norm",
    )(x2, g)


def _mla_q_weight(w_uq):
    w = w_uq.reshape(MLA_Q_RANK, MLA_HEADS, MLA_NOPE + MLA_ROPE)
    nope = w[:, :, :MLA_NOPE].reshape(MLA_Q_RANK, MLA_HEADS * MLA_NOPE)
    pad = ((0, 0), (0, 0), (0, 128 - MLA_ROPE))
    rope_p = jnp.pad(w[:, :, MLA_NOPE:], pad).reshape(MLA_Q_RANK, MLA_HEADS * 128)
    return jnp.concatenate([nope, rope_p], axis=1).astype(BF16)


def _rope_tables():
    half = MLA_ROPE // 2
    inv = 1.0 / (ROPE_THETA ** (jnp.arange(half, dtype=F32) / half))
    zeros = jnp.zeros((128 - MLA_ROPE,), F32)
    invf = jnp.concatenate([inv, inv, zeros]).reshape(1, 128)
    sgn = jnp.concatenate([-jnp.ones((half,), F32), jnp.ones((half,), F32), zeros]).reshape(1, 128)
    msk = jnp.concatenate([jnp.ones((MLA_ROPE,), F32), zeros]).reshape(1, 128)
    return invf, sgn, msk


def _gla_tril(tg):
    r = np.arange(tg)
    m = (r[None, :] <= r[:, None]) & ((r[:, None] // GLA_CHUNK) == (r[None, :] // GLA_CHUNK))
    return jnp.asarray(m, dtype=BF16)


def kernel(x, positions, norm_g, w_in, s5_lambda_re, s5_lambda_im, s5_b_re, s5_b_im, s5_c_re, s5_c_im,
           s5_d, s5_log_dt, s5_w_glu, s5_b_glu, s5_out_g, mla_q_norm_g, mla_kv_norm_g, mla_w_uq,
           mla_w_uk, mla_w_uv, mla_out_g, gla_w_gate, gla_b_gate, gla_out_g, lru_conv_w, lru_conv_b,
           lru_w_a, lru_b_a, lru_w_x, lru_b_x, lru_lambda, lru_out_g, w_out, final_g):
    nb, seq, d = x.shape
    depth = w_in.shape[0]
    t = nb * seq
    x2 = x.reshape(t, d)
    pos2 = positions.reshape(t, 1)
    invf, sgn, msk = _rope_tables()
    gla_tg = 256
    tril = _gla_tril(gla_tg)

    rows = lambda v: v.reshape(depth, 1, -1).astype(F32)
    w_in_t = jnp.swapaxes(w_in, 1, 2)
    w_in_extra = _w_in_extra(w_in_t)
    norm_g_r = rows(norm_g)
    s5_wb, s5_lamr, s5_lami, s5_wcr, s5_wci = jax.vmap(functools.partial(_s5_params, nb=nb))(
        s5_lambda_re, s5_lambda_im, s5_b_re, s5_b_im, s5_c_re, s5_c_im, s5_log_dt)
    s5_rest = (rows(s5_d), s5_w_glu.astype(BF16), rows(s5_b_glu), rows(s5_out_g))
    mla_w = (rows(mla_q_norm_g), rows(mla_kv_norm_g), jax.vmap(_mla_q_weight)(mla_w_uq),
             mla_w_uk.astype(BF16), jnp.swapaxes(mla_w_uv, 1, 2).astype(BF16))
    mla_out_g_r = rows(mla_out_g)
    gla_w = (jnp.pad(gla_w_gate, ((0, 0), (0, 128 - GLA_GATE_RANK), (0, 0))).astype(BF16),
             rows(gla_b_gate), rows(gla_out_g))
    lru_w = (lru_conv_w.astype(F32), rows(lru_conv_b),
             jnp.concatenate([lru_w_a, lru_w_x], axis=-1).astype(BF16),
             rows(lru_b_a), rows(lru_b_x), rows(jax.nn.softplus(-lru_lambda.astype(F32))),
             rows(lru_out_g))

    for l in range(depth):
        z2 = _inproj(x2, norm_g_r, w_in_t, w_in_extra, l)
        z3 = z2.reshape(nb, seq, Z_COLS)
        y_s5, y_lru = _s5_lru(z3, l, (s5_wb, s5_lamr, s5_lami, s5_wcr, s5_wci) + s5_rest, lru_w)
        q2, k2, vt3 = _mla_proj(z2, pos2, l, *mla_w, invf, sgn, msk, tm=MLA_TQ)
        y_mla = _mla_attn(q2.reshape(nb, seq, 2048), k2.reshape(nb, seq, 2048),
                          vt3.reshape(nb, seq // MLA_TQ, BRANCH, MLA_TQ), z3, l, mla_out_g_r, tq=MLA_TQ)
        y_gla = _gla(z3, l, *gla_w, tril, tg=gla_tg)
        ys = [y.reshape(t, BRANCH) for y in (y_s5, y_mla, y_gla, y_lru)]
        x2 = _outproj(ys, w_out, l, x2)

    return _final_norm(x2, final_g.reshape(1, d).astype(F32)).reshape(nb, seq, d)
```

```python
import functools

import jax
import jax.numpy as jnp
import numpy as np
from jax import lax
from jax.experimental import pallas as pl
from jax.experimental.pallas import tpu as pltpu

F32 = jnp.float32
BF16 = jnp.bfloat16

D_MODEL = 4096
BRANCH = 1024
NORM_EPS = 1e-6

S5_GROUP = 16
S5_GROUPS = 64
S5_STATE = 64

MLA_NOPE = 128
MLA_ROPE = 64
MLA_HEADS = 8
MLA_Q_RANK = 768
MLA_KV_RANK = 512
ROPE_THETA = 10000.0
MLA_SCALE = (MLA_NOPE + MLA_ROPE) ** -0.5
NEG_INF = -1e30
LOG2_E = 1.4426950408889634
MLA_TQ = 512
GLA_HEADS = 4
GLA_DV = 256
GLA_DK = 128
GLA_GATE_RANK = 16
GLA_TAU = 16.0
GLA_CHUNK = 64

LRU_BLOCKS = 8
LRU_BLOCK_W = 128
LRU_CONV = 4
LRU_C = 8.0

LANES = 128
SUBLANES = 8

Z_S5_U, Z_S5_GATE, Z_MLA_GATE, Z_GLA_V, Z_GLA_GATE, Z_LRU_U, Z_LRU_GATE = range(7)
Z_CKV_512 = 14
Z_CQ_768 = 10
Z_KR_128 = 66
Z_GLR_128 = 67
Z_GLA_Q_512 = 17
Z_GLA_K_512 = 18
Z_COLS = 9728

VMEM_LIMIT = 56 * 1024 * 1024
NORM_ROWS = 256
S5_SCAN_W = 8


def _cparams(n_axes):
    return pltpu.CompilerParams(dimension_semantics=("arbitrary",) * n_axes,
                                vmem_limit_bytes=VMEM_LIMIT)


def _layer_spec(a, layer):
    zeros = (0,) * (a.ndim - 1)
    return pl.BlockSpec((None,) + tuple(a.shape[1:]), lambda *_: (layer,) + zeros)


def _whole_spec(a):
    return pl.BlockSpec(a.shape, lambda *_: (0,) * a.ndim)


def _rms(x, g):
    return x * lax.rsqrt(jnp.mean(x * x, axis=-1, keepdims=True) + NORM_EPS) * g


def _silu(x):
    return x * jax.nn.sigmoid(x)


W_IN_TN = 512
W_IN_EXTRA = 16
_W_IN_BLOCK_ROWS = (0, 512, 1024, 1536,
                    3392, 3904,
                    5440, 5952, 6480, 6992,
                    7504, 8016, 8528, 9040,
                    2816, 2048,
                    2048,
                    4416, 4928)
INPROJ_TAIL = 1024
INPROJ_TM = 1024
CAST_TN = 256


def _norm_rows(x_ref, g_ref, h_ref):
    for r in range(x_ref.shape[0] // NORM_ROWS):
        rows = slice(r * NORM_ROWS, (r + 1) * NORM_ROWS)
        h_ref[rows, :] = _rms(x_ref[rows, :], g_ref[...]).astype(BF16)


def _inproj_cast_kernel(rows_ref, x_ref, g_ref, w_ref, wx_ref, o_ref, wb_ref, h_ref):
    del rows_ref
    nt = (((1,), (1,)), ((), ()))
    j = pl.program_id(0)

    @pl.when(j == 0)
    def _():
        _norm_rows(x_ref, g_ref, h_ref)

    def project(w_f32):
        wb = w_f32.astype(BF16)
        wb_ref[...] = wb
        o_ref[...] = lax.dot_general(h_ref[...], wb, nt, preferred_element_type=F32).astype(o_ref.dtype)

    is_extra = j // (W_IN_TN // CAST_TN) == W_IN_EXTRA

    @pl.when(is_extra)
    def _():
        project(wx_ref[...])

    @pl.when(jnp.logical_not(is_extra))
    def _():
        project(w_ref[0])


def _inproj_kernel(x_ref, g_ref, w_ref, z_ref, o_ref, h_ref):
    del z_ref
    nt = (((1,), (1,)), ((), ()))
    first = pl.program_id(1) == 0

    @pl.when(first)
    def _():
        for r in range(x_ref.shape[0] // NORM_ROWS):
            rows = slice(r * NORM_ROWS, (r + 1) * NORM_ROWS)
            h = _rms(x_ref[rows, :], g_ref[...]).astype(BF16)
            h_ref[rows, :] = h
            o_ref[rows, :] = lax.dot_general(h, w_ref[...], nt, preferred_element_type=F32).astype(o_ref.dtype)

    @pl.when(jnp.logical_not(first))
    def _():
        o_ref[...] = lax.dot_general(h_ref[...], w_ref[...], nt,
                                     preferred_element_type=F32).astype(o_ref.dtype)


def _inproj(x2, g, w_t, w_extra, layer):
    t, d = x2.shape
    tn, nblk = W_IN_TN, len(_W_IN_BLOCK_ROWS)
    assert (t - INPROJ_TAIL) % INPROJ_TM == 0 and INPROJ_TM % INPROJ_TAIL == 0
    split = tn // CAST_TN
    rows = jnp.asarray([r + s * CAST_TN for r in _W_IN_BLOCK_ROWS for s in range(split)], jnp.int32)
    tail_blk = t // INPROJ_TAIL - 1
    extra_piece = lambda j: jnp.clip(j - W_IN_EXTRA * split, 0, split - 1)
    z_tail, w_bf = pl.pallas_call(
        _inproj_cast_kernel,
        grid_spec=pltpu.PrefetchScalarGridSpec(
            num_scalar_prefetch=1,
            grid=(nblk * split,),
            in_specs=[pl.BlockSpec((INPROJ_TAIL, d), lambda j, r: (tail_blk, 0), pipeline_mode=pl.Buffered(1)),
                      _layer_spec(g, layer),
                      pl.BlockSpec((pl.Element(1), pl.Element(CAST_TN), pl.Element(d)),
                                   lambda j, r: (layer, pl.multiple_of(r[j], 16), 0)),
                      pl.BlockSpec((None, CAST_TN, d), lambda j, r: (layer, extra_piece(j), 0))],
            out_specs=[pl.BlockSpec((INPROJ_TAIL, CAST_TN), lambda j, r: (tail_blk, j)),
                       pl.BlockSpec((None, CAST_TN, d), lambda j, r: (j, 0, 0))],
            scratch_shapes=[pltpu.VMEM((INPROJ_TAIL, d), BF16)]),
        out_shape=[jax.ShapeDtypeStruct((t, nblk * tn), BF16),
                   jax.ShapeDtypeStruct((nblk * split, CAST_TN, d), BF16)],
        compiler_params=_cparams(1),
        name="inproj_cast",
    )(rows, x2, g, w_t, w_extra)
    w_bf = w_bf.reshape(nblk, tn, d)
    return pl.pallas_call(
        _inproj_kernel,
        grid=((t - INPROJ_TAIL) // INPROJ_TM, nblk),
        in_specs=[pl.BlockSpec((INPROJ_TM, d), lambda i, j: (i, 0)),
                  _layer_spec(g, layer),
                  pl.BlockSpec((None, tn, d), lambda i, j: (j, 0, 0)),
                  pl.BlockSpec(memory_space=pl.ANY)],
        out_specs=pl.BlockSpec((INPROJ_TM, tn), lambda i, j: (i, j)),
        out_shape=jax.ShapeDtypeStruct((t, nblk * tn), BF16),
        scratch_shapes=[pltpu.VMEM((INPROJ_TM, d), BF16)],
        input_output_aliases={3: 0},
        compiler_params=_cparams(2),
        name="inproj",
    )(x2, g, w_bf, z_tail)


def _w_in_extra(w_t):
    depth, _, d = w_t.shape
    parts = [w_t[:, 2560:2816], w_t[:, 3328:3392], w_t[:, 3360:3392], w_t[:, 3328:3360],
             w_t[:, 6464:6480], jnp.zeros((depth, 128 - GLA_GATE_RANK, d), w_t.dtype)]
    return jnp.concatenate(parts, axis=1)


def _s5_phases(u_ref, gate_ref, wb_ref, lamr_ref, lami_ref, wcr_ref, wci_ref, d_ref,
               wglu_ref, bglu_ref, g_ref, o_ref,
               bre_ref, bim_ref, hre_ref, him_ref, y_ref, ush_ref, ysh_ref, *, nb, tc, pitch):
    shift = pitch - tc
    tcs = tc + 2 * shift

    @pl.when(pl.program_id(0) == 0)
    def _():
        hre_ref[...] = jnp.zeros_like(hre_ref)
        him_ref[...] = jnp.zeros_like(him_ref)
        ush_ref[...] = jnp.zeros_like(ush_ref)

    yield _PHASE
    ub = u_ref[...].reshape(nb * tc, BRANCH)
    ush_ref[:, shift:shift + tc, :] = u_ref[...].astype(F32)
    ubs = ush_ref[...].reshape(nb * tcs, BRANCH).astype(BF16)

    def block_rows(b, half):
        if half:
            return pl.ds((2 * b + 1) * pitch - shift, tcs)
        return pl.ds(2 * b * pitch, tc)

    for kb in range(8):
        half = kb // 4
        lhs, m = (ubs, tcs) if half else (ub, tc)
        res = jnp.dot(lhs[:, kb * 128:(kb + 1) * 128], wb_ref[kb], preferred_element_type=F32)
        for b in range(nb):
            rows = block_rows(b, half)
            for jj in range(4):
                j = (kb % 4) * 4 + jj
                bre_ref[j, rows, :] = res[b * m:(b + 1) * m, jj * 128:(jj + 1) * 128]
                bim_ref[j, rows, :] = res[b * m:(b + 1) * m, 512 + jj * 128:512 + (jj + 1) * 128]
        yield _STEP

    yield _PHASE
    for cb in range(16 // S5_SCAN_W):
        slabs = [cb * S5_SCAN_W + jj for jj in range(S5_SCAN_W)]
        lr = [lamr_ref[:, j * 128:(j + 1) * 128] for j in slabs]
        li = [lami_ref[:, j * 128:(j + 1) * 128] for j in slabs]
        init = []
        for j in slabs:
            init.append(hre_ref[:, j * 128:(j + 1) * 128])
            init.append(him_ref[:, j * 128:(j + 1) * 128])

        def body(t, carry, slabs=slabs, lr=lr, li=li):
            new = []
            idx = pl.ds(t, SUBLANES, stride=pitch)
            for jj, j in enumerate(slabs):
                hr, hi = carry[2 * jj], carry[2 * jj + 1]
                nr = lr[jj] * hr - li[jj] * hi + bre_ref[j, idx, :]
                ni = lr[jj] * hi + li[jj] * hr + bim_ref[j, idx, :]
                bre_ref[j, idx, :] = nr
                bim_ref[j, idx, :] = ni
                new += [nr, ni]
            return tuple(new)

        fin = lax.fori_loop(0, tc, body, tuple(init), unroll=2)
        for jj, j in enumerate(slabs):
            hre_ref[:, j * 128:(j + 1) * 128] = fin[2 * jj]
            him_ref[:, j * 128:(j + 1) * 128] = fin[2 * jj + 1]

    yield _PHASE
    for kb in range(8):
        half = kb // 4
        sre, sim = [], []
        for b in range(nb):
            rows = block_rows(b, half)
            sre.append(jnp.concatenate([bre_ref[(kb % 4) * 4 + jj, rows, :] for jj in range(4)], axis=-1))
            sim.append(jnp.concatenate([bim_ref[(kb % 4) * 4 + jj, rows, :] for jj in range(4)], axis=-1))
        sre = jnp.concatenate(sre, axis=0).astype(BF16)
        sim = jnp.concatenate(sim, axis=0).astype(BF16)
        ykb = (jnp.dot(sre, wcr_ref[kb], preferred_element_type=F32)
               + jnp.dot(sim, wci_ref[kb], preferred_element_type=F32))
        if half:
            ysh_ref[...] = ykb.reshape(nb, tcs, 128)
            ykb = ysh_ref[:, shift:shift + tc, :].reshape(nb * tc, 128)
        cols = slice(kb * 128, (kb + 1) * 128)
        y_ref[:, cols] = ykb + d_ref[:, cols] * ub[:, cols].astype(F32)
        yield _STEP

    y = jax.nn.gelu(y_ref[...], approximate=True)
    zg = jnp.dot(y.astype(BF16), wglu_ref[...], preferred_element_type=F32) + bglu_ref[...]
    y = y * jax.nn.sigmoid(zg)
    gate = gate_ref[...].reshape(nb * tc, BRANCH).astype(F32)
    o_ref[...] = (_rms(y, g_ref[...]) * _silu(gate)).astype(BF16).reshape(nb, tc, BRANCH)


def _s5_params(lam_re, lam_im, b_re, b_im, c_re, c_im, log_dt, nb):
    dt = jnp.exp(log_dt.astype(F32))[:, None]
    lr, li = lam_re.astype(F32), lam_im.astype(F32)
    mag = jnp.exp(lr * dt)
    lbr, lbi = mag * jnp.cos(li * dt), mag * jnp.sin(li * dt)
    den = lr * lr + li * li
    cr = ((lbr - 1.0) * lr + lbi * li) / den
    ci = (lbi * lr - (lbr - 1.0) * li) / den
    bbr = cr[..., None] * b_re - ci[..., None] * b_im
    bbi = cr[..., None] * b_im + ci[..., None] * b_re
    eye = jnp.eye(8, dtype=F32)

    def in_layout(m):
        m = m.reshape(8, 8, S5_STATE, S5_GROUP)
        return jnp.einsum('kgph,gG->kghGp', m, eye).reshape(8, 128, 512)

    def out_layout(m):
        m = m.reshape(8, 8, S5_GROUP, S5_STATE)
        return jnp.einsum('kghp,gG->kgpGh', m, eye).reshape(8, 512, 128)

    wb = jnp.concatenate([in_layout(bbr), in_layout(bbi)], axis=-1).astype(BF16)
    wcr = out_layout(c_re.astype(F32)).astype(BF16)
    wci = out_layout(-c_im.astype(F32)).astype(BF16)

    def seq_layout(v):
        return jnp.tile(v.reshape(1, 2, 2048), (nb, 1, 1)).reshape(2 * nb, 2048)

    return wb, seq_layout(lbr), seq_layout(lbi), wcr, wci


def _lru_phases(u_ref, gate_ref, cw_ref, cb_ref, wax_ref, ba_ref, bx_ref, sp_ref, g_ref, o_ref,
                xpad_ref, a_ref, h_ref, st_ref, *, nb, tc, pitch):
    @pl.when(pl.program_id(0) == 0)
    def _():
        xpad_ref[:, 0:SUBLANES, :] = jnp.zeros((nb, SUBLANES, BRANCH), F32)
        st_ref[...] = jnp.zeros_like(st_ref)

    yield _PHASE
    xpad_ref[:, SUBLANES:SUBLANES + tc, :] = u_ref[...].astype(F32)
    xc = cb_ref[...].reshape(1, 1, BRANCH)
    for k in range(LRU_CONV):
        off = SUBLANES - (LRU_CONV - 1) + k
        xc = xc + cw_ref[k:k + 1, :].reshape(1, 1, BRANCH) * xpad_ref[:, off:off + tc, :]
    xpad_ref[:, 0:SUBLANES, :] = xpad_ref[:, tc:tc + SUBLANES, :]
    x2 = xc.reshape(nb * tc, BRANCH)
    xb = x2.astype(BF16)
    yield _STEP

    for hb in range(LRU_BLOCKS):
        cols = slice(hb * 128, (hb + 1) * 128)
        ri = jnp.dot(xb[:, cols], wax_ref[hb], preferred_element_type=F32)
        r = jax.nn.sigmoid(ri[:, :128] + ba_ref[:, cols])
        ig = jax.nn.sigmoid(ri[:, 128:] + bx_ref[:, cols])
        log_a = (-LRU_C) * r * sp_ref[:, cols]
        a = jnp.exp(log_a)
        gated = jnp.sqrt(-jnp.tanh(log_a) * (a * a + 1.0)) * (ig * x2[:, cols])
        half, j = hb // 4, hb % 4
        for b in range(nb):
            rows = pl.ds((2 * b + half) * pitch, tc)
            a_ref[j, rows, :] = a[b * tc:(b + 1) * tc]
            h_ref[j, rows, :] = gated[b * tc:(b + 1) * tc]
        yield _STEP

    yield _PHASE

    def body(t, carry):
        new = []
        for j in range(4):
            idx = pl.ds(t, SUBLANES, stride=pitch)
            h = a_ref[j, idx, :] * carry[j] + h_ref[j, idx, :]
            h_ref[j, idx, :] = h
            new.append(h)
        return tuple(new)

    init = tuple(st_ref[:, j * 128:(j + 1) * 128] for j in range(4))
    fin = lax.fori_loop(0, tc, body, init, unroll=8)
    for j in range(4):
        st_ref[:, j * 128:(j + 1) * 128] = fin[j]

    yield _PHASE
    for b in range(nb):
        parts = [h_ref[j, pl.ds((2 * b + half) * pitch, tc), :] for half in range(2) for j in range(4)]
        hb_ = jnp.concatenate(parts, axis=-1)
        o_ref[b] = (_rms(hb_, g_ref[...]) * _silu(gate_ref[b].astype(F32))).astype(BF16)
        yield _STEP


_STEP, _PHASE = 0, 1
S5_N_IN, LRU_N_IN = 11, 9
S5_N_SCRATCH = 7


def _s5_lru_kernel(*refs, nb, tc, s5_pitch, lru_pitch):
    s5_in, refs = refs[:S5_N_IN], refs[S5_N_IN:]
    lru_in, refs = refs[:LRU_N_IN], refs[LRU_N_IN:]
    (o_s5, o_lru), refs = refs[:2], refs[2:]
    s5 = _s5_phases(*s5_in, o_s5, *refs[:S5_N_SCRATCH], nb=nb, tc=tc, pitch=s5_pitch)
    lru = _lru_phases(*lru_in, o_lru, *refs[S5_N_SCRATCH:], nb=nb, tc=tc, pitch=lru_pitch)
    _run_interleaved([s5, lru])


def _run_interleaved(gens):
    live = list(gens)
    while live:
        running = list(live)
        while running:
            for g in list(running):
                token = next(g, None)
                if token is None:
                    running.remove(g)
                    live.remove(g)
                elif token == _PHASE:
                    running.remove(g)


def _s5_lru(z3, layer, s5_params, lru_params, *, tc=128):
    nb, seq, _ = z3.shape
    assert 2 * nb == SUBLANES and len(s5_params) == S5_N_IN - 2 and len(lru_params) == LRU_N_IN - 2
    s5_pitch = tc + SUBLANES // 2
    lru_pitch = tc + SUBLANES
    full = functools.partial(_layer_spec, layer=layer)
    zblk = lambda c: pl.BlockSpec((nb, tc, BRANCH), lambda i: (0, i, c))
    oblk = pl.BlockSpec((nb, tc, BRANCH), lambda i: (0, i, 0))
    kern = functools.partial(_s5_lru_kernel, nb=nb, tc=tc, s5_pitch=s5_pitch, lru_pitch=lru_pitch)
    return pl.pallas_call(
        kern,
        grid=(seq // tc,),
        in_specs=([zblk(Z_S5_U), zblk(Z_S5_GATE)] + [full(a) for a in s5_params]
                  + [zblk(Z_LRU_U), zblk(Z_LRU_GATE)] + [full(a) for a in lru_params]),
        out_specs=[oblk, oblk],
        out_shape=[jax.ShapeDtypeStruct((nb, seq, BRANCH), BF16)] * 2,
        scratch_shapes=[pltpu.VMEM((16, SUBLANES * s5_pitch, LANES), F32),
                        pltpu.VMEM((16, SUBLANES * s5_pitch, LANES), F32),
                        pltpu.VMEM((SUBLANES, 2048), F32),
                        pltpu.VMEM((SUBLANES, 2048), F32),
                        pltpu.VMEM((nb * tc, BRANCH), F32),
                        pltpu.VMEM((nb, tc + SUBLANES, BRANCH), F32),
                        pltpu.VMEM((nb, tc + SUBLANES, LANES), F32),
                        pltpu.VMEM((nb, tc + SUBLANES, BRANCH), F32),
                        pltpu.VMEM((4, SUBLANES * lru_pitch, LANES), F32),
                        pltpu.VMEM((4, SUBLANES * lru_pitch, LANES), F32),
                        pltpu.VMEM((SUBLANES, 512), F32)],
        compiler_params=_cparams(1),
        name="s5_rglru",
    )(z3, z3, *s5_params, z3, z3, *lru_params)


def _gla_kernel(q_ref, k_ref, v_ref, glr_ref, gate_ref, wg_ref, bg_ref, og_ref, tril_ref, o_ref,
                st_ref, *, tg):
    @pl.when(pl.program_id(1) == 0)
    def _():
        st_ref[...] = jnp.zeros_like(st_ref)

    n_ch = tg // GLA_CHUNK
    x = jnp.dot(glr_ref[...], wg_ref[...], preferred_element_type=F32) + bg_ref[...]
    log_a = (jnp.minimum(x, 0.0) - jnp.log1p(jnp.exp(-jnp.abs(x)))) / GLA_TAU

    tril = tril_ref[...]
    p0 = log_a.astype(BF16)
    r0 = log_a - p0.astype(F32)
    p1 = r0.astype(BF16)
    p2 = (r0 - p1.astype(F32)).astype(BF16)
    cum = (jnp.dot(tril, p0, preferred_element_type=F32)
           + jnp.dot(tril, p1, preferred_element_type=F32)
           + jnp.dot(tril, p2, preferred_element_type=F32))

    row = lax.broadcasted_iota(jnp.int32, (tg, tg), 0)
    col = lax.broadcasted_iota(jnp.int32, (tg, tg), 1)
    causal = (col <= row) & ((row // GLA_CHUNK) == (col // GLA_CHUNK))
    nt = (((1,), (1,)), ((), ()))
    tn = (((0,), (0,)), ((), ()))

    for hh in range(GLA_HEADS):
        kc = slice(hh * GLA_DK, (hh + 1) * GLA_DK)
        vc = slice(hh * GLA_DV, (hh + 1) * GLA_DV)
        cumh = cum[:, kc]
        kh = k_ref[:, kc].astype(F32)
        qd = (q_ref[:, kc].astype(F32) * (GLA_DK ** -0.5)) * jnp.exp(cumh)
        ki = kh * jnp.exp(-cumh)
        qdb = qd.astype(BF16)
        att = lax.dot_general(qdb, ki.astype(BF16), nt, preferred_element_type=F32)
        att = jnp.where(causal, att, 0.0)
        vb = v_ref[:, vc]
        o = jnp.dot(att.astype(BF16), vb, preferred_element_type=F32)
        st = st_ref[hh]
        inter = []
        for c in range(n_ch):
            rows = slice(c * GLA_CHUNK, (c + 1) * GLA_CHUNK)
            tot = cumh[(c + 1) * GLA_CHUNK - 1:(c + 1) * GLA_CHUNK, :]
            kd = kh[rows] * jnp.exp(tot - cumh[rows])
            inter.append(lax.dot_general(qdb[rows], st.astype(BF16), nt, preferred_element_type=F32))
            kv_t = lax.dot_general(vb[rows], kd.astype(BF16), tn, preferred_element_type=F32)
            st = st * jnp.exp(tot) + kv_t
        st_ref[hh] = st
        o = o + jnp.concatenate(inter, axis=0)
        gate = gate_ref[:, vc].astype(F32)
        o_ref[:, vc] = (_rms(o, og_ref[:, vc]) * _silu(gate)).astype(BF16)


def _gla(z3, layer, wg, bg, og, tril, *, tg=256):
    nb, seq, _ = z3.shape
    kern = functools.partial(_gla_kernel, tg=tg)
    full = functools.partial(_layer_spec, layer=layer)
    return pl.pallas_call(
        kern,
        grid=(nb, seq // tg),
        in_specs=[pl.BlockSpec((None, tg, 512), lambda b, i: (b, i, Z_GLA_Q_512)),
                  pl.BlockSpec((None, tg, 512), lambda b, i: (b, i, Z_GLA_K_512)),
                  pl.BlockSpec((None, tg, BRANCH), lambda b, i: (b, i, Z_GLA_V)),
                  pl.BlockSpec((None, tg, 128), lambda b, i: (b, i, Z_GLR_128)),
                  pl.BlockSpec((None, tg, BRANCH), lambda b, i: (b, i, Z_GLA_GATE)),
                  full(wg), full(bg), full(og), _whole_spec(tril)],
        out_specs=pl.BlockSpec((None, tg, BRANCH), lambda b, i: (b, i, 0)),
        out_shape=jax.ShapeDtypeStruct((nb, seq, BRANCH), BF16),
        scratch_shapes=[pltpu.VMEM((GLA_HEADS, GLA_DV, GLA_DK), F32)],
        compiler_params=_cparams(2),
        name="gla",
    )(z3, z3, z3, z3, z3, wg, bg, og, tril)


def _mla_proj_kernel(cq_ref, ckv_ref, kr_ref, pos_ref, gq_ref, gkv_ref, wq_ref, wk_ref, wvt_ref,
                     invf_ref, sgn_ref, msk_ref, q_out, k_out, vt_out):
    cqn = _rms(cq_ref[...].astype(F32), gq_ref[...]).astype(BF16)
    qa = jnp.dot(cqn, wq_ref[...], preferred_element_type=F32)
    ang = pos_ref[...].astype(F32) * invf_ref[...]
    cosm = jnp.cos(ang) * msk_ref[...]
    sinm = jnp.sin(ang) * sgn_ref[...]
    nh = MLA_HEADS * 128
    half = MLA_ROPE // 2
    lo_half = lax.broadcasted_iota(jnp.int32, (1, 128), 1) < half
    qscale = MLA_SCALE * LOG2_E
    for h in range(MLA_HEADS):
        q_out[:, h * 256:h * 256 + 128] = (qa[:, h * 128:(h + 1) * 128] * qscale).astype(BF16)
        qr = qa[:, nh + h * 128:nh + (h + 1) * 128]
        swapped = jnp.where(lo_half, pltpu.roll(qr, 128 - half, 1), pltpu.roll(qr, half, 1))
        q_out[:, h * 256 + 128:(h + 1) * 256] = ((qr * cosm + swapped * sinm) * qscale).astype(BF16)
    c = _rms(ckv_ref[...].astype(F32), gkv_ref[...]).astype(BF16)
    kn = jnp.dot(c, wk_ref[...], preferred_element_type=F32)
    kr = kr_ref[...].astype(F32)
    krp = (kr * cosm + pltpu.roll(kr, 64, 1) * sinm).astype(BF16)
    for h in range(MLA_HEADS):
        k_out[:, h * 256:h * 256 + 128] = kn[:, h * 128:(h + 1) * 128].astype(BF16)
        k_out[:, h * 256 + 128:(h + 1) * 256] = krp
    nt = (((1,), (1,)), ((), ()))
    vt_out[...] = lax.dot_general(wvt_ref[...], c, nt, preferred_element_type=F32).astype(BF16)


def _mla_proj(z2, pos2, layer, gq, gkv, wq, wk, wvt, invf, sgn, msk, *, tm):
    t = z2.shape[0]
    full = functools.partial(_layer_spec, layer=layer)
    return pl.pallas_call(
        _mla_proj_kernel,
        grid=(t // tm,),
        in_specs=[pl.BlockSpec((tm, MLA_Q_RANK), lambda i: (i, Z_CQ_768)),
                  pl.BlockSpec((tm, MLA_KV_RANK), lambda i: (i, Z_CKV_512)),
                  pl.BlockSpec((tm, 128), lambda i: (i, Z_KR_128)),
                  pl.BlockSpec((tm, 1), lambda i: (i, 0)),
                  full(gq), full(gkv), full(wq), full(wk), full(wvt),
                  _whole_spec(invf), _whole_spec(sgn), _whole_spec(msk)],
        out_specs=[pl.BlockSpec((tm, 2048), lambda i: (i, 0)),
                   pl.BlockSpec((tm, 2048), lambda i: (i, 0)),
                   pl.BlockSpec((None, BRANCH, tm), lambda i: (i, 0, 0))],
        out_shape=[jax.ShapeDtypeStruct((t, 2048), BF16),
                   jax.ShapeDtypeStruct((t, 2048), BF16),
                   jax.ShapeDtypeStruct((t // tm, BRANCH, tm), BF16)],
        compiler_params=_cparams(1),
        name="mla_proj",
    )(z2, z2, z2, pos2, gq, gkv, wq, wk, wvt, invf, sgn, msk)


def _mla_attn_kernel(q_ref, k_ref, vt_ref, gate_ref, g_ref, o_ref, m_ref, l_ref, acc_ref, o_scr, *, tq):
    qi = pl.program_id(1)
    nt = (((1,), (1,)), ((), ()))
    key = lax.broadcasted_iota(jnp.int32, (tq, tq), 0)
    qry = lax.broadcasted_iota(jnp.int32, (tq, tq), 1)

    m_ref[...] = jnp.full(m_ref.shape, NEG_INF, F32)
    l_ref[...] = jnp.zeros_like(l_ref)
    acc_ref[...] = jnp.zeros_like(acc_ref)

    def step(j, masked):
        r0 = pl.multiple_of(j * tq, tq)

        def scores(h):
            st = lax.dot_general(k_ref[pl.ds(r0, tq), h * 256:(h + 1) * 256],
                                 q_ref[:, h * 256:(h + 1) * 256], nt, preferred_element_type=F32)
            if masked:
                st = jnp.where(key <= qry, st, NEG_INF)
            m_old = m_ref[h]
            return st, m_old, jnp.maximum(m_old, jnp.max(st, axis=0, keepdims=True))

        nxt = scores(0)
        for h in range(MLA_HEADS):
            st, m_old, m_new = nxt
            if h + 1 < MLA_HEADS:
                nxt = scores(h + 1)
            alpha = jnp.exp2(m_old - m_new)
            pt = jnp.exp2(st - m_new)
            l_ref[h] = alpha * l_ref[h] + jnp.sum(pt, axis=0, keepdims=True)
            acc_ref[h] = alpha * acc_ref[h] + jnp.dot(
                vt_ref[j, h * 128:(h + 1) * 128, :], pt.astype(BF16), preferred_element_type=F32)
            m_ref[h] = m_new

    def body(j, c):
        step(j, False)
        return c

    lax.fori_loop(0, qi, body, 0)
    step(qi, True)

    for h in range(MLA_HEADS):
        o_scr[:, h * 128:(h + 1) * 128] = (acc_ref[h] / l_ref[h]).T
    o_ref[...] = (_rms(o_scr[...], g_ref[...]) * _silu(gate_ref[...].astype(F32))).astype(BF16)


def _mla_attn(q3, k3, vt4, z3, layer, g, *, tq):
    nb, seq, _ = q3.shape
    kern = functools.partial(_mla_attn_kernel, tq=tq)
    return pl.pallas_call(
        kern,
        grid=(nb, seq // tq),
        in_specs=[pl.BlockSpec((None, tq, 2048), lambda b, i: (b, i, 0)),
                  pl.BlockSpec((None, seq, 2048), lambda b, i: (b, 0, 0)),
                  pl.BlockSpec((None, seq // tq, BRANCH, tq), lambda b, i: (b, 0, 0, 0)),
                  pl.BlockSpec((None, tq, BRANCH), lambda b, i: (b, i, Z_MLA_GATE)),
                  _layer_spec(g, layer)],
        out_specs=pl.BlockSpec((None, tq, BRANCH), lambda b, i: (b, i, 0)),
        out_shape=jax.ShapeDtypeStruct((nb, seq, BRANCH), BF16),
        scratch_shapes=[pltpu.VMEM((MLA_HEADS, 1, tq), F32),
                        pltpu.VMEM((MLA_HEADS, 1, tq), F32),
                        pltpu.VMEM((MLA_HEADS, 128, tq), F32),
                        pltpu.VMEM((tq, BRANCH), F32)],
        compiler_params=_cparams(2),
        name="mla_attn",
    )(q3, k3, vt4, z3, g)


def _outproj_kernel(y0_ref, y1_ref, y2_ref, y3_ref, w_ref, x_ref, o_ref, wb_ref):
    first = pl.program_id(1) == 0
    ys = (y0_ref, y1_ref, y2_ref, y3_ref)

    @pl.when(first)
    def _():
        acc = x_ref[...]
        for n, y_ref in enumerate(ys):
            wb = w_ref[n * BRANCH:(n + 1) * BRANCH, :].astype(BF16)
            wb_ref[n * BRANCH:(n + 1) * BRANCH, :] = wb
            acc = acc + jnp.dot(y_ref[...], wb, preferred_element_type=F32)
        o_ref[...] = acc

    @pl.when(jnp.logical_not(first))
    def _():
        acc = x_ref[...]
        for n, y_ref in enumerate(ys):
            acc = acc + jnp.dot(y_ref[...], wb_ref[n * BRANCH:(n + 1) * BRANCH, :], preferred_element_type=F32)
        o_ref[...] = acc


def _outproj(ys, w_all, layer, x2, *, tm=1024, tn=512):
    t, d = x2.shape
    yspec = pl.BlockSpec((tm, BRANCH), lambda j, i: (i, 0))
    return pl.pallas_call(
        _outproj_kernel,
        grid=(d // tn, t // tm),
        in_specs=[yspec, yspec, yspec, yspec,
                  pl.BlockSpec((None, 4 * BRANCH, tn), lambda j, i: (layer, 0, j)),
                  pl.BlockSpec((tm, tn), lambda j, i: (i, j))],
        out_specs=pl.BlockSpec((tm, tn), lambda j, i: (i, j)),
        out_shape=jax.ShapeDtypeStruct((t, d), F32),
        scratch_shapes=[pltpu.VMEM((4 * BRANCH, tn), BF16)],
        compiler_params=_cparams(2),
        name="outproj",
    )(*ys, w_all, x2)


def _final_norm_kernel(x_ref, g_ref, o_ref):
    o_ref[...] = _rms(x_ref[...], g_ref[...])


def _final_norm(x2, g, *, tm=512):
    t, d = x2.shape
    return pl.pallas_call(
        _final_norm_kernel,
        grid=(t // tm,),
        in_specs=[pl.BlockSpec((tm, d), lambda i: (i, 0)), pl.BlockSpec((1, d), lambda i: (0, 0))],
        out_specs=pl.BlockSpec((tm, d), lambda i: (i, 0)),
        out_shape=jax.ShapeDtypeStruct((t, d), F32),
        compiler_params=_cparams(1),
        name="final_norm",
    )(x2, g)


def _mla_q_weight(w_uq):
    w = w_uq.reshape(MLA_Q_RANK, MLA_HEADS, MLA_NOPE + MLA_ROPE)
    nope = w[:, :, :MLA_NOPE].reshape(MLA_Q_RANK, MLA_HEADS * MLA_NOPE)
    pad = ((0, 0), (0, 0), (0, 128 - MLA_ROPE))
    rope_p = jnp.pad(w[:, :, MLA_NOPE:], pad).reshape(MLA_Q_RANK, MLA_HEADS * 128)
    return jnp.concatenate([nope, rope_p], axis=1).astype(BF16)


def _rope_tables():
    half = MLA_ROPE // 2
    inv = 1.0 / (ROPE_THETA ** (jnp.arange(half, dtype=F32) / half))
    zeros = jnp.zeros((128 - MLA_ROPE,), F32)
    invf = jnp.concatenate([inv, inv, zeros]).reshape(1, 128)
    sgn = jnp.concatenate([-jnp.ones((half,), F32), jnp.ones((half,), F32), zeros]).reshape(1, 128)
    msk = jnp.concatenate([jnp.ones((MLA_ROPE,), F32), zeros]).reshape(1, 128)
    return invf, sgn, msk


def _gla_tril(tg):
    r = np.arange(tg)
    m = (r[None, :] <= r[:, None]) & ((r[:, None] // GLA_CHUNK) == (r[None, :] // GLA_CHUNK))
    return jnp.asarray(m, dtype=BF16)


def kernel(x, positions, norm_g, w_in, s5_lambda_re, s5_lambda_im, s5_b_re, s5_b_im, s5_c_re, s5_c_im,
           s5_d, s5_log_dt, s5_w_glu, s5_b_glu, s5_out_g, mla_q_norm_g, mla_kv_norm_g, mla_w_uq,
           mla_w_uk, mla_w_uv, mla_out_g, gla_w_gate, gla_b_gate, gla_out_g, lru_conv_w, lru_conv_b,
           lru_w_a, lru_b_a, lru_w_x, lru_b_x, lru_lambda, lru_out_g, w_out, final_g):
    nb, seq, d = x.shape
    depth = w_in.shape[0]
    t = nb * seq
    x2 = x.reshape(t, d)
    pos2 = positions.reshape(t, 1)
    invf, sgn, msk = _rope_tables()
    gla_tg = 256
    tril = _gla_tril(gla_tg)

    rows = lambda v: v.reshape(depth, 1, -1).astype(F32)
    w_in_t = jnp.swapaxes(w_in, 1, 2)
    w_in_extra = _w_in_extra(w_in_t)
    norm_g_r = rows(norm_g)
    s5_wb, s5_lamr, s5_lami, s5_wcr, s5_wci = jax.vmap(functools.partial(_s5_params, nb=nb))(
        s5_lambda_re, s5_lambda_im, s5_b_re, s5_b_im, s5_c_re, s5_c_im, s5_log_dt)
    s5_rest = (rows(s5_d), s5_w_glu.astype(BF16), rows(s5_b_glu), rows(s5_out_g))
    mla_w = (rows(mla_q_norm_g), rows(mla_kv_norm_g), jax.vmap(_mla_q_weight)(mla_w_uq),
             mla_w_uk.astype(BF16), jnp.swapaxes(mla_w_uv, 1, 2).astype(BF16))
    mla_out_g_r = rows(mla_out_g)
    gla_w = (jnp.pad(gla_w_gate, ((0, 0), (0, 128 - GLA_GATE_RANK), (0, 0))).astype(BF16),
             rows(gla_b_gate), rows(gla_out_g))
    lru_w = (lru_conv_w.astype(F32), rows(lru_conv_b),
             jnp.concatenate([lru_w_a, lru_w_x], axis=-1).astype(BF16),
             rows(lru_b_a), rows(lru_b_x), rows(jax.nn.softplus(-lru_lambda.astype(F32))),
             rows(lru_out_g))

    for l in range(depth):
        z2 = _inproj(x2, norm_g_r, w_in_t, w_in_extra, l)
        z3 = z2.reshape(nb, seq, Z_COLS)
        y_s5, y_lru = _s5_lru(z3, l, (s5_wb, s5_lamr, s5_lami, s5_wcr, s5_wci) + s5_rest, lru_w)
        q2, k2, vt3 = _mla_proj(z2, pos2, l, *mla_w, invf, sgn, msk, tm=MLA_TQ)
        y_mla = _mla_attn(q2.reshape(nb, seq, 2048), k2.reshape(nb, seq, 2048),
                          vt3.reshape(nb, seq // MLA_TQ, BRANCH, MLA_TQ), z3, l, mla_out_g_r, tq=MLA_TQ)
        y_gla = _gla(z3, l, *gla_w, tril, tg=gla_tg)
        ys = [y.reshape(t, BRANCH) for y in (y_s5, y_mla, y_gla, y_lru)]
        x2 = _outproj(ys, w_out, l, x2)

    return _final_norm(x2, final_g.reshape(1, d).astype(F32)).reshape(nb, seq, d)
```

```python
import functools

import jax
import jax.numpy as jnp
import numpy as np
from jax import lax
from jax.experimental import pallas as pl
from jax.experimental.pallas import tpu as pltpu

F32 = jnp.float32
BF16 = jnp.bfloat16

D_MODEL = 4096
BRANCH = 1024
NORM_EPS = 1e-6

S5_GROUP = 16
S5_GROUPS = 64
S5_STATE = 64

MLA_NOPE = 128
MLA_ROPE = 64
MLA_HEADS = 8
MLA_Q_RANK = 768
MLA_KV_RANK = 512
ROPE_THETA = 10000.0
MLA_SCALE = (MLA_NOPE + MLA_ROPE) ** -0.5
NEG_INF = -1e30
LOG2_E = 1.4426950408889634
MLA_TQ = 512
GLA_HEADS = 4
GLA_DV = 256
GLA_DK = 128
GLA_GATE_RANK = 16
GLA_TAU = 16.0
GLA_CHUNK = 64

LRU_BLOCKS = 8
LRU_BLOCK_W = 128
LRU_CONV = 4
LRU_C = 8.0

LANES = 128
SUBLANES = 8

Z_S5_U, Z_S5_GATE, Z_MLA_GATE, Z_GLA_V, Z_GLA_GATE, Z_LRU_U, Z_LRU_GATE = range(7)
Z_CKV_512 = 14
Z_CQ_768 = 10
Z_KR_128 = 66
Z_GLR_128 = 67
Z_GLA_Q_512 = 17
Z_GLA_K_512 = 18
Z_COLS = 9728

VMEM_LIMIT = 56 * 1024 * 1024
NORM_ROWS = 256
S5_SCAN_W = 8


def _cparams(n_axes):
    return pltpu.CompilerParams(dimension_semantics=("arbitrary",) * n_axes,
                                vmem_limit_bytes=VMEM_LIMIT)


def _layer_spec(a, layer):
    zeros = (0,) * (a.ndim - 1)
    return pl.BlockSpec((None,) + tuple(a.shape[1:]), lambda *_: (layer,) + zeros)


def _whole_spec(a):
    return pl.BlockSpec(a.shape, lambda *_: (0,) * a.ndim)


def _rms(x, g):
    return x * lax.rsqrt(jnp.mean(x * x, axis=-1, keepdims=True) + NORM_EPS) * g


def _silu(x):
    return x * jax.nn.sigmoid(x)


W_IN_TN = 512
W_IN_EXTRA = 16
_W_IN_BLOCK_ROWS = (0, 512, 1024, 1536,
                    3392, 3904,
                    5440, 5952, 6480, 6992,
                    7504, 8016, 8528, 9040,
                    2816, 2048,
                    2048,
                    4416, 4928)
INPROJ_TAIL = 1024
INPROJ_TM = 1024
CAST_TN = 256


def _norm_rows(x_ref, g_ref, h_ref):
    for r in range(x_ref.shape[0] // NORM_ROWS):
        rows = slice(r * NORM_ROWS, (r + 1) * NORM_ROWS)
        h_ref[rows, :] = _rms(x_ref[rows, :], g_ref[...]).astype(BF16)


def _inproj_cast_kernel(rows_ref, x_ref, g_ref, w_ref, wx_ref, o_ref, wb_ref, h_ref):
    del rows_ref
    nt = (((1,), (1,)), ((), ()))
    j = pl.program_id(0)

    @pl.when(j == 0)
    def _():
        _norm_rows(x_ref, g_ref, h_ref)

    def project(w_f32):
        wb = w_f32.astype(BF16)
        wb_ref[...] = wb
        o_ref[...] = lax.dot_general(h_ref[...], wb, nt, preferred_element_type=F32).astype(o_ref.dtype)

    is_extra = j // (W_IN_TN // CAST_TN) == W_IN_EXTRA

    @pl.when(is_extra)
    def _():
        project(wx_ref[...])

    @pl.when(jnp.logical_not(is_extra))
    def _():
        project(w_ref[0])


def _inproj_kernel(x_ref, g_ref, w_ref, z_ref, o_ref, h_ref):
    del z_ref
    nt = (((1,), (1,)), ((), ()))
    first = pl.program_id(1) == 0

    @pl.when(first)
    def _():
        for r in range(x_ref.shape[0] // NORM_ROWS):
            rows = slice(r * NORM_ROWS, (r + 1) * NORM_ROWS)
            h = _rms(x_ref[rows, :], g_ref[...]).astype(BF16)
            h_ref[rows, :] = h
            o_ref[rows, :] = lax.dot_general(h, w_ref[...], nt, preferred_element_type=F32).astype(o_ref.dtype)

    @pl.when(jnp.logical_not(first))
    def _():
        o_ref[...] = lax.dot_general(h_ref[...], w_ref[...], nt,
                                     preferred_element_type=F32).astype(o_ref.dtype)


def _inproj(x2, g, w_t, w_extra, layer):
    t, d = x2.shape
    tn, nblk = W_IN_TN, len(_W_IN_BLOCK_ROWS)
    assert (t - INPROJ_TAIL) % INPROJ_TM == 0 and INPROJ_TM % INPROJ_TAIL == 0
    split = tn // CAST_TN
    rows = jnp.asarray([r + s * CAST_TN for r in _W_IN_BLOCK_ROWS for s in range(split)], jnp.int32)
    tail_blk = t // INPROJ_TAIL - 1
    extra_piece = lambda j: jnp.clip(j - W_IN_EXTRA * split, 0, split - 1)
    z_tail, w_bf = pl.pallas_call(
        _inproj_cast_kernel,
        grid_spec=pltpu.PrefetchScalarGridSpec(
            num_scalar_prefetch=1,
            grid=(nblk * split,),
            in_specs=[pl.BlockSpec((INPROJ_TAIL, d), lambda j, r: (tail_blk, 0), pipeline_mode=pl.Buffered(1)),
                      _layer_spec(g, layer),
                      pl.BlockSpec((pl.Element(1), pl.Element(CAST_TN), pl.Element(d)),
                                   lambda j, r: (layer, pl.multiple_of(r[j], 16), 0)),
                      pl.BlockSpec((None, CAST_TN, d), lambda j, r: (layer, extra_piece(j), 0))],
            out_specs=[pl.BlockSpec((INPROJ_TAIL, CAST_TN), lambda j, r: (tail_blk, j)),
                       pl.BlockSpec((None, CAST_TN, d), lambda j, r: (j, 0, 0))],
            scratch_shapes=[pltpu.VMEM((INPROJ_TAIL, d), BF16)]),
        out_shape=[jax.ShapeDtypeStruct((t, nblk * tn), BF16),
                   jax.ShapeDtypeStruct((nblk * split, CAST_TN, d), BF16)],
        compiler_params=_cparams(1),
        name="inproj_cast",
    )(rows, x2, g, w_t, w_extra)
    w_bf = w_bf.reshape(nblk, tn, d)
    return pl.pallas_call(
        _inproj_kernel,
        grid=((t - INPROJ_TAIL) // INPROJ_TM, nblk),
        in_specs=[pl.BlockSpec((INPROJ_TM, d), lambda i, j: (i, 0)),
                  _layer_spec(g, layer),
                  pl.BlockSpec((None, tn, d), lambda i, j: (j, 0, 0)),
                  pl.BlockSpec(memory_space=pl.ANY)],
        out_specs=pl.BlockSpec((INPROJ_TM, tn), lambda i, j: (i, j)),
        out_shape=jax.ShapeDtypeStruct((t, nblk * tn), BF16),
        scratch_shapes=[pltpu.VMEM((INPROJ_TM, d), BF16)],
        input_output_aliases={3: 0},
        compiler_params=_cparams(2),
        name="inproj",
    )(x2, g, w_bf, z_tail)


def _w_in_extra(w_t):
    depth, _, d = w_t.shape
    parts = [w_t[:, 2560:2816], w_t[:, 3328:3392], w_t[:, 3360:3392], w_t[:, 3328:3360],
             w_t[:, 6464:6480], jnp.zeros((depth, 128 - GLA_GATE_RANK, d), w_t.dtype)]
    return jnp.concatenate(parts, axis=1)


def _s5_phases(u_ref, gate_ref, wb_ref, lamr_ref, lami_ref, wcr_ref, wci_ref, d_ref,
               wglu_ref, bglu_ref, g_ref, o_ref,
               bre_ref, bim_ref, hre_ref, him_ref, y_ref, ush_ref, ysh_ref, *, nb, tc, pitch):
    shift = pitch - tc
    tcs = tc + 2 * shift

    @pl.when(pl.program_id(0) == 0)
    def _():
        hre_ref[...] = jnp.zeros_like(hre_ref)
        him_ref[...] = jnp.zeros_like(him_ref)
        ush_ref[...] = jnp.zeros_like(ush_ref)

    yield _PHASE
    ub = u_ref[...].reshape(nb * tc, BRANCH)
    ush_ref[:, shift:shift + tc, :] = u_ref[...].astype(F32)
    ubs = ush_ref[...].reshape(nb * tcs, BRANCH).astype(BF16)

    def block_rows(b, half):
        if half:
            return pl.ds((2 * b + 1) * pitch - shift, tcs)
        return pl.ds(2 * b * pitch, tc)

    for kb in range(8):
        half = kb // 4
        lhs, m = (ubs, tcs) if half else (ub, tc)
        res = jnp.dot(lhs[:, kb * 128:(kb + 1) * 128], wb_ref[kb], preferred_element_type=F32)
        for b in range(nb):
            rows = block_rows(b, half)
            for jj in range(4):
                j = (kb % 4) * 4 + jj
                bre_ref[j, rows, :] = res[b * m:(b + 1) * m, jj * 128:(jj + 1) * 128]
                bim_ref[j, rows, :] = res[b * m:(b + 1) * m, 512 + jj * 128:512 + (jj + 1) * 128]
        yield _STEP

    yield _PHASE
    for cb in range(16 // S5_SCAN_W):
        slabs = [cb * S5_SCAN_W + jj for jj in range(S5_SCAN_W)]
        lr = [lamr_ref[:, j * 128:(j + 1) * 128] for j in slabs]
        li = [lami_ref[:, j * 128:(j + 1) * 128] for j in slabs]
        init = []
        for j in slabs:
            init.append(hre_ref[:, j * 128:(j + 1) * 128])
            init.append(him_ref[:, j * 128:(j + 1) * 128])

        def body(t, carry, slabs=slabs, lr=lr, li=li):
            new = []
            idx = pl.ds(t, SUBLANES, stride=pitch)
            for jj, j in enumerate(slabs):
                hr, hi = carry[2 * jj], carry[2 * jj + 1]
                nr = lr[jj] * hr - li[jj] * hi + bre_ref[j, idx, :]
                ni = lr[jj] * hi + li[jj] * hr + bim_ref[j, idx, :]
                bre_ref[j, idx, :] = nr
                bim_ref[j, idx, :] = ni
                new += [nr, ni]
            return tuple(new)

        fin = lax.fori_loop(0, tc, body, tuple(init), unroll=2)
        for jj, j in enumerate(slabs):
            hre_ref[:, j * 128:(j + 1) * 128] = fin[2 * jj]
            him_ref[:, j * 128:(j + 1) * 128] = fin[2 * jj + 1]

    yield _PHASE
    for kb in range(8):
        half = kb // 4
        sre, sim = [], []
        for b in range(nb):
            rows = block_rows(b, half)
            sre.append(jnp.concatenate([bre_ref[(kb % 4) * 4 + jj, rows, :] for jj in range(4)], axis=-1))
            sim.append(jnp.concatenate([bim_ref[(kb % 4) * 4 + jj, rows, :] for jj in range(4)], axis=-1))
        sre = jnp.concatenate(sre, axis=0).astype(BF16)
        sim = jnp.concatenate(sim, axis=0).astype(BF16)
        ykb = (jnp.dot(sre, wcr_ref[kb], preferred_element_type=F32)
               + jnp.dot(sim, wci_ref[kb], preferred_element_type=F32))
        if half:
            ysh_ref[...] = ykb.reshape(nb, tcs, 128)
            ykb = ysh_ref[:, shift:shift + tc, :].reshape(nb * tc, 128)
        cols = slice(kb * 128, (kb + 1) * 128)
        y_ref[:, cols] = ykb + d_ref[:, cols] * ub[:, cols].astype(F32)
        yield _STEP

    y = jax.nn.gelu(y_ref[...], approximate=True)
    zg = jnp.dot(y.astype(BF16), wglu_ref[...], preferred_element_type=F32) + bglu_ref[...]
    y = y * jax.nn.sigmoid(zg)
    gate = gate_ref[...].reshape(nb * tc, BRANCH).astype(F32)
    o_ref[...] = (_rms(y, g_ref[...]) * _silu(gate)).astype(BF16).reshape(nb, tc, BRANCH)


def _s5_params(lam_re, lam_im, b_re, b_im, c_re, c_im, log_dt, nb):
    dt = jnp.exp(log_dt.astype(F32))[:, None]
    lr, li = lam_re.astype(F32), lam_im.astype(F32)
    mag = jnp.exp(lr * dt)
    lbr, lbi = mag * jnp.cos(li * dt), mag * jnp.sin(li * dt)
    den = lr * lr + li * li
    cr = ((lbr - 1.0) * lr + lbi * li) / den
    ci = (lbi * lr - (lbr - 1.0) * li) / den
    bbr = cr[..., None] * b_re - ci[..., None] * b_im
    bbi = cr[..., None] * b_im + ci[..., None] * b_re
    eye = jnp.eye(8, dtype=F32)

    def in_layout(m):
        m = m.reshape(8, 8, S5_STATE, S5_GROUP)
        return jnp.einsum('kgph,gG->kghGp', m, eye).reshape(8, 128, 512)

    def out_layout(m):
        m = m.reshape(8, 8, S5_GROUP, S5_STATE)
        return jnp.einsum('kghp,gG->kgpGh', m, eye).reshape(8, 512, 128)

    wb = jnp.concatenate([in_layout(bbr), in_layout(bbi)], axis=-1).astype(BF16)
    wcr = out_layout(c_re.astype(F32)).astype(BF16)
    wci = out_layout(-c_im.astype(F32)).astype(BF16)

    def seq_layout(v):
        return jnp.tile(v.reshape(1, 2, 2048), (nb, 1, 1)).reshape(2 * nb, 2048)

    return wb, seq_layout(lbr), seq_layout(lbi), wcr, wci


def _lru_phases(u_ref, gate_ref, cw_ref, cb_ref, wax_ref, ba_ref, bx_ref, sp_ref, g_ref, o_ref,
                xpad_ref, a_ref, h_ref, st_ref, *, nb, tc, pitch):
    @pl.when(pl.program_id(0) == 0)
    def _():
        xpad_ref[:, 0:SUBLANES, :] = jnp.zeros((nb, SUBLANES, BRANCH), F32)
        st_ref[...] = jnp.zeros_like(st_ref)

    yield _PHASE
    xpad_ref[:, SUBLANES:SUBLANES + tc, :] = u_ref[...].astype(F32)
    xc = cb_ref[...].reshape(1, 1, BRANCH)
    for k in range(LRU_CONV):
        off = SUBLANES - (LRU_CONV - 1) + k
        xc = xc + cw_ref[k:k + 1, :].reshape(1, 1, BRANCH) * xpad_ref[:, off:off + tc, :]
    xpad_ref[:, 0:SUBLANES, :] = xpad_ref[:, tc:tc + SUBLANES, :]
    x2 = xc.reshape(nb * tc, BRANCH)
    xb = x2.astype(BF16)
    yield _STEP

    for hb in range(LRU_BLOCKS):
        cols = slice(hb * 128, (hb + 1) * 128)
        ri = jnp.dot(xb[:, cols], wax_ref[hb], preferred_element_type=F32)
        r = jax.nn.sigmoid(ri[:, :128] + ba_ref[:, cols])
        ig = jax.nn.sigmoid(ri[:, 128:] + bx_ref[:, cols])
        log_a = (-LRU_C) * r * sp_ref[:, cols]
        a = jnp.exp(log_a)
        gated = jnp.sqrt(-jnp.tanh(log_a) * (a * a + 1.0)) * (ig * x2[:, cols])
        half, j = hb // 4, hb % 4
        for b in range(nb):
            rows = pl.ds((2 * b + half) * pitch, tc)
            a_ref[j, rows, :] = a[b * tc:(b + 1) * tc]
            h_ref[j, rows, :] = gated[b * tc:(b + 1) * tc]
        yield _STEP

    yield _PHASE

    def body(t, carry):
        new = []
        for j in range(4):
            idx = pl.ds(t, SUBLANES, stride=pitch)
            h = a_ref[j, idx, :] * carry[j] + h_ref[j, idx, :]
            h_ref[j, idx, :] = h
            new.append(h)
        return tuple(new)

    init = tuple(st_ref[:, j * 128:(j + 1) * 128] for j in range(4))
    fin = lax.fori_loop(0, tc, body, init, unroll=8)
    for j in range(4):
        st_ref[:, j * 128:(j + 1) * 128] = fin[j]

    yield _PHASE
    for b in range(nb):
        parts = [h_ref[j, pl.ds((2 * b + half) * pitch, tc), :] for half in range(2) for j in range(4)]
        hb_ = jnp.concatenate(parts, axis=-1)
        o_ref[b] = (_rms(hb_, g_ref[...]) * _silu(gate_ref[b].astype(F32))).astype(BF16)
        yield _STEP


_STEP, _PHASE = 0, 1
S5_N_IN, LRU_N_IN = 11, 9
S5_N_SCRATCH = 7


def _s5_lru_kernel(*refs, nb, tc, s5_pitch, lru_pitch):
    s5_in, refs = refs[:S5_N_IN], refs[S5_N_IN:]
    lru_in, refs = refs[:LRU_N_IN], refs[LRU_N_IN:]
    (o_s5, o_lru), refs = refs[:2], refs[2:]
    s5 = _s5_phases(*s5_in, o_s5, *refs[:S5_N_SCRATCH], nb=nb, tc=tc, pitch=s5_pitch)
    lru = _lru_phases(*lru_in, o_lru, *refs[S5_N_SCRATCH:], nb=nb, tc=tc, pitch=lru_pitch)
    _run_interleaved([s5, lru])


def _run_interleaved(gens):
    live = list(gens)
    while live:
        running = list(live)
        while running:
            for g in list(running):
                token = next(g, None)
                if token is None:
                    running.remove(g)
                    live.remove(g)
                elif token == _PHASE:
                    running.remove(g)


def _s5_lru(z3, layer, s5_params, lru_params, *, tc=128):
    nb, seq, _ = z3.shape
    assert 2 * nb == SUBLANES and len(s5_params) == S5_N_IN - 2 and len(lru_params) == LRU_N_IN - 2
    s5_pitch = tc + SUBLANES // 2
    lru_pitch = tc + SUBLANES
    full = functools.partial(_layer_spec, layer=layer)
    zblk = lambda c: pl.BlockSpec((nb, tc, BRANCH), lambda i: (0, i, c))
    oblk = pl.BlockSpec((nb, tc, BRANCH), lambda i: (0, i, 0))
    kern = functools.partial(_s5_lru_kernel, nb=nb, tc=tc, s5_pitch=s5_pitch, lru_pitch=lru_pitch)
    return pl.pallas_call(
        kern,
        grid=(seq // tc,),
        in_specs=([zblk(Z_S5_U), zblk(Z_S5_GATE)] + [full(a) for a in s5_params]
                  + [zblk(Z_LRU_U), zblk(Z_LRU_GATE)] + [full(a) for a in lru_params]),
        out_specs=[oblk, oblk],
        out_shape=[jax.ShapeDtypeStruct((nb, seq, BRANCH), BF16)] * 2,
        scratch_shapes=[pltpu.VMEM((16, SUBLANES * s5_pitch, LANES), F32),
                        pltpu.VMEM((16, SUBLANES * s5_pitch, LANES), F32),
                        pltpu.VMEM((SUBLANES, 2048), F32),
                        pltpu.VMEM((SUBLANES, 2048), F32),
                        pltpu.VMEM((nb * tc, BRANCH), F32),
                        pltpu.VMEM((nb, tc + SUBLANES, BRANCH), F32),
                        pltpu.VMEM((nb, tc + SUBLANES, LANES), F32),
                        pltpu.VMEM((nb, tc + SUBLANES, BRANCH), F32),
                        pltpu.VMEM((4, SUBLANES * lru_pitch, LANES), F32),
                        pltpu.VMEM((4, SUBLANES * lru_pitch, LANES), F32),
                        pltpu.VMEM((SUBLANES, 512), F32)],
        compiler_params=_cparams(1),
        name="s5_rglru",
    )(z3, z3, *s5_params, z3, z3, *lru_params)


def _gla_kernel(q_ref, k_ref, v_ref, glr_ref, gate_ref, wg_ref, bg_ref, og_ref, tril_ref, o_ref,
                st_ref, *, tg):
    @pl.when(pl.program_id(1) == 0)
    def _():
        st_ref[...] = jnp.zeros_like(st_ref)

    n_ch = tg // GLA_CHUNK
    x = jnp.dot(glr_ref[...], wg_ref[...], preferred_element_type=F32) + bg_ref[...]
    log_a = (jnp.minimum(x, 0.0) - jnp.log1p(jnp.exp(-jnp.abs(x)))) / GLA_TAU

    tril = tril_ref[...]
    p0 = log_a.astype(BF16)
    r0 = log_a - p0.astype(F32)
    p1 = r0.astype(BF16)
    p2 = (r0 - p1.astype(F32)).astype(BF16)
    cum = (jnp.dot(tril, p0, preferred_element_type=F32)
           + jnp.dot(tril, p1, preferred_element_type=F32)
           + jnp.dot(tril, p2, preferred_element_type=F32))

    row = lax.broadcasted_iota(jnp.int32, (tg, tg), 0)
    col = lax.broadcasted_iota(jnp.int32, (tg, tg), 1)
    causal = (col <= row) & ((row // GLA_CHUNK) == (col // GLA_CHUNK))
    nt = (((1,), (1,)), ((), ()))
    tn = (((0,), (0,)), ((), ()))

    for hh in range(GLA_HEADS):
        kc = slice(hh * GLA_DK, (hh + 1) * GLA_DK)
        vc = slice(hh * GLA_DV, (hh + 1) * GLA_DV)
        cumh = cum[:, kc]
        kh = k_ref[:, kc].astype(F32)
        qd = (q_ref[:, kc].astype(F32) * (GLA_DK ** -0.5)) * jnp.exp(cumh)
        ki = kh * jnp.exp(-cumh)
        qdb = qd.astype(BF16)
        att = lax.dot_general(qdb, ki.astype(BF16), nt, preferred_element_type=F32)
        att = jnp.where(causal, att, 0.0)
        vb = v_ref[:, vc]
        o = jnp.dot(att.astype(BF16), vb, preferred_element_type=F32)
        st = st_ref[hh]
        inter = []
        for c in range(n_ch):
            rows = slice(c * GLA_CHUNK, (c + 1) * GLA_CHUNK)
            tot = cumh[(c + 1) * GLA_CHUNK - 1:(c + 1) * GLA_CHUNK, :]
            kd = kh[rows] * jnp.exp(tot - cumh[rows])
            inter.append(lax.dot_general(qdb[rows], st.astype(BF16), nt, preferred_element_type=F32))
            kv_t = lax.dot_general(vb[rows], kd.astype(BF16), tn, preferred_element_type=F32)
            st = st * jnp.exp(tot) + kv_t
        st_ref[hh] = st
        o = o + jnp.concatenate(inter, axis=0)
        gate = gate_ref[:, vc].astype(F32)
        o_ref[:, vc] = (_rms(o, og_ref[:, vc]) * _silu(gate)).astype(BF16)


def _gla(z3, layer, wg, bg, og, tril, *, tg=256):
    nb, seq, _ = z3.shape
    kern = functools.partial(_gla_kernel, tg=tg)
    full = functools.partial(_layer_spec, layer=layer)
    return pl.pallas_call(
        kern,
        grid=(nb, seq // tg),
        in_specs=[pl.BlockSpec((None, tg, 512), lambda b, i: (b, i, Z_GLA_Q_512)),
                  pl.BlockSpec((None, tg, 512), lambda b, i: (b, i, Z_GLA_K_512)),
                  pl.BlockSpec((None, tg, BRANCH), lambda b, i: (b, i, Z_GLA_V)),
                  pl.BlockSpec((None, tg, 128), lambda b, i: (b, i, Z_GLR_128)),
                  pl.BlockSpec((None, tg, BRANCH), lambda b, i: (b, i, Z_GLA_GATE)),
                  full(wg), full(bg), full(og), _whole_spec(tril)],
        out_specs=pl.BlockSpec((None, tg, BRANCH), lambda b, i: (b, i, 0)),
        out_shape=jax.ShapeDtypeStruct((nb, seq, BRANCH), BF16),
        scratch_shapes=[pltpu.VMEM((GLA_HEADS, GLA_DV, GLA_DK), F32)],
        compiler_params=_cparams(2),
        name="gla",
    )(z3, z3, z3, z3, z3, wg, bg, og, tril)


def _mla_proj_kernel(cq_ref, ckv_ref, kr_ref, pos_ref, gq_ref, gkv_ref, wq_ref, wk_ref, wvt_ref,
                     invf_ref, sgn_ref, msk_ref, q_out, k_out, vt_out):
    cqn = _rms(cq_ref[...].astype(F32), gq_ref[...]).astype(BF16)
    qa = jnp.dot(cqn, wq_ref[...], preferred_element_type=F32)
    ang = pos_ref[...].astype(F32) * invf_ref[...]
    cosm = jnp.cos(ang) * msk_ref[...]
    sinm = jnp.sin(ang) * sgn_ref[...]
    nh = MLA_HEADS * 128
    half = MLA_ROPE // 2
    lo_half = lax.broadcasted_iota(jnp.int32, (1, 128), 1) < half
    qscale = MLA_SCALE * LOG2_E
    for h in range(MLA_HEADS):
        q_out[:, h * 256:h * 256 + 128] = (qa[:, h * 128:(h + 1) * 128] * qscale).astype(BF16)
        qr = qa[:, nh + h * 128:nh + (h + 1) * 128]
        swapped = jnp.where(lo_half, pltpu.roll(qr, 128 - half, 1), pltpu.roll(qr, half, 1))
        q_out[:, h * 256 + 128:(h + 1) * 256] = ((qr * cosm + swapped * sinm) * qscale).astype(BF16)
    c = _rms(ckv_ref[...].astype(F32), gkv_ref[...]).astype(BF16)
    kn = jnp.dot(c, wk_ref[...], preferred_element_type=F32)
    kr = kr_ref[...].astype(F32)
    krp = (kr * cosm + pltpu.roll(kr, 64, 1) * sinm).astype(BF16)
    for h in range(MLA_HEADS):
        k_out[:, h * 256:h * 256 + 128] = kn[:, h * 128:(h + 1) * 128].astype(BF16)
        k_out[:, h * 256 + 128:(h + 1) * 256] = krp
    nt = (((1,), (1,)), ((), ()))
    vt_out[...] = lax.dot_general(wvt_ref[...], c, nt, preferred_element_type=F32).astype(BF16)


def _mla_proj(z2, pos2, layer, gq, gkv, wq, wk, wvt, invf, sgn, msk, *, tm):
    t = z2.shape[0]
    full = functools.partial(_layer_spec, layer=layer)
    return pl.pallas_call(
        _mla_proj_kernel,
        grid=(t // tm,),
        in_specs=[pl.BlockSpec((tm, MLA_Q_RANK), lambda i: (i, Z_CQ_768)),
                  pl.BlockSpec((tm, MLA_KV_RANK), lambda i: (i, Z_CKV_512)),
                  pl.BlockSpec((tm, 128), lambda i: (i, Z_KR_128)),
                  pl.BlockSpec((tm, 1), lambda i: (i, 0)),
                  full(gq), full(gkv), full(wq), full(wk), full(wvt),
                  _whole_spec(invf), _whole_spec(sgn), _whole_spec(msk)],
        out_specs=[pl.BlockSpec((tm, 2048), lambda i: (i, 0)),
                   pl.BlockSpec((tm, 2048), lambda i: (i, 0)),
                   pl.BlockSpec((None, BRANCH, tm), lambda i: (i, 0, 0))],
        out_shape=[jax.ShapeDtypeStruct((t, 2048), BF16),
                   jax.ShapeDtypeStruct((t, 2048), BF16),
                   jax.ShapeDtypeStruct((t // tm, BRANCH, tm), BF16)],
        compiler_params=_cparams(1),
        name="mla_proj",
    )(z2, z2, z2, pos2, gq, gkv, wq, wk, wvt, invf, sgn, msk)


def _mla_attn_kernel(q_ref, k_ref, vt_ref, gate_ref, g_ref, o_ref, m_ref, l_ref, acc_ref, o_scr, *, tq):
    qi = pl.program_id(1)
    nt = (((1,), (1,)), ((), ()))
    key = lax.broadcasted_iota(jnp.int32, (tq, tq), 0)
    qry = lax.broadcasted_iota(jnp.int32, (tq, tq), 1)

    m_ref[...] = jnp.full(m_ref.shape, NEG_INF, F32)
    l_ref[...] = jnp.zeros_like(l_ref)
    acc_ref[...] = jnp.zeros_like(acc_ref)

    def step(j, masked):
        r0 = pl.multiple_of(j * tq, tq)

        def scores(h):
            st = lax.dot_general(k_ref[pl.ds(r0, tq), h * 256:(h + 1) * 256],
                                 q_ref[:, h * 256:(h + 1) * 256], nt, preferred_element_type=F32)
            if masked:
                st = jnp.where(key <= qry, st, NEG_INF)
            m_old = m_ref[h]
            return st, m_old, jnp.maximum(m_old, jnp.max(st, axis=0, keepdims=True))

        nxt = scores(0)
        for h in range(MLA_HEADS):
            st, m_old, m_new = nxt
            if h + 1 < MLA_HEADS:
                nxt = scores(h + 1)
            alpha = jnp.exp2(m_old - m_new)
            pt = jnp.exp2(st - m_new)
            l_ref[h] = alpha * l_ref[h] + jnp.sum(pt, axis=0, keepdims=True)
            acc_ref[h] = alpha * acc_ref[h] + jnp.dot(
                vt_ref[j, h * 128:(h + 1) * 128, :], pt.astype(BF16), preferred_element_type=F32)
            m_ref[h] = m_new

    def body(j, c):
        step(j, False)
        return c

    lax.fori_loop(0, qi, body, 0)
    step(qi, True)

    for h in range(MLA_HEADS):
        o_scr[:, h * 128:(h + 1) * 128] = (acc_ref[h] / l_ref[h]).T
    o_ref[...] = (_rms(o_scr[...], g_ref[...]) * _silu(gate_ref[...].astype(F32))).astype(BF16)


def _mla_attn(q3, k3, vt4, z3, layer, g, *, tq):
    nb, seq, _ = q3.shape
    kern = functools.partial(_mla_attn_kernel, tq=tq)
    return pl.pallas_call(
        kern,
        grid=(nb, seq // tq),
        in_specs=[pl.BlockSpec((None, tq, 2048), lambda b, i: (b, i, 0)),
                  pl.BlockSpec((None, seq, 2048), lambda b, i: (b, 0, 0)),
                  pl.BlockSpec((None, seq // tq, BRANCH, tq), lambda b, i: (b, 0, 0, 0)),
                  pl.BlockSpec((None, tq, BRANCH), lambda b, i: (b, i, Z_MLA_GATE)),
                  _layer_spec(g, layer)],
        out_specs=pl.BlockSpec((None, tq, BRANCH), lambda b, i: (b, i, 0)),
        out_shape=jax.ShapeDtypeStruct((nb, seq, BRANCH), BF16),
        scratch_shapes=[pltpu.VMEM((MLA_HEADS, 1, tq), F32),
                        pltpu.VMEM((MLA_HEADS, 1, tq), F32),
                        pltpu.VMEM((MLA_HEADS, 128, tq), F32),
                        pltpu.VMEM((tq, BRANCH), F32)],
        compiler_params=_cparams(2),
        name="mla_attn",
    )(q3, k3, vt4, z3, g)


def _outproj_kernel(y0_ref, y1_ref, y2_ref, y3_ref, w_ref, x_ref, o_ref, wb_ref):
    first = pl.program_id(1) == 0
    ys = (y0_ref, y1_ref, y2_ref, y3_ref)

    @pl.when(first)
    def _():
        acc = x_ref[...]
        for n, y_ref in enumerate(ys):
            wb = w_ref[n * BRANCH:(n + 1) * BRANCH, :].astype(BF16)
            wb_ref[n * BRANCH:(n + 1) * BRANCH, :] = wb
            acc = acc + jnp.dot(y_ref[...], wb, preferred_element_type=F32)
        o_ref[...] = acc

    @pl.when(jnp.logical_not(first))
    def _():
        acc = x_ref[...]
        for n, y_ref in enumerate(ys):
            acc = acc + jnp.dot(y_ref[...], wb_ref[n * BRANCH:(n + 1) * BRANCH, :], preferred_element_type=F32)
        o_ref[...] = acc


def _outproj(ys, w_all, layer, x2, *, tm=1024, tn=512):
    t, d = x2.shape
    yspec = pl.BlockSpec((tm, BRANCH), lambda j, i: (i, 0))
    return pl.pallas_call(
        _outproj_kernel,
        grid=(d // tn, t // tm),
        in_specs=[yspec, yspec, yspec, yspec,
                  pl.BlockSpec((None, 4 * BRANCH, tn), lambda j, i: (layer, 0, j)),
                  pl.BlockSpec((tm, tn), lambda j, i: (i, j))],
        out_specs=pl.BlockSpec((tm, tn), lambda j, i: (i, j)),
        out_shape=jax.ShapeDtypeStruct((t, d), F32),
        scratch_shapes=[pltpu.VMEM((4 * BRANCH, tn), BF16)],
        compiler_params=_cparams(2),
        name="outproj",
    )(*ys, w_all, x2)


def _final_norm_kernel(x_ref, g_ref, o_ref):
    o_ref[...] = _rms(x_ref[...], g_ref[...])


def _final_norm(x2, g, *, tm=512):
    t, d = x2.shape
    return pl.pallas_call(
        _final_norm_kernel,
        grid=(t // tm,),
        in_specs=[pl.BlockSpec((tm, d), lambda i: (i, 0)), pl.BlockSpec((1, d), lambda i: (0, 0))],
        out_specs=pl.BlockSpec((tm, d), lambda i: (i, 0)),
        out_shape=jax.ShapeDtypeStruct((t, d), F32),
        compiler_params=_cparams(1),
        name="final_norm",
    )(x2, g)


def _mla_q_weight(w_uq):
    w = w_uq.reshape(MLA_Q_RANK, MLA_HEADS, MLA_NOPE + MLA_ROPE)
    nope = w[:, :, :MLA_NOPE].reshape(MLA_Q_RANK, MLA_HEADS * MLA_NOPE)
    pad = ((0, 0), (0, 0), (0, 128 - MLA_ROPE))
    rope_p = jnp.pad(w[:, :, MLA_NOPE:], pad).reshape(MLA_Q_RANK, MLA_HEADS * 128)
    return jnp.concatenate([nope, rope_p], axis=1).astype(BF16)


def _rope_tables():
    half = MLA_ROPE // 2
    inv = 1.0 / (ROPE_THETA ** (jnp.arange(half, dtype=F32) / half))
    zeros = jnp.zeros((128 - MLA_ROPE,), F32)
    invf = jnp.concatenate([inv, inv, zeros]).reshape(1, 128)
    sgn = jnp.concatenate([-jnp.ones((half,), F32), jnp.ones((half,), F32), zeros]).reshape(1, 128)
    msk = jnp.concatenate([jnp.ones((MLA_ROPE,), F32), zeros]).reshape(1, 128)
    return invf, sgn, msk


def _gla_tril(tg):
    r = np.arange(tg)
    m = (r[None, :] <= r[:, None]) & ((r[:, None] // GLA_CHUNK) == (r[None, :] // GLA_CHUNK))
    return jnp.asarray(m, dtype=BF16)


def kernel(x, positions, norm_g, w_in, s5_lambda_re, s5_lambda_im, s5_b_re, s5_b_im, s5_c_re, s5_c_im,
           s5_d, s5_log_dt, s5_w_glu, s5_b_glu, s5_out_g, mla_q_norm_g, mla_kv_norm_g, mla_w_uq,
           mla_w_uk, mla_w_uv, mla_out_g, gla_w_gate, gla_b_gate, gla_out_g, lru_conv_w, lru_conv_b,
           lru_w_a, lru_b_a, lru_w_x, lru_b_x, lru_lambda, lru_out_g, w_out, final_g):
    nb, seq, d = x.shape
    depth = w_in.shape[0]
    t = nb * seq
    x2 = x.reshape(t, d)
    pos2 = positions.reshape(t, 1)
    invf, sgn, msk = _rope_tables()
    gla_tg = 512
    tril = _gla_tril(gla_tg)

    rows = lambda v: v.reshape(depth, 1, -1).astype(F32)
    w_in_t = jnp.swapaxes(w_in, 1, 2)
    w_in_extra = _w_in_extra(w_in_t)
    norm_g_r = rows(norm_g)
    s5_wb, s5_lamr, s5_lami, s5_wcr, s5_wci = jax.vmap(functools.partial(_s5_params, nb=nb))(
        s5_lambda_re, s5_lambda_im, s5_b_re, s5_b_im, s5_c_re, s5_c_im, s5_log_dt)
    s5_rest = (rows(s5_d), s5_w_glu.astype(BF16), rows(s5_b_glu), rows(s5_out_g))
    mla_w = (rows(mla_q_norm_g), rows(mla_kv_norm_g), jax.vmap(_mla_q_weight)(mla_w_uq),
             mla_w_uk.astype(BF16), jnp.swapaxes(mla_w_uv, 1, 2).astype(BF16))
    mla_out_g_r = rows(mla_out_g)
    gla_w = (jnp.pad(gla_w_gate, ((0, 0), (0, 128 - GLA_GATE_RANK), (0, 0))).astype(BF16),
             rows(gla_b_gate), rows(gla_out_g))
    lru_w = (lru_conv_w.astype(F32), rows(lru_conv_b),
             jnp.concatenate([lru_w_a, lru_w_x], axis=-1).astype(BF16),
             rows(lru_b_a), rows(lru_b_x), rows(jax.nn.softplus(-lru_lambda.astype(F32))),
             rows(lru_out_g))

    for l in range(depth):
        z2 = _inproj(x2, norm_g_r, w_in_t, w_in_extra, l)
        z3 = z2.reshape(nb, seq, Z_COLS)
        y_s5, y_lru = _s5_lru(z3, l, (s5_wb, s5_lamr, s5_lami, s5_wcr, s5_wci) + s5_rest, lru_w)
        q2, k2, vt3 = _mla_proj(z2, pos2, l, *mla_w, invf, sgn, msk, tm=MLA_TQ)
        y_mla = _mla_attn(q2.reshape(nb, seq, 2048), k2.reshape(nb, seq, 2048),
                          vt3.reshape(nb, seq // MLA_TQ, BRANCH, MLA_TQ), z3, l, mla_out_g_r, tq=MLA_TQ)
        y_gla = _gla(z3, l, *gla_w, tril, tg=gla_tg)
        ys = [y.reshape(t, BRANCH) for y in (y_s5, y_mla, y_gla, y_lru)]
        x2 = _outproj(ys, w_out, l, x2)

    return _final_norm(x2, final_g.reshape(1, d).astype(F32)).reshape(nb, seq, d)
```

```python
import functools

import jax
import jax.numpy as jnp
import numpy as np
from jax import lax
from jax.experimental import pallas as pl
from jax.experimental.pallas import tpu as pltpu

F32 = jnp.float32
BF16 = jnp.bfloat16

D_MODEL = 4096
BRANCH = 1024
NORM_EPS = 1e-6

S5_GROUP = 16
S5_GROUPS = 64
S5_STATE = 64

MLA_NOPE = 128
MLA_ROPE = 64
MLA_HEADS = 8
MLA_Q_RANK = 768
MLA_KV_RANK = 512
ROPE_THETA = 10000.0
MLA_SCALE = (MLA_NOPE + MLA_ROPE) ** -0.5
NEG_INF = -1e30
LOG2_E = 1.4426950408889634
MLA_TQ = 512
GLA_HEADS = 4
GLA_DV = 256
GLA_DK = 128
GLA_GATE_RANK = 16
GLA_TAU = 16.0
GLA_CHUNK = 64

LRU_BLOCKS = 8
LRU_BLOCK_W = 128
LRU_CONV = 4
LRU_C = 8.0

LANES = 128
SUBLANES = 8

Z_S5_U, Z_S5_GATE, Z_MLA_GATE, Z_GLA_V, Z_GLA_GATE, Z_LRU_U, Z_LRU_GATE = range(7)
Z_CKV_512 = 14
Z_CQ_768 = 10
Z_KR_128 = 66
Z_GLR_128 = 67
Z_GLA_Q_512 = 17
Z_GLA_K_512 = 18
Z_COLS = 9728

VMEM_LIMIT = 56 * 1024 * 1024
NORM_ROWS = 256
S5_SCAN_W = 8


def _cparams(n_axes):
    return pltpu.CompilerParams(dimension_semantics=("arbitrary",) * n_axes,
                                vmem_limit_bytes=VMEM_LIMIT)


def _layer_spec(a, layer):
    zeros = (0,) * (a.ndim - 1)
    return pl.BlockSpec((None,) + tuple(a.shape[1:]), lambda *_: (layer,) + zeros)


def _whole_spec(a):
    return pl.BlockSpec(a.shape, lambda *_: (0,) * a.ndim)


def _rms(x, g):
    return x * lax.rsqrt(jnp.mean(x * x, axis=-1, keepdims=True) + NORM_EPS) * g


def _silu(x):
    return x * jax.nn.sigmoid(x)


W_IN_TN = 512
W_IN_EXTRA = 16
_W_IN_BLOCK_ROWS = (0, 512, 1024, 1536,
                    3392, 3904,
                    5440, 5952, 6480, 6992,
                    7504, 8016, 8528, 9040,
                    2816, 2048,
                    2048,
                    4416, 4928)
INPROJ_TAIL = 1024
INPROJ_TM = 1024
CAST_TN = 256


def _norm_rows(x_ref, g_ref, h_ref):
    for r in range(x_ref.shape[0] // NORM_ROWS):
        rows = slice(r * NORM_ROWS, (r + 1) * NORM_ROWS)
        h_ref[rows, :] = _rms(x_ref[rows, :], g_ref[...]).astype(BF16)


def _inproj_cast_kernel(rows_ref, x_ref, g_ref, w_ref, wx_ref, o_ref, wb_ref, h_ref):
    del rows_ref
    nt = (((1,), (1,)), ((), ()))
    j = pl.program_id(0)

    @pl.when(j == 0)
    def _():
        _norm_rows(x_ref, g_ref, h_ref)

    def project(w_f32):
        wb = w_f32.astype(BF16)
        wb_ref[...] = wb
        o_ref[...] = lax.dot_general(h_ref[...], wb, nt, preferred_element_type=F32).astype(o_ref.dtype)

    is_extra = j // (W_IN_TN // CAST_TN) == W_IN_EXTRA

    @pl.when(is_extra)
    def _():
        project(wx_ref[...])

    @pl.when(jnp.logical_not(is_extra))
    def _():
        project(w_ref[0])


def _inproj_kernel(x_ref, g_ref, w_ref, zt_ref, o_ref, h_ref):
    nt = (((1,), (1,)), ((), ()))
    tail = pl.program_id(0) == pl.num_programs(0) - 1
    first = jnp.logical_and(pl.program_id(1) == 0, jnp.logical_not(tail))
    rest = jnp.logical_and(pl.program_id(1) != 0, jnp.logical_not(tail))

    @pl.when(tail)
    def _():
        o_ref[...] = zt_ref[...]

    @pl.when(first)
    def _():
        for r in range(x_ref.shape[0] // NORM_ROWS):
            rows = slice(r * NORM_ROWS, (r + 1) * NORM_ROWS)
            h = _rms(x_ref[rows, :], g_ref[...]).astype(BF16)
            h_ref[rows, :] = h
            o_ref[rows, :] = lax.dot_general(h, w_ref[...], nt, preferred_element_type=F32).astype(o_ref.dtype)

    @pl.when(rest)
    def _():
        o_ref[...] = lax.dot_general(h_ref[...], w_ref[...], nt,
                                     preferred_element_type=F32).astype(o_ref.dtype)


def _inproj(x2, g, w_t, w_extra, layer):
    t, d = x2.shape
    tn, nblk = W_IN_TN, len(_W_IN_BLOCK_ROWS)
    assert (t - INPROJ_TAIL) % INPROJ_TM == 0 and INPROJ_TM % INPROJ_TAIL == 0
    split = tn // CAST_TN
    rows = jnp.asarray([r + s * CAST_TN for r in _W_IN_BLOCK_ROWS for s in range(split)], jnp.int32)
    tail_blk = t // INPROJ_TAIL - 1
    extra_piece = lambda j: jnp.clip(j - W_IN_EXTRA * split, 0, split - 1)
    z_tail, w_bf = pl.pallas_call(
        _inproj_cast_kernel,
        grid_spec=pltpu.PrefetchScalarGridSpec(
            num_scalar_prefetch=1,
            grid=(nblk * split,),
            in_specs=[pl.BlockSpec((INPROJ_TAIL, d), lambda j, r: (tail_blk, 0), pipeline_mode=pl.Buffered(1)),
                      _layer_spec(g, layer),
                      pl.BlockSpec((pl.Element(1), pl.Element(CAST_TN), pl.Element(d)),
                                   lambda j, r: (layer, pl.multiple_of(r[j], 16), 0)),
                      pl.BlockSpec((None, CAST_TN, d), lambda j, r: (layer, extra_piece(j), 0))],
            out_specs=[pl.BlockSpec((INPROJ_TAIL, CAST_TN), lambda j, r: (0, j)),
                       pl.BlockSpec((None, CAST_TN, d), lambda j, r: (j, 0, 0))],
            scratch_shapes=[pltpu.VMEM((INPROJ_TAIL, d), BF16)]),
        out_shape=[jax.ShapeDtypeStruct((INPROJ_TAIL, nblk * tn), BF16),
                   jax.ShapeDtypeStruct((nblk * split, CAST_TN, d), BF16)],
        compiler_params=_cparams(1),
        name="inproj_cast",
    )(rows, x2, g, w_t, w_extra)
    w_bf = w_bf.reshape(nblk, tn, d)
    n_main = (t - INPROJ_TAIL) // INPROJ_TM
    return pl.pallas_call(
        _inproj_kernel,
        grid=(n_main + 1, nblk),
        in_specs=[pl.BlockSpec((INPROJ_TM, d), lambda i, j: (jnp.minimum(i, n_main - 1), 0)),
                  _layer_spec(g, layer),
                  pl.BlockSpec((None, tn, d), lambda i, j: (j, 0, 0)),
                  pl.BlockSpec((INPROJ_TAIL, tn), lambda i, j: (0, jnp.where(i == n_main, j, 0)))],
        out_specs=pl.BlockSpec((INPROJ_TM, tn), lambda i, j: (i, j)),
        out_shape=jax.ShapeDtypeStruct((t, nblk * tn), BF16),
        scratch_shapes=[pltpu.VMEM((INPROJ_TM, d), BF16)],
        compiler_params=_cparams(2),
        name="inproj",
    )(x2, g, w_bf, z_tail)


def _w_in_extra(w_t):
    depth, _, d = w_t.shape
    parts = [w_t[:, 2560:2816], w_t[:, 3328:3392], w_t[:, 3360:3392], w_t[:, 3328:3360],
             w_t[:, 6464:6480], jnp.zeros((depth, 128 - GLA_GATE_RANK, d), w_t.dtype)]
    return jnp.concatenate(parts, axis=1)


def _s5_phases(u_ref, gate_ref, wb_ref, lamr_ref, lami_ref, wcr_ref, wci_ref, d_ref,
               wglu_ref, bglu_ref, g_ref, o_ref,
               bre_ref, bim_ref, hre_ref, him_ref, y_ref, ush_ref, ysh_ref, *, nb, tc, pitch):
    shift = pitch - tc
    tcs = tc + 2 * shift

    @pl.when(pl.program_id(0) == 0)
    def _():
        hre_ref[...] = jnp.zeros_like(hre_ref)
        him_ref[...] = jnp.zeros_like(him_ref)
        ush_ref[...] = jnp.zeros_like(ush_ref)

    yield _PHASE
    ub = u_ref[...].reshape(nb * tc, BRANCH)
    ush_ref[:, shift:shift + tc, :] = u_ref[...].astype(F32)
    ubs = ush_ref[...].reshape(nb * tcs, BRANCH).astype(BF16)

    def block_rows(b, half):
        if half:
            return pl.ds((2 * b + 1) * pitch - shift, tcs)
        return pl.ds(2 * b * pitch, tc)

    for kb in range(8):
        half = kb // 4
        lhs, m = (ubs, tcs) if half else (ub, tc)
        res = jnp.dot(lhs[:, kb * 128:(kb + 1) * 128], wb_ref[kb], preferred_element_type=F32)
        for b in range(nb):
            rows = block_rows(b, half)
            for jj in range(4):
                j = (kb % 4) * 4 + jj
                bre_ref[j, rows, :] = res[b * m:(b + 1) * m, jj * 128:(jj + 1) * 128]
                bim_ref[j, rows, :] = res[b * m:(b + 1) * m, 512 + jj * 128:512 + (jj + 1) * 128]
        yield _STEP

    yield _PHASE
    for cb in range(16 // S5_SCAN_W):
        slabs = [cb * S5_SCAN_W + jj for jj in range(S5_SCAN_W)]
        lr = [lamr_ref[:, j * 128:(j + 1) * 128] for j in slabs]
        li = [lami_ref[:, j * 128:(j + 1) * 128] for j in slabs]
        init = []
        for j in slabs:
            init.append(hre_ref[:, j * 128:(j + 1) * 128])
            init.append(him_ref[:, j * 128:(j + 1) * 128])

        def body(t, carry, slabs=slabs, lr=lr, li=li):
            new = []
            idx = pl.ds(t, SUBLANES, stride=pitch)
            for jj, j in enumerate(slabs):
                hr, hi = carry[2 * jj], carry[2 * jj + 1]
                nr = lr[jj] * hr - li[jj] * hi + bre_ref[j, idx, :]
                ni = lr[jj] * hi + li[jj] * hr + bim_ref[j, idx, :]
                bre_ref[j, idx, :] = nr
                bim_ref[j, idx, :] = ni
                new += [nr, ni]
            return tuple(new)

        fin = lax.fori_loop(0, tc, body, tuple(init), unroll=2)
        for jj, j in enumerate(slabs):
            hre_ref[:, j * 128:(j + 1) * 128] = fin[2 * jj]
            him_ref[:, j * 128:(j + 1) * 128] = fin[2 * jj + 1]

    yield _PHASE
    for kb in range(8):
        half = kb // 4
        sre, sim = [], []
        for b in range(nb):
            rows = block_rows(b, half)
            sre.append(jnp.concatenate([bre_ref[(kb % 4) * 4 + jj, rows, :] for jj in range(4)], axis=-1))
            sim.append(jnp.concatenate([bim_ref[(kb % 4) * 4 + jj, rows, :] for jj in range(4)], axis=-1))
        sre = jnp.concatenate(sre, axis=0).astype(BF16)
        sim = jnp.concatenate(sim, axis=0).astype(BF16)
        ykb = (jnp.dot(sre, wcr_ref[kb], preferred_element_type=F32)
               + jnp.dot(sim, wci_ref[kb], preferred_element_type=F32))
        if half:
            ysh_ref[...] = ykb.reshape(nb, tcs, 128)
            ykb = ysh_ref[:, shift:shift + tc, :].reshape(nb * tc, 128)
        cols = slice(kb * 128, (kb + 1) * 128)
        y_ref[:, cols] = ykb + d_ref[:, cols] * ub[:, cols].astype(F32)
        yield _STEP

    y = jax.nn.gelu(y_ref[...], approximate=True)
    zg = jnp.dot(y.astype(BF16), wglu_ref[...], preferred_element_type=F32) + bglu_ref[...]
    y = y * jax.nn.sigmoid(zg)
    gate = gate_ref[...].reshape(nb * tc, BRANCH).astype(F32)
    o_ref[...] = (_rms(y, g_ref[...]) * _silu(gate)).astype(BF16).reshape(nb, tc, BRANCH)


def _s5_params(lam_re, lam_im, b_re, b_im, c_re, c_im, log_dt, nb):
    dt = jnp.exp(log_dt.astype(F32))[:, None]
    lr, li = lam_re.astype(F32), lam_im.astype(F32)
    mag = jnp.exp(lr * dt)
    lbr, lbi = mag * jnp.cos(li * dt), mag * jnp.sin(li * dt)
    den = lr * lr + li * li
    cr = ((lbr - 1.0) * lr + lbi * li) / den
    ci = (lbi * lr - (lbr - 1.0) * li) / den
    bbr = cr[..., None] * b_re - ci[..., None] * b_im
    bbi = cr[..., None] * b_im + ci[..., None] * b_re
    eye = jnp.eye(8, dtype=F32)

    def in_layout(m):
        m = m.reshape(8, 8, S5_STATE, S5_GROUP)
        return jnp.einsum('kgph,gG->kghGp', m, eye).reshape(8, 128, 512)

    def out_layout(m):
        m = m.reshape(8, 8, S5_GROUP, S5_STATE)
        return jnp.einsum('kghp,gG->kgpGh', m, eye).reshape(8, 512, 128)

    wb = jnp.concatenate([in_layout(bbr), in_layout(bbi)], axis=-1).astype(BF16)
    wcr = out_layout(c_re.astype(F32)).astype(BF16)
    wci = out_layout(-c_im.astype(F32)).astype(BF16)

    def seq_layout(v):
        return jnp.tile(v.reshape(1, 2, 2048), (nb, 1, 1)).reshape(2 * nb, 2048)

    return wb, seq_layout(lbr), seq_layout(lbi), wcr, wci


def _lru_phases(u_ref, gate_ref, cw_ref, cb_ref, wax_ref, ba_ref, bx_ref, sp_ref, g_ref, o_ref,
                xpad_ref, a_ref, h_ref, st_ref, *, nb, tc, pitch):
    @pl.when(pl.program_id(0) == 0)
    def _():
        xpad_ref[:, 0:SUBLANES, :] = jnp.zeros((nb, SUBLANES, BRANCH), F32)
        st_ref[...] = jnp.zeros_like(st_ref)

    yield _PHASE
    xpad_ref[:, SUBLANES:SUBLANES + tc, :] = u_ref[...].astype(F32)
    xc = cb_ref[...].reshape(1, 1, BRANCH)
    for k in range(LRU_CONV):
        off = SUBLANES - (LRU_CONV - 1) + k
        xc = xc + cw_ref[k:k + 1, :].reshape(1, 1, BRANCH) * xpad_ref[:, off:off + tc, :]
    xpad_ref[:, 0:SUBLANES, :] = xpad_ref[:, tc:tc + SUBLANES, :]
    x2 = xc.reshape(nb * tc, BRANCH)
    xb = x2.astype(BF16)
    yield _STEP

    for hb in range(LRU_BLOCKS):
        cols = slice(hb * 128, (hb + 1) * 128)
        ri = jnp.dot(xb[:, cols], wax_ref[hb], preferred_element_type=F32)
        r = jax.nn.sigmoid(ri[:, :128] + ba_ref[:, cols])
        ig = jax.nn.sigmoid(ri[:, 128:] + bx_ref[:, cols])
        log_a = (-LRU_C) * r * sp_ref[:, cols]
        a = jnp.exp(log_a)
        gated = jnp.sqrt(-jnp.tanh(log_a) * (a * a + 1.0)) * (ig * x2[:, cols])
        half, j = hb // 4, hb % 4
        for b in range(nb):
            rows = pl.ds((2 * b + half) * pitch, tc)
            a_ref[j, rows, :] = a[b * tc:(b + 1) * tc]
            h_ref[j, rows, :] = gated[b * tc:(b + 1) * tc]
        yield _STEP

    yield _PHASE

    def body(t, carry):
        new = []
        for j in range(4):
            idx = pl.ds(t, SUBLANES, stride=pitch)
            h = a_ref[j, idx, :] * carry[j] + h_ref[j, idx, :]
            h_ref[j, idx, :] = h
            new.append(h)
        return tuple(new)

    init = tuple(st_ref[:, j * 128:(j + 1) * 128] for j in range(4))
    fin = lax.fori_loop(0, tc, body, init, unroll=8)
    for j in range(4):
        st_ref[:, j * 128:(j + 1) * 128] = fin[j]

    yield _PHASE
    for b in range(nb):
        parts = [h_ref[j, pl.ds((2 * b + half) * pitch, tc), :] for half in range(2) for j in range(4)]
        hb_ = jnp.concatenate(parts, axis=-1)
        o_ref[b] = (_rms(hb_, g_ref[...]) * _silu(gate_ref[b].astype(F32))).astype(BF16)
        yield _STEP


_STEP, _PHASE = 0, 1
S5_N_IN, LRU_N_IN = 11, 9
S5_N_SCRATCH = 7


def _s5_lru_kernel(*refs, nb, tc, s5_pitch, lru_pitch):
    s5_in, refs = refs[:S5_N_IN], refs[S5_N_IN:]
    lru_in, refs = refs[:LRU_N_IN], refs[LRU_N_IN:]
    (o_s5, o_lru), refs = refs[:2], refs[2:]
    s5 = _s5_phases(*s5_in, o_s5, *refs[:S5_N_SCRATCH], nb=nb, tc=tc, pitch=s5_pitch)
    lru = _lru_phases(*lru_in, o_lru, *refs[S5_N_SCRATCH:], nb=nb, tc=tc, pitch=lru_pitch)
    _run_interleaved([s5, lru])


def _run_interleaved(gens):
    live = list(gens)
    while live:
        running = list(live)
        while running:
            for g in list(running):
                token = next(g, None)
                if token is None:
                    running.remove(g)
                    live.remove(g)
                elif token == _PHASE:
                    running.remove(g)


def _s5_lru(z3, layer, s5_params, lru_params, *, tc=128):
    nb, seq, _ = z3.shape
    assert 2 * nb == SUBLANES and len(s5_params) == S5_N_IN - 2 and len(lru_params) == LRU_N_IN - 2
    s5_pitch = tc + SUBLANES // 2
    lru_pitch = tc + SUBLANES
    full = functools.partial(_layer_spec, layer=layer)
    zblk = lambda c: pl.BlockSpec((nb, tc, BRANCH), lambda i: (0, i, c))
    oblk = pl.BlockSpec((nb, tc, BRANCH), lambda i: (0, i, 0))
    kern = functools.partial(_s5_lru_kernel, nb=nb, tc=tc, s5_pitch=s5_pitch, lru_pitch=lru_pitch)
    return pl.pallas_call(
        kern,
        grid=(seq // tc,),
        in_specs=([zblk(Z_S5_U), zblk(Z_S5_GATE)] + [full(a) for a in s5_params]
                  + [zblk(Z_LRU_U), zblk(Z_LRU_GATE)] + [full(a) for a in lru_params]),
        out_specs=[oblk, oblk],
        out_shape=[jax.ShapeDtypeStruct((nb, seq, BRANCH), BF16)] * 2,
        scratch_shapes=[pltpu.VMEM((16, SUBLANES * s5_pitch, LANES), F32),
                        pltpu.VMEM((16, SUBLANES * s5_pitch, LANES), F32),
                        pltpu.VMEM((SUBLANES, 2048), F32),
                        pltpu.VMEM((SUBLANES, 2048), F32),
                        pltpu.VMEM((nb * tc, BRANCH), F32),
                        pltpu.VMEM((nb, tc + SUBLANES, BRANCH), F32),
                        pltpu.VMEM((nb, tc + SUBLANES, LANES), F32),
                        pltpu.VMEM((nb, tc + SUBLANES, BRANCH), F32),
                        pltpu.VMEM((4, SUBLANES * lru_pitch, LANES), F32),
                        pltpu.VMEM((4, SUBLANES * lru_pitch, LANES), F32),
                        pltpu.VMEM((SUBLANES, 512), F32)],
        compiler_params=_cparams(1),
        name="s5_rglru",
    )(z3, z3, *s5_params, z3, z3, *lru_params)


def _gla_kernel(q_ref, k_ref, v_ref, glr_ref, gate_ref, wg_ref, bg_ref, og_ref, tril_ref, o_ref,
                st_ref, *, tg):
    @pl.when(pl.program_id(1) == 0)
    def _():
        st_ref[...] = jnp.zeros_like(st_ref)

    n_ch = tg // GLA_CHUNK
    x = jnp.dot(glr_ref[...], wg_ref[...], preferred_element_type=F32) + bg_ref[...]
    log_a = (jnp.minimum(x, 0.0) - jnp.log1p(jnp.exp(-jnp.abs(x)))) / GLA_TAU

    tril = tril_ref[...]
    p0 = log_a.astype(BF16)
    r0 = log_a - p0.astype(F32)
    p1 = r0.astype(BF16)
    p2 = (r0 - p1.astype(F32)).astype(BF16)
    cum = (jnp.dot(tril, p0, preferred_element_type=F32)
           + jnp.dot(tril, p1, preferred_element_type=F32)
           + jnp.dot(tril, p2, preferred_element_type=F32))

    row = lax.broadcasted_iota(jnp.int32, (tg, tg), 0)
    col = lax.broadcasted_iota(jnp.int32, (tg, tg), 1)
    causal = (col <= row) & ((row // GLA_CHUNK) == (col // GLA_CHUNK))
    nt = (((1,), (1,)), ((), ()))
    tn = (((0,), (0,)), ((), ()))

    for hh in range(GLA_HEADS):
        kc = slice(hh * GLA_DK, (hh + 1) * GLA_DK)
        vc = slice(hh * GLA_DV, (hh + 1) * GLA_DV)
        cumh = cum[:, kc]
        kh = k_ref[:, kc].astype(F32)
        qd = (q_ref[:, kc].astype(F32) * (GLA_DK ** -0.5)) * jnp.exp(cumh)
        ki = kh * jnp.exp(-cumh)
        qdb = qd.astype(BF16)
        att = lax.dot_general(qdb, ki.astype(BF16), nt, preferred_element_type=F32)
        att = jnp.where(causal, att, 0.0)
        vb = v_ref[:, vc]
        o = jnp.dot(att.astype(BF16), vb, preferred_element_type=F32)
        st = st_ref[hh]
        inter = []
        for c in range(n_ch):
            rows = slice(c * GLA_CHUNK, (c + 1) * GLA_CHUNK)
            tot = cumh[(c + 1) * GLA_CHUNK - 1:(c + 1) * GLA_CHUNK, :]
            kd = kh[rows] * jnp.exp(tot - cumh[rows])
            inter.append(lax.dot_general(qdb[rows], st.astype(BF16), nt, preferred_element_type=F32))
            kv_t = lax.dot_general(vb[rows], kd.astype(BF16), tn, preferred_element_type=F32)
            st = st * jnp.exp(tot) + kv_t
        st_ref[hh] = st
        o = o + jnp.concatenate(inter, axis=0)
        gate = gate_ref[:, vc].astype(F32)
        o_ref[:, vc] = (_rms(o, og_ref[:, vc]) * _silu(gate)).astype(BF16)


def _gla(z3, layer, wg, bg, og, tril, *, tg=256):
    nb, seq, _ = z3.shape
    kern = functools.partial(_gla_kernel, tg=tg)
    full = functools.partial(_layer_spec, layer=layer)
    return pl.pallas_call(
        kern,
        grid=(nb, seq // tg),
        in_specs=[pl.BlockSpec((None, tg, 512), lambda b, i: (b, i, Z_GLA_Q_512)),
                  pl.BlockSpec((None, tg, 512), lambda b, i: (b, i, Z_GLA_K_512)),
                  pl.BlockSpec((None, tg, BRANCH), lambda b, i: (b, i, Z_GLA_V)),
                  pl.BlockSpec((None, tg, 128), lambda b, i: (b, i, Z_GLR_128)),
                  pl.BlockSpec((None, tg, BRANCH), lambda b, i: (b, i, Z_GLA_GATE)),
                  full(wg), full(bg), full(og), _whole_spec(tril)],
        out_specs=pl.BlockSpec((None, tg, BRANCH), lambda b, i: (b, i, 0)),
        out_shape=jax.ShapeDtypeStruct((nb, seq, BRANCH), BF16),
        scratch_shapes=[pltpu.VMEM((GLA_HEADS, GLA_DV, GLA_DK), F32)],
        compiler_params=_cparams(2),
        name="gla",
    )(z3, z3, z3, z3, z3, wg, bg, og, tril)


def _mla_proj_kernel(cq_ref, ckv_ref, kr_ref, pos_ref, gq_ref, gkv_ref, wq_ref, wk_ref, wvt_ref,
                     invf_ref, sgn_ref, msk_ref, q_out, k_out, vt_out):
    cqn = _rms(cq_ref[...].astype(F32), gq_ref[...]).astype(BF16)
    qa = jnp.dot(cqn, wq_ref[...], preferred_element_type=F32)
    ang = pos_ref[...].astype(F32) * invf_ref[...]
    cosm = jnp.cos(ang) * msk_ref[...]
    sinm = jnp.sin(ang) * sgn_ref[...]
    nh = MLA_HEADS * 128
    half = MLA_ROPE // 2
    lo_half = lax.broadcasted_iota(jnp.int32, (1, 128), 1) < half
    qscale = MLA_SCALE * LOG2_E
    for h in range(MLA_HEADS):
        q_out[:, h * 256:h * 256 + 128] = (qa[:, h * 128:(h + 1) * 128] * qscale).astype(BF16)
        qr = qa[:, nh + h * 128:nh + (h + 1) * 128]
        swapped = jnp.where(lo_half, pltpu.roll(qr, 128 - half, 1), pltpu.roll(qr, half, 1))
        q_out[:, h * 256 + 128:(h + 1) * 256] = ((qr * cosm + swapped * sinm) * qscale).astype(BF16)
    c = _rms(ckv_ref[...].astype(F32), gkv_ref[...]).astype(BF16)
    kn = jnp.dot(c, wk_ref[...], preferred_element_type=F32)
    kr = kr_ref[...].astype(F32)
    krp = (kr * cosm + pltpu.roll(kr, 64, 1) * sinm).astype(BF16)
    for h in range(MLA_HEADS):
        k_out[:, h * 256:h * 256 + 128] = kn[:, h * 128:(h + 1) * 128].astype(BF16)
        k_out[:, h * 256 + 128:(h + 1) * 256] = krp
    nt = (((1,), (1,)), ((), ()))
    vt_out[...] = lax.dot_general(wvt_ref[...], c, nt, preferred_element_type=F32).astype(BF16)


def _mla_proj(z2, pos2, layer, gq, gkv, wq, wk, wvt, invf, sgn, msk, *, tm):
    t = z2.shape[0]
    full = functools.partial(_layer_spec, layer=layer)
    return pl.pallas_call(
        _mla_proj_kernel,
        grid=(t // tm,),
        in_specs=[pl.BlockSpec((tm, MLA_Q_RANK), lambda i: (i, Z_CQ_768)),
                  pl.BlockSpec((tm, MLA_KV_RANK), lambda i: (i, Z_CKV_512)),
                  pl.BlockSpec((tm, 128), lambda i: (i, Z_KR_128)),
                  pl.BlockSpec((tm, 1), lambda i: (i, 0)),
                  full(gq), full(gkv), full(wq), full(wk), full(wvt),
                  _whole_spec(invf), _whole_spec(sgn), _whole_spec(msk)],
        out_specs=[pl.BlockSpec((tm, 2048), lambda i: (i, 0)),
                   pl.BlockSpec((tm, 2048), lambda i: (i, 0)),
                   pl.BlockSpec((None, BRANCH, tm), lambda i: (i, 0, 0))],
        out_shape=[jax.ShapeDtypeStruct((t, 2048), BF16),
                   jax.ShapeDtypeStruct((t, 2048), BF16),
                   jax.ShapeDtypeStruct((t // tm, BRANCH, tm), BF16)],
        compiler_params=_cparams(1),
        name="mla_proj",
    )(z2, z2, z2, pos2, gq, gkv, wq, wk, wvt, invf, sgn, msk)


def _mla_attn_kernel(q_ref, k_ref, vt_ref, gate_ref, g_ref, o_ref, m_ref, l_ref, acc_ref, o_scr, *, tq):
    qi = pl.program_id(1)
    nt = (((1,), (1,)), ((), ()))
    key = lax.broadcasted_iota(jnp.int32, (tq, tq), 0)
    qry = lax.broadcasted_iota(jnp.int32, (tq, tq), 1)

    m_ref[...] = jnp.full(m_ref.shape, NEG_INF, F32)
    l_ref[...] = jnp.zeros_like(l_ref)
    acc_ref[...] = jnp.zeros_like(acc_ref)

    def step(j, masked):
        r0 = pl.multiple_of(j * tq, tq)

        def scores(h):
            st = lax.dot_general(k_ref[pl.ds(r0, tq), h * 256:(h + 1) * 256],
                                 q_ref[:, h * 256:(h + 1) * 256], nt, preferred_element_type=F32)
            if masked:
                st = jnp.where(key <= qry, st, NEG_INF)
            m_old = m_ref[h]
            return st, m_old, jnp.maximum(m_old, jnp.max(st, axis=0, keepdims=True))

        nxt = scores(0)
        for h in range(MLA_HEADS):
            st, m_old, m_new = nxt
            if h + 1 < MLA_HEADS:
                nxt = scores(h + 1)
            alpha = jnp.exp2(m_old - m_new)
            pt = jnp.exp2(st - m_new)
            l_ref[h] = alpha * l_ref[h] + jnp.sum(pt, axis=0, keepdims=True)
            acc_ref[h] = alpha * acc_ref[h] + jnp.dot(
                vt_ref[j, h * 128:(h + 1) * 128, :], pt.astype(BF16), preferred_element_type=F32)
            m_ref[h] = m_new

    def body(j, c):
        step(j, False)
        return c

    lax.fori_loop(0, qi, body, 0)
    step(qi, True)

    for h in range(MLA_HEADS):
        o_scr[:, h * 128:(h + 1) * 128] = (acc_ref[h] / l_ref[h]).T
    o_ref[...] = (_rms(o_scr[...], g_ref[...]) * _silu(gate_ref[...].astype(F32))).astype(BF16)


def _mla_attn(q3, k3, vt4, z3, layer, g, *, tq):
    nb, seq, _ = q3.shape
    kern = functools.partial(_mla_attn_kernel, tq=tq)
    return pl.pallas_call(
        kern,
        grid=(nb, seq // tq),
        in_specs=[pl.BlockSpec((None, tq, 2048), lambda b, i: (b, i, 0)),
                  pl.BlockSpec((None, seq, 2048), lambda b, i: (b, 0, 0)),
                  pl.BlockSpec((None, seq // tq, BRANCH, tq), lambda b, i: (b, 0, 0, 0)),
                  pl.BlockSpec((None, tq, BRANCH), lambda b, i: (b, i, Z_MLA_GATE)),
                  _layer_spec(g, layer)],
        out_specs=pl.BlockSpec((None, tq, BRANCH), lambda b, i: (b, i, 0)),
        out_shape=jax.ShapeDtypeStruct((nb, seq, BRANCH), BF16),
        scratch_shapes=[pltpu.VMEM((MLA_HEADS, 1, tq), F32),
                        pltpu.VMEM((MLA_HEADS, 1, tq), F32),
                        pltpu.VMEM((MLA_HEADS, 128, tq), F32),
                        pltpu.VMEM((tq, BRANCH), F32)],
        compiler_params=_cparams(2),
        name="mla_attn",
    )(q3, k3, vt4, z3, g)


def _outproj_kernel(y0_ref, y1_ref, y2_ref, y3_ref, w_ref, x_ref, o_ref, wb_ref):
    first = pl.program_id(1) == 0
    ys = (y0_ref, y1_ref, y2_ref, y3_ref)

    @pl.when(first)
    def _():
        acc = x_ref[...]
        for n, y_ref in enumerate(ys):
            wb = w_ref[n * BRANCH:(n + 1) * BRANCH, :].astype(BF16)
            wb_ref[n * BRANCH:(n + 1) * BRANCH, :] = wb
            acc = acc + jnp.dot(y_ref[...], wb, preferred_element_type=F32)
        o_ref[...] = acc

    @pl.when(jnp.logical_not(first))
    def _():
        acc = x_ref[...]
        for n, y_ref in enumerate(ys):
            acc = acc + jnp.dot(y_ref[...], wb_ref[n * BRANCH:(n + 1) * BRANCH, :], preferred_element_type=F32)
        o_ref[...] = acc


def _outproj(ys, w_all, layer, x2, *, tm=1024, tn=512):
    t, d = x2.shape
    yspec = pl.BlockSpec((tm, BRANCH), lambda j, i: (i, 0))
    return pl.pallas_call(
        _outproj_kernel,
        grid=(d // tn, t // tm),
        in_specs=[yspec, yspec, yspec, yspec,
                  pl.BlockSpec((None, 4 * BRANCH, tn), lambda j, i: (layer, 0, j)),
                  pl.BlockSpec((tm, tn), lambda j, i: (i, j))],
        out_specs=pl.BlockSpec((tm, tn), lambda j, i: (i, j)),
        out_shape=jax.ShapeDtypeStruct((t, d), F32),
        scratch_shapes=[pltpu.VMEM((4 * BRANCH, tn), BF16)],
        compiler_params=_cparams(2),
        name="outproj",
    )(*ys, w_all, x2)


def _final_norm_kernel(x_ref, g_ref, o_ref):
    o_ref[...] = _rms(x_ref[...], g_ref[...])


def _final_norm(x2, g, *, tm=512):
    t, d = x2.shape
    return pl.pallas_call(
        _final_norm_kernel,
        grid=(t // tm,),
        in_specs=[pl.BlockSpec((tm, d), lambda i: (i, 0)), pl.BlockSpec((1, d), lambda i: (0, 0))],
        out_specs=pl.BlockSpec((tm, d), lambda i: (i, 0)),
        out_shape=jax.ShapeDtypeStruct((t, d), F32),
        compiler_params=_cparams(1),
        name="final_norm",
    )(x2, g)


def _mla_q_weight(w_uq):
    w = w_uq.reshape(MLA_Q_RANK, MLA_HEADS, MLA_NOPE + MLA_ROPE)
    nope = w[:, :, :MLA_NOPE].reshape(MLA_Q_RANK, MLA_HEADS * MLA_NOPE)
    pad = ((0, 0), (0, 0), (0, 128 - MLA_ROPE))
    rope_p = jnp.pad(w[:, :, MLA_NOPE:], pad).reshape(MLA_Q_RANK, MLA_HEADS * 128)
    return jnp.concatenate([nope, rope_p], axis=1).astype(BF16)


def _rope_tables():
    half = MLA_ROPE // 2
    inv = 1.0 / (ROPE_THETA ** (jnp.arange(half, dtype=F32) / half))
    zeros = jnp.zeros((128 - MLA_ROPE,), F32)
    invf = jnp.concatenate([inv, inv, zeros]).reshape(1, 128)
    sgn = jnp.concatenate([-jnp.ones((half,), F32), jnp.ones((half,), F32), zeros]).reshape(1, 128)
    msk = jnp.concatenate([jnp.ones((MLA_ROPE,), F32), zeros]).reshape(1, 128)
    return invf, sgn, msk


def _gla_tril(tg):
    r = np.arange(tg)
    m = (r[None, :] <= r[:, None]) & ((r[:, None] // GLA_CHUNK) == (r[None, :] // GLA_CHUNK))
    return jnp.asarray(m, dtype=BF16)


def kernel(x, positions, norm_g, w_in, s5_lambda_re, s5_lambda_im, s5_b_re, s5_b_im, s5_c_re, s5_c_im,
           s5_d, s5_log_dt, s5_w_glu, s5_b_glu, s5_out_g, mla_q_norm_g, mla_kv_norm_g, mla_w_uq,
           mla_w_uk, mla_w_uv, mla_out_g, gla_w_gate, gla_b_gate, gla_out_g, lru_conv_w, lru_conv_b,
           lru_w_a, lru_b_a, lru_w_x, lru_b_x, lru_lambda, lru_out_g, w_out, final_g):
    nb, seq, d = x.shape
    depth = w_in.shape[0]
    t = nb * seq
    x2 = x.reshape(t, d)
    pos2 = positions.reshape(t, 1)
    invf, sgn, msk = _rope_tables()
    gla_tg = 256
    tril = _gla_tril(gla_tg)

    rows = lambda v: v.reshape(depth, 1, -1).astype(F32)
    w_in_t = jnp.swapaxes(w_in, 1, 2)
    w_in_extra = _w_in_extra(w_in_t)
    norm_g_r = rows(norm_g)
    s5_wb, s5_lamr, s5_lami, s5_wcr, s5_wci = jax.vmap(functools.partial(_s5_params, nb=nb))(
        s5_lambda_re, s5_lambda_im, s5_b_re, s5_b_im, s5_c_re, s5_c_im, s5_log_dt)
    s5_rest = (rows(s5_d), s5_w_glu.astype(BF16), rows(s5_b_glu), rows(s5_out_g))
    mla_w = (rows(mla_q_norm_g), rows(mla_kv_norm_g), jax.vmap(_mla_q_weight)(mla_w_uq),
             mla_w_uk.astype(BF16), jnp.swapaxes(mla_w_uv, 1, 2).astype(BF16))
    mla_out_g_r = rows(mla_out_g)
    gla_w = (jnp.pad(gla_w_gate, ((0, 0), (0, 128 - GLA_GATE_RANK), (0, 0))).astype(BF16),
             rows(gla_b_gate), rows(gla_out_g))
    lru_w = (lru_conv_w.astype(F32), rows(lru_conv_b),
             jnp.concatenate([lru_w_a, lru_w_x], axis=-1).astype(BF16),
             rows(lru_b_a), rows(lru_b_x), rows(jax.nn.softplus(-lru_lambda.astype(F32))),
             rows(lru_out_g))

    for l in range(depth):
        z2 = _inproj(x2, norm_g_r, w_in_t, w_in_extra, l)
        z3 = z2.reshape(nb, seq, Z_COLS)
        y_s5, y_lru = _s5_lru(z3, l, (s5_wb, s5_lamr, s5_lami, s5_wcr, s5_wci) + s5_rest, lru_w)
        q2, k2, vt3 = _mla_proj(z2, pos2, l, *mla_w, invf, sgn, msk, tm=MLA_TQ)
        y_mla = _mla_attn(q2.reshape(nb, seq, 2048), k2.reshape(nb, seq, 2048),
                          vt3.reshape(nb, seq // MLA_TQ, BRANCH, MLA_TQ), z3, l, mla_out_g_r, tq=MLA_TQ)
        y_gla = _gla(z3, l, *gla_w, tril, tg=gla_tg)
        ys = [y.reshape(t, BRANCH) for y in (y_s5, y_mla, y_gla, y_lru)]
        x2 = _outproj(ys, w_out, l, x2)

    return _final_norm(x2, final_g.reshape(1, d).astype(F32)).reshape(nb, seq, d)
```
